```python
import math, functools
import jax, jax.numpy as jnp
from jax import lax
import numpy as np

D_MODEL = 2048
BATCH = 2
SEQ = 4096
DEPTH = 1
DEC_BATCH = 32
DEC_SEQ = 8
PAST_LEN = 8192
PAGE_SIZE = 128

D_MIX = D_MODEL
D_MLSTM = D_MIX // 2
N_MLSTM_HEADS = 4
HD_MLSTM = D_MLSTM // N_MLSTM_HEADS
D_ATT = D_MIX - D_MLSTM
N_ATT_HEADS = 8
HD_ATT = D_ATT // N_ATT_HEADS
DILATED_CONFIGS = ((128, 1), (512, 4), (2048, 16))
MAX_WINDOW = max(w for w, _ in DILATED_CONFIGS)
BAND_BLOCK = 128
MLSTM_CHUNK = 64
CONV_W = 4
N_GROUPS = 4
E_PER_GROUP = 8
N_EXPERTS = N_GROUPS * E_PER_GROUP
TOP_K = 2
D_FF_EXPERT = D_MODEL // 2
ALPHA = (2.0 * DEPTH) ** 0.25
BETA = (8.0 * DEPTH) ** -0.25
EPS = 1e-5
IN_SPLITS = (2 * D_MLSTM, 3 * D_MLSTM, 4 * D_MLSTM, 4 * D_MLSTM + 2 * N_MLSTM_HEADS)
N_IN_COLS = 4 * D_MLSTM + 2 * N_MLSTM_HEADS + 3 * D_ATT

kernel_name = 'hymba_mlstm_dilated_hmoe_step'


def layer_norm(x, g, b):
    xf = x.astype(jnp.float32)
    mu = xf.mean(axis=-1, keepdims=True)
    var = jnp.mean(jnp.square(xf - mu), axis=-1, keepdims=True)
    return ((xf - mu) * lax.rsqrt(var + EPS) * g + b).astype(x.dtype)


def alibi_slopes():
    return jnp.asarray([2.0 ** (-8.0 * (h + 1) / N_ATT_HEADS) for h in range(N_ATT_HEADS)], dtype=jnp.float32)


def in_projection(x, w_in):
    B, T, _ = x.shape
    xp = jnp.einsum('btd,dc->btc', x, w_in)
    qk_raw, v_m, o_m, gates, qkv_a = jnp.split(xp, IN_SPLITS, axis=-1)
    v_m = v_m.reshape(B, T, N_MLSTM_HEADS, HD_MLSTM)
    qkv_a = qkv_a.reshape(B, T, 3, N_ATT_HEADS, HD_ATT)
    return qk_raw, v_m, o_m, gates, qkv_a[:, :, 0], qkv_a[:, :, 1], qkv_a[:, :, 2]


def mlstm_prep(u, gates, conv_w, conv_b, b_gate):
    B = u.shape[0]
    T = u.shape[1] - (CONV_W - 1)
    conv = conv_b
    for j in range(CONV_W):
        conv = conv + u[:, j:j + T] * conv_w[j]
    qk = jax.nn.silu(conv.astype(jnp.float32)).reshape(B, T, 2, N_MLSTM_HEADS, HD_MLSTM)
    q = qk[:, :, 0]
    k = qk[:, :, 1] * (HD_MLSTM ** -0.5)
    g = gates.astype(jnp.float32) + b_gate.astype(jnp.float32)
    ig = g[..., :N_MLSTM_HEADS]
    lf = jax.nn.log_sigmoid(g[..., N_MLSTM_HEADS:])
    return q, k, ig, lf


def mlstm_chunk(carry, inp):
    C, n, m = (a.astype(jnp.float32) for a in carry)
    q, k, v, ig, lf = inp
    L = q.shape[1]
    b = jnp.cumsum(lf, axis=1)
    a = b + m[:, None, :]
    causal = jnp.tril(jnp.ones((L, L), dtype=bool))
    dlog = jnp.where(causal[None, :, :, None],
                     b[:, :, None, :] - b[:, None, :, :] + ig[:, None, :, :], -jnp.inf)
    m_t = jnp.maximum(a, dlog.max(axis=2))
    w_intra = jnp.exp(dlog - m_t[:, :, None, :])
    w_inter = jnp.exp(a - m_t)
    s = jnp.einsum('bthe,bshe->btsh', q, k) * w_intra
    num = jnp.einsum('btsh,bshf->bthf', s, v) + w_inter[..., None] * jnp.einsum('bthe,bhef->bthf', q, C)
    den = s.sum(axis=2) + w_inter * jnp.einsum('bthe,bhe->bth', q, n)
    h = num / jnp.maximum(jnp.abs(den), jnp.exp(-m_t))[..., None]
    b_last = b[:, -1]
    log_w = b_last[:, None, :] - b + ig
    m_new = jnp.maximum(b_last + m, log_w.max(axis=1))
    w_upd = jnp.exp(log_w - m_new[:, None, :])
    decay = jnp.exp(b_last + m - m_new)
    C_new = decay[..., None, None] * C + jnp.einsum('bsh,bshe,bshf->bhef', w_upd, k, v)
    n_new = decay[..., None] * n + jnp.einsum('bsh,bshe->bhe', w_upd, k)
    return (C_new, n_new, m_new), h


def mlstm_prompt(q, k, v, ig, lf):
    B, T, H, E = q.shape
    n_chunks = T // MLSTM_CHUNK
    def to_chunks(a):
        return jnp.moveaxis(a.reshape(B, n_chunks, MLSTM_CHUNK, *a.shape[2:]), 1, 0)
    init = (jnp.zeros((B, H, E, E), jnp.float32), jnp.zeros((B, H, E), jnp.float32), jnp.zeros((B, H), jnp.float32))
    state, h = lax.scan(mlstm_chunk, init, (to_chunks(q), to_chunks(k), to_chunks(v), to_chunks(ig), to_chunks(lf)))
    return state, jnp.moveaxis(h, 0, 1).reshape(B, T, H, E)


def dilated_band_prompt(q, k, v, dilation, n_back, slopes):
    B, T, H, E = q.shape
    L = T // dilation
    n_blk = -(-L // BAND_BLOCK)
    Lp = n_blk * BAND_BLOCK
    def to_blocks(a):
        a = a.reshape(B, L, dilation, H, E).transpose(0, 2, 1, 3, 4)
        a = jnp.pad(a, ((0, 0), (0, 0), (0, Lp - L), (0, 0), (0, 0)))
        return a.reshape(B, dilation, n_blk, BAND_BLOCK, H, E)
    def with_prev(a):
        prev = jnp.concatenate([jnp.zeros_like(a[:, :, :1]), a[:, :, :-1]], axis=2)
        return jnp.concatenate([prev, a], axis=3)
    qb = to_blocks(q)
    kb = with_prev(to_blocks(k))
    vb = with_prev(to_blocks(v))
    s = jnp.einsum('bdnqhe,bdnkhe->bdnhqk', qb, kb).astype(jnp.float32) * (E ** -0.5)
    qi = jnp.arange(BAND_BLOCK)[:, None]
    ci = jnp.arange(2 * BAND_BLOCK)[None, :]
    steps = BAND_BLOCK + qi - ci
    blk = jnp.arange(n_blk)[:, None, None]
    valid = (steps >= 0) & (steps <= n_back) & ((blk > 0) | (ci >= BAND_BLOCK))
    bias = -slopes[:, None, None] * (steps * dilation).astype(jnp.float32)
    s = jnp.where(valid[None, None, :, None], s + bias[None, None, None], -jnp.inf)
    m = s.max(axis=-1)
    p = jnp.exp(s - m[..., None])
    l = p.sum(axis=-1)
    o = jnp.einsum('bdnhqk,bdnkhe->bdnqhe', p, vb.astype(jnp.float32))
    def from_blocks(a):
        rest = a.shape[4:]
        a = a.reshape(B, dilation, Lp, *rest)[:, :, :L]
        return jnp.moveaxis(a, 1, 2).reshape(B, T, *rest)
    m = from_blocks(jnp.moveaxis(m, 3, 4))
    l = from_blocks(jnp.moveaxis(l, 3, 4))
    o = from_blocks(o) / l[..., None]
    return o, m, l


def dilated_gather_sample(q, kc, vc, dilation, n_back, slopes):
    B, S, H, E = q.shape
    W = kc.shape[1] - S
    steps = jnp.arange(n_back + 1)
    idx = W + jnp.arange(S)[:, None] - steps[None, :] * dilation
    valid = idx >= 0
    idx = jnp.maximum(idx, 0)
    kg = kc[:, idx]
    vg = vc[:, idx]
    s = jnp.einsum('bshe,bskhe->bhsk', q, kg).astype(jnp.float32) * (E ** -0.5)
    bias = -slopes[:, None, None] * (steps * dilation).astype(jnp.float32)[None, None, :]
    s = jnp.where(valid[None, None], s + bias, -jnp.inf)
    m = s.max(axis=-1)
    p = jnp.exp(s - m[..., None])
    l = p.sum(axis=-1)
    o = jnp.einsum('bhsk,bskhe->bshe', p, vg.astype(jnp.float32)) / jnp.swapaxes(l, 1, 2)[..., None]
    return o, jnp.swapaxes(m, 1, 2), jnp.swapaxes(l, 1, 2)


def combine_by_denominators(results):
    m_all = functools.reduce(jnp.maximum, [r[1] for r in results])
    ws = [r[2] * jnp.exp(r[1] - m_all) for r in results]
    num = functools.reduce(jnp.add, [w[..., None] * r[0] for w, r in zip(ws, results)])
    den = functools.reduce(jnp.add, ws)
    return num / den[..., None]


def multihead_norm(h, center):
    B, T, H, E = h.shape
    h = h.astype(jnp.float32)
    if center:
        h = h - h.mean(axis=-1, keepdims=True)
    h = h * lax.rsqrt(jnp.mean(jnp.square(h), axis=-1, keepdims=True) + EPS)
    return h.reshape(B, T, H * E)


def out_projection(h_m, o_m, h_a, mh_gain, att_gain, w_out):
    hm = multihead_norm(h_m, True) * mh_gain * jax.nn.sigmoid(o_m.astype(jnp.float32))
    ha = multihead_norm(h_a, False) * att_gain
    cat = jnp.concatenate([hm, ha], axis=-1).astype(w_out.dtype)
    return jnp.einsum('btc,cd->btd', cat, w_out)


def hier_moe(x, w_group, b_group, w_router, b_router, w_gate, w_up, w_down):
    g_prob = jax.nn.softmax(jnp.einsum('nd,dg->ng', x, w_group).astype(jnp.float32) + b_group, axis=-1)
    g_w, g_idx = lax.top_k(g_prob, 1)
    e_logits = jnp.einsum('nd,gde->nge', x, w_router).astype(jnp.float32) + b_router
    e_sel = jnp.take_along_axis(e_logits, g_idx[:, :, None], axis=1)[:, 0]
    e_val, e_idx = lax.top_k(e_sel, TOP_K)
    e_w = jax.nn.softmax(e_val, axis=-1) * g_w
    expert_id = g_idx * E_PER_GROUP + e_idx
    gate = jnp.sum(jax.nn.one_hot(expert_id, N_EXPERTS, dtype=jnp.float32) * e_w[..., None], axis=1)
    out = jnp.zeros(x.shape, jnp.float32)
    for e in range(N_EXPERTS):
        h = jax.nn.silu(x @ w_gate[e]) * (x @ w_up[e])
        out = out + gate[:, e:e + 1] * (h @ w_down[e]).astype(jnp.float32)
    return out.astype(x.dtype)


def post_layer(x, y, ln1_g, ln1_b, w_group, b_group, w_router, b_router, w_gate, w_up, w_down, ln2_g, ln2_b):
    x1 = layer_norm(ALPHA * x + y, ln1_g, ln1_b)
    B, T, D = x1.shape
    f = hier_moe(x1.reshape(B * T, D), w_group, b_group, w_router, b_router, w_gate, w_up, w_down).reshape(B, T, D)
    return layer_norm(ALPHA * x1 + f, ln2_g, ln2_b)


def setup_inputs(seed: int = 0) -> dict:
    key = jax.random.key(seed)
    ks = jax.random.split(key, 26)
    f32 = jnp.float32
    W_BUF = min(MAX_WINDOW, PAST_LEN)
    def nrm(k, shape, scale):
        return jax.random.normal(k, shape, f32) * scale
    b_gate = jnp.concatenate([
        nrm(ks[8], (DEPTH, N_MLSTM_HEADS), 0.1),
        jnp.linspace(3.0, 6.0, N_MLSTM_HEADS, dtype=f32)[None, :] + nrm(ks[9], (DEPTH, N_MLSTM_HEADS), 0.1)], axis=-1)
    return {
        'x_prompt': nrm(ks[0], (BATCH, SEQ, D_MODEL), 1.0),
        'x_sample': nrm(ks[1], (DEC_BATCH, DEC_SEQ, D_MODEL), 1.0),
        'state_conv': nrm(ks[2], (DEPTH, DEC_BATCH, CONV_W - 1, 2 * D_MLSTM), 1.0),
        'state_mlstm_C': nrm(ks[3], (DEPTH, DEC_BATCH, N_MLSTM_HEADS, HD_MLSTM, HD_MLSTM), 0.05),
        'state_mlstm_n': nrm(ks[4], (DEPTH, DEC_BATCH, N_MLSTM_HEADS, HD_MLSTM), 0.1),
        'state_mlstm_m': jax.random.uniform(ks[5], (DEPTH, DEC_BATCH, N_MLSTM_HEADS), f32, 0.0, 3.0),
        'cache_win_k': nrm(ks[6], (DEPTH, DEC_BATCH, W_BUF, N_ATT_HEADS, HD_ATT), 1.0),
        'cache_win_v': nrm(ks[7], (DEPTH, DEC_BATCH, W_BUF, N_ATT_HEADS, HD_ATT), 1.0),
        'w_in': nrm(ks[10], (DEPTH, D_MODEL, N_IN_COLS), D_MODEL ** -0.5),
        'b_gate': b_gate,
        'conv_w': nrm(ks[11], (DEPTH, CONV_W, 2 * D_MLSTM), CONV_W ** -0.5),
        'conv_b': nrm(ks[12], (DEPTH, 2 * D_MLSTM), 0.01),
        'mh_gain': 1.0 + nrm(ks[13], (DEPTH, D_MLSTM), 0.02),
        'att_gain': 1.0 + nrm(ks[14], (DEPTH, D_ATT), 0.02),
        'w_out': nrm(ks[15], (DEPTH, D_MIX, D_MODEL), BETA * D_MIX ** -0.5),
        'ln1_g': 1.0 + nrm(ks[16], (DEPTH, D_MODEL), 0.02),
        'ln1_b': nrm(ks[17], (DEPTH, D_MODEL), 0.02),
        'w_group': nrm(ks[18], (DEPTH, D_MODEL, N_GROUPS), D_MODEL ** -0.5),
        'b_group': nrm(ks[19], (DEPTH, N_GROUPS), 0.01),
        'w_router': nrm(ks[20], (DEPTH, N_GROUPS, D_MODEL, E_PER_GROUP), D_MODEL ** -0.5),
        'b_router': nrm(ks[21], (DEPTH, N_GROUPS, E_PER_GROUP), 0.01),
        'w_gate': nrm(ks[22], (DEPTH, N_EXPERTS, D_MODEL, D_FF_EXPERT), D_MODEL ** -0.5),
        'w_up': nrm(ks[23], (DEPTH, N_EXPERTS, D_MODEL, D_FF_EXPERT), D_MODEL ** -0.5),
        'w_down': nrm(ks[24], (DEPTH, N_EXPERTS, D_FF_EXPERT, D_MODEL), BETA * D_FF_EXPERT ** -0.5),
        'ln2_g': 1.0 + nrm(ks[25], (DEPTH, D_MODEL), 0.02),
        'ln2_b': nrm(jax.random.fold_in(ks[25], 1), (DEPTH, D_MODEL), 0.02),
    }


def reference(x_prompt, x_sample, state_conv, state_mlstm_C, state_mlstm_n, state_mlstm_m, cache_win_k, cache_win_v,
              w_in, b_gate, conv_w, conv_b, mh_gain, att_gain, w_out, ln1_g, ln1_b,
              w_group, b_group, w_router, b_router, w_gate, w_up, w_down, ln2_g, ln2_b):
    slopes = alibi_slopes()
    xp, xs = x_prompt, x_sample
    p_conv, p_C, p_n, p_m, p_wk, p_wv = [], [], [], [], [], []
    s_conv, s_C, s_n, s_m, s_wk, s_wv = [], [], [], [], [], []
    for layer in range(DEPTH):
        moe_args = (ln1_g[layer], ln1_b[layer], w_group[layer], b_group[layer], w_router[layer], b_router[layer],
                    w_gate[layer], w_up[layer], w_down[layer], ln2_g[layer], ln2_b[layer])
        T = xp.shape[1]
        qk_raw, v_m, o_m, gates, q_a, k_a, v_a = in_projection(xp, w_in[layer])
        u = jnp.pad(qk_raw, ((0, 0), (CONV_W - 1, 0), (0, 0)))
        q_m, k_m, ig, lf = mlstm_prep(u, gates, conv_w[layer], conv_b[layer], b_gate[layer])
        (C_p, n_p, m_p), h_m = mlstm_prompt(q_m, k_m, v_m.astype(jnp.float32), ig, lf)
        h_a = combine_by_denominators([dilated_band_prompt(q_a, k_a, v_a, d, w // d, slopes) for w, d in DILATED_CONFIGS])
        y = out_projection(h_m, o_m, h_a, mh_gain[layer], att_gain[layer], w_out[layer])
        win = min(MAX_WINDOW, T)
        p_conv.append(u[:, -(CONV_W - 1):])
        p_C.append(C_p)
        p_n.append(n_p)
        p_m.append(m_p)
        p_wk.append(k_a[:, T - win:])
        p_wv.append(v_a[:, T - win:])
        xp = post_layer(xp, y, *moe_args)
        S = xs.shape[1]
        qk_raw, v_m, o_m, gates, q_a, k_a, v_a = in_projection(xs, w_in[layer])
        u = jnp.concatenate([state_conv[layer].astype(qk_raw.dtype), qk_raw], axis=1)
        q_m, k_m, ig, lf = mlstm_prep(u, gates, conv_w[layer], conv_b[layer], b_gate[layer])
        (C_s, n_s, m_s), h_m = mlstm_chunk((state_mlstm_C[layer], state_mlstm_n[layer], state_mlstm_m[layer]),
                                           (q_m, k_m, v_m.astype(jnp.float32), ig, lf))
        kc = jnp.concatenate([cache_win_k[layer].astype(k_a.dtype), k_a], axis=1)
        vc = jnp.concatenate([cache_win_v[layer].astype(v_a.dtype), v_a], axis=1)
        h_a = combine_by_denominators([dilated_gather_sample(q_a, kc, vc, d, w // d, slopes) for w, d in DILATED_CONFIGS])
        y = out_projection(h_m, o_m, h_a, mh_gain[layer], att_gain[layer], w_out[layer])
        s_conv.append(u[:, -(CONV_W - 1):])
        s_C.append(C_s)
        s_n.append(n_s)
        s_m.append(m_s)
        s_wk.append(kc[:, S:])
        s_wv.append(vc[:, S:])
        xs = post_layer(xs, y, *moe_args)
    return (xp, xs,
            jnp.stack(p_conv), jnp.stack(p_C), jnp.stack(p_n), jnp.stack(p_m), jnp.stack(p_wk), jnp.stack(p_wv),
            jnp.stack(s_conv), jnp.stack(s_C), jnp.stack(s_n), jnp.stack(s_m), jnp.stack(s_wk), jnp.stack(s_wv))
```

```python
import functools
import math

import jax
import jax.numpy as jnp
from jax import lax
from jax.experimental import pallas as pl
from jax.experimental.pallas import tpu as pltpu

F32 = jnp.float32
BF16 = jnp.bfloat16
NEG_INF = float("-inf")

D_MODEL = 2048
D_MLSTM = 1024
N_MH = 4
E_MH = 256
N_AH = 8
E_AH = 128
DILATED_CONFIGS = ((128, 1), (512, 4), (2048, 16))
N_BACK = 128
BAND = 128
CONV_W = 4
N_GROUPS = 4
E_PER_GROUP = 8
N_EXPERTS = 32
D_FF = 1024
EPS = 1e-5
ALPHA = 2.0 ** 0.25

LANE = 128
SUBLANE = 8

COL_QM, COL_KM, COL_VM, COL_OM = 0, 1024, 2048, 3072
COL_QA, COL_KA, COL_VA, COL_G = 4096, 5120, 6144, 7168
N_PROJ = 7296
PROJ_TN = 2432

VMEM_LIMIT = 56 * 1024 * 1024


def _cparams(sem):
    return pltpu.CompilerParams(dimension_semantics=sem, vmem_limit_bytes=VMEM_LIMIT)


def _proj_body(x_ref, w_ref, o_ref):
    o_ref[...] = jnp.dot(x_ref[...].astype(BF16), w_ref[...], preferred_element_type=F32)


def _in_proj(x, w, tm):
    n = x.shape[0]
    return pl.pallas_call(
        _proj_body,
        grid=(N_PROJ // PROJ_TN, n // tm),
        in_specs=[pl.BlockSpec((tm, D_MODEL), lambda j, i: (i, 0)),
                  pl.BlockSpec((D_MODEL, PROJ_TN), lambda j, i: (0, j))],
        out_specs=pl.BlockSpec((tm, PROJ_TN), lambda j, i: (i, j)),
        out_shape=jax.ShapeDtypeStruct((n, N_PROJ), F32),
        compiler_params=_cparams(("arbitrary", "arbitrary")),
    )(x, w)


def _mlstm_body(bg_ref, xq_ref, xk_ref, v_ref, om_ref, g_ref, hq_ref, hk_ref, cwq_ref, cwk_ref,
                cbq_ref, cbk_ref, c0_ref, n0_ref, m0_ref, gain_ref,
                hm_ref, cout_ref, nout_ref, mout_ref,
                caug, m_s, uq, uk, vbuf, *, lb, lc, t_valid):
    h = pl.program_id(1)
    c = pl.program_id(2)
    nc = pl.num_programs(2)
    e = E_MH

    row_e = lax.broadcasted_iota(jnp.int32, (e, e), 0)
    col_e = lax.broadcasted_iota(jnp.int32, (e, e), 1)
    eye_e = row_e == col_e

    @pl.when(c == 0)
    def _init():
        caug[:, 0:e] = c0_ref[...]
        ncol = jnp.sum(jnp.where(eye_e, n0_ref[...], 0.0), axis=1, keepdims=True)
        caug[:, e:e + LANE] = jnp.broadcast_to(ncol, (e, LANE))
        m_s[...] = m0_ref[...]
        uq[...] = jnp.zeros(uq.shape, F32)
        uk[...] = jnp.zeros(uk.shape, F32)
        uq[pl.ds(SUBLANE - (CONV_W - 1), CONV_W - 1), :] = hq_ref[...]
        uk[pl.ds(SUBLANE - (CONV_W - 1), CONV_W - 1), :] = hk_ref[...]
        if lb != lc:
            vbuf[...] = jnp.zeros(vbuf.shape, F32)

    uq[pl.ds(SUBLANE, lb), :] = xq_ref[...]
    uk[pl.ds(SUBLANE, lb), :] = xk_ref[...]

    def conv_silu(u, cw_ref, cb_ref):
        acc = cb_ref[...]
        for j in range(CONV_W):
            acc = acc + u[pl.ds(SUBLANE - (CONV_W - 1) + j, lc), :] * cw_ref[j:j + 1, :]
        return acc * jax.nn.sigmoid(acc)

    q = conv_silu(uq, cwq_ref, cbq_ref)
    k = conv_silu(uk, cwk_ref, cbk_ref) * (e ** -0.5)
    if lb != lc:
        vbuf[pl.ds(0, lb), :] = v_ref[...]
        v = vbuf[...]
    else:
        v = v_ref[...]

    tq = uq[pl.ds(lc + SUBLANE - (CONV_W - 1), CONV_W - 1), :]
    tk = uk[pl.ds(lc + SUBLANE - (CONV_W - 1), CONV_W - 1), :]
    uq[pl.ds(SUBLANE - (CONV_W - 1), CONV_W - 1), :] = tq
    uk[pl.ds(SUBLANE - (CONV_W - 1), CONV_W - 1), :] = tk

    lane_t = lax.broadcasted_iota(jnp.int32, (1, lc), 1)
    valid = (lane_t + c * lc) < t_valid
    gi = g_ref[0:1, :] + bg_ref[h]
    gf = g_ref[1:2, :] + bg_ref[N_MH + h]
    lf = -(jnp.maximum(-gf, 0.0) + jnp.log1p(jnp.exp(-jnp.abs(gf))))
    ig_row = jnp.where(valid, gi, NEG_INF)
    lf_row = jnp.where(valid, lf, 0.0)

    row_l = lax.broadcasted_iota(jnp.int32, (lc, lc), 0)
    col_l = lax.broadcasted_iota(jnp.int32, (lc, lc), 1)
    causal = col_l <= row_l
    eye_l = col_l == row_l
    b_col = jnp.sum(jnp.where(causal, lf_row, 0.0), axis=1, keepdims=True)
    b_row = jnp.sum(jnp.where(eye_l, b_col, 0.0), axis=0, keepdims=True)
    m_prev = m_s[...]

    dlog = jnp.where(causal, b_col - b_row + ig_row, NEG_INF)
    a_col = b_col + m_prev
    m_t = jnp.maximum(a_col, jnp.max(dlog, axis=1, keepdims=True))
    w_intra = jnp.exp(dlog - m_t)
    w_inter = jnp.exp(a_col - m_t)

    qb = q.astype(BF16)
    kb = k.astype(BF16)
    vb = v.astype(BF16)
    s = lax.dot_general(qb, kb, (((1,), (1,)), ((), ())), preferred_element_type=F32) * w_intra
    qc = jnp.dot(qb, caug[...].astype(BF16), preferred_element_type=F32)
    num = jnp.dot(s.astype(BF16), vb, preferred_element_type=F32) + w_inter * qc[:, 0:e]
    den = jnp.sum(s, axis=1, keepdims=True) + w_inter * qc[:, e:e + 1]
    hh = num / jnp.maximum(jnp.abs(den), jnp.exp(-m_t))

    hh = hh - jnp.mean(hh, axis=1, keepdims=True)
    hh = hh * lax.rsqrt(jnp.mean(hh * hh, axis=1, keepdims=True) + EPS)
    if lb != lc:
        hh = hh[0:lb, :]
    hm = hh * gain_ref[...] * jax.nn.sigmoid(om_ref[...])
    hm_ref[...] = hm.astype(hm_ref.dtype)

    b_last = b_row[:, lc - 1:lc]
    logw_row = b_last - b_row + ig_row
    m_new = jnp.maximum(b_last + m_prev, jnp.max(logw_row, axis=1, keepdims=True))
    w_row = jnp.exp(logw_row - m_new)
    w_col = jnp.sum(jnp.where(eye_l, w_row, 0.0), axis=1, keepdims=True)
    decay = jnp.exp(b_last + m_prev - m_new)
    kw = (k * w_col).astype(BF16)
    vaug = jnp.concatenate([vb, jnp.ones((lc, LANE), BF16)], axis=1)
    upd = lax.dot_general(kw, vaug, (((0,), (0,)), ((), ())), preferred_element_type=F32)
    caug[...] = decay * caug[...] + upd
    m_s[...] = m_new

    @pl.when(c == nc - 1)
    def _fin():
        cout_ref[...] = caug[:, 0:e]
        nout_ref[...] = jnp.sum(jnp.where(eye_e, caug[:, e:e + 1], 0.0), axis=0, keepdims=True)
        mout_ref[...] = jnp.broadcast_to(m_s[...], (1, LANE))


def _mlstm(xp, gates_t, hist, conv_w, conv_b, c0, n0, m0, gain, b_gate, *, batch, seq, lb, lc, out_dtype):
    nc = seq // lb
    e = E_MH
    rb = lambda b, h, c, bg: b * nc + c
    qcol, kcol, vcol, ocol = COL_QM // e, COL_KM // e, COL_VM // e, COL_OM // e
    body = functools.partial(_mlstm_body, lb=lb, lc=lc, t_valid=seq if lb == lc else lb)
    grid_spec = pltpu.PrefetchScalarGridSpec(
        num_scalar_prefetch=1,
        grid=(batch, N_MH, nc),
        in_specs=[
            pl.BlockSpec((lb, e), lambda b, h, c, bg: (rb(b, h, c, bg), qcol + h)),
            pl.BlockSpec((lb, e), lambda b, h, c, bg: (rb(b, h, c, bg), kcol + h)),
            pl.BlockSpec((lb, e), lambda b, h, c, bg: (rb(b, h, c, bg), vcol + h)),
            pl.BlockSpec((lb, e), lambda b, h, c, bg: (rb(b, h, c, bg), ocol + h)),
            pl.BlockSpec((None, None, 2, lc), lambda b, h, c, bg: (b, h, 0, c)),
            pl.BlockSpec((None, CONV_W - 1, e), lambda b, h, c, bg: (b, 0, h)),
            pl.BlockSpec((None, CONV_W - 1, e), lambda b, h, c, bg: (b, 0, N_MH + h)),
            pl.BlockSpec((CONV_W, e), lambda b, h, c, bg: (0, h)),
            pl.BlockSpec((CONV_W, e), lambda b, h, c, bg: (0, N_MH + h)),
            pl.BlockSpec((1, e), lambda b, h, c, bg: (0, h)),
            pl.BlockSpec((1, e), lambda b, h, c, bg: (0, N_MH + h)),
            pl.BlockSpec((None, None, e, e), lambda b, h, c, bg: (b, h, 0, 0)),
            pl.BlockSpec((None, None, 1, e), lambda b, h, c, bg: (b, h, 0, 0)),
            pl.BlockSpec((None, None, 1, 1), lambda b, h, c, bg: (b, h, 0, 0)),
            pl.BlockSpec((1, e), lambda b, h, c, bg: (0, h)),
        ],
        out_specs=[
            pl.BlockSpec((lb, e), lambda b, h, c, bg: (rb(b, h, c, bg), h)),
            pl.BlockSpec((None, None, e, e), lambda b, h, c, bg: (b, h, 0, 0)),
            pl.BlockSpec((None, None, 1, e), lambda b, h, c, bg: (b, h, 0, 0)),
            pl.BlockSpec((None, None, 1, LANE), lambda b, h, c, bg: (b, h, 0, 0)),
        ],
        scratch_shapes=[
            pltpu.VMEM((e, e + LANE), F32),
            pltpu.VMEM((1, 1), F32),
            pltpu.VMEM((lc + 2 * SUBLANE, e), F32),
            pltpu.VMEM((lc + 2 * SUBLANE, e), F32),
            pltpu.VMEM((lc, e), F32),
        ],
    )
    return pl.pallas_call(
        body,
        grid_spec=grid_spec,
        out_shape=[
            jax.ShapeDtypeStruct((batch * seq, D_MLSTM), out_dtype),
            jax.ShapeDtypeStruct((batch, N_MH, e, e), F32),
            jax.ShapeDtypeStruct((batch, N_MH, 1, e), F32),
            jax.ShapeDtypeStruct((batch, N_MH, 1, LANE), F32),
        ],
        compiler_params=_cparams(("arbitrary", "arbitrary", "arbitrary")),
    )(b_gate, xp, xp, xp, xp, gates_t, hist, hist, conv_w, conv_w, conv_b, conv_b, c0, n0, m0, gain)


ATT_BLK = 2048


def _attn_body(sl_ref, q_ref, kc_ref, vc_ref, kp_ref, vp_ref, gain_ref, o_ref, o_s, m_s, l_s):
    blk = pl.program_id(1)
    h = pl.program_id(2)
    slope = sl_ref[h]
    scale = E_AH ** -0.5
    qi = lax.broadcasted_iota(jnp.int32, (BAND, BAND), 0)
    ci = lax.broadcasted_iota(jnp.int32, (BAND, BAND), 1)
    dist_prev = (BAND + qi - ci).astype(F32)
    dist_cur = (qi - ci).astype(F32)
    ok_prev = ci >= qi
    ok_cur = ci <= qi

    def unit(cfg, d, prev_pen, q_sl, kp_src, kc_src, vp_src, vc_src):
        sd = slope * float(d)
        bm_prev = jnp.where(ok_prev, prev_pen - sd * dist_prev, NEG_INF)
        bm_cur = jnp.where(ok_cur, -sd * dist_cur, NEG_INF)
        qv = q_ref[q_sl, :].astype(BF16)
        kp = kp_src[0][kp_src[1], :].astype(BF16)
        kc = kc_src[0][kc_src[1], :].astype(BF16)
        nt = (((1,), (1,)), ((), ()))
        s_p = lax.dot_general(qv, kp, nt, preferred_element_type=F32) * scale + bm_prev
        s_c = lax.dot_general(qv, kc, nt, preferred_element_type=F32) * scale + bm_cur
        m = jnp.maximum(jnp.max(s_p, axis=1, keepdims=True), jnp.max(s_c, axis=1, keepdims=True))
        p_p = jnp.exp(s_p - m)
        p_c = jnp.exp(s_c - m)
        l = jnp.sum(p_p, axis=1, keepdims=True) + jnp.sum(p_c, axis=1, keepdims=True)
        vp = vp_src[0][vp_src[1], :].astype(BF16)
        vc = vc_src[0][vc_src[1], :].astype(BF16)
        o = (jnp.dot(p_p.astype(BF16), vp, preferred_element_type=F32)
             + jnp.dot(p_c.astype(BF16), vc, preferred_element_type=F32))
        o_s[cfg, q_sl, :] = o
        m_s[cfg, q_sl, :] = jnp.broadcast_to(m, (BAND, E_AH))
        l_s[cfg, q_sl, :] = jnp.broadcast_to(l, (BAND, E_AH))

    first_pen = jnp.where(blk == 0, NEG_INF, 0.0)

    for cfg, (win, d) in enumerate(DILATED_CONFIGS):
        span = BAND * d
        n_u = ATT_BLK // span

        def sl(r, u, _d=d, _span=span):
            if _d == 1:
                if isinstance(u, int):
                    return pl.ds(u * _span, BAND)
                return pl.ds(pl.multiple_of(u * _span, BAND), BAND)
            return pl.ds(r + u * _span, BAND, stride=_d)

        def head_unit(r, _, cfg=cfg, d=d, span=span, sl=sl):
            cur = sl(r, 0)
            prv = sl(r + ATT_BLK - span, 0) if d > 1 else pl.ds(ATT_BLK - span, BAND)
            unit(cfg, d, first_pen, cur, (kp_ref, prv), (kc_ref, cur), (vp_ref, prv), (vc_ref, cur))
            return 0

        if d == 1:
            head_unit(0, 0)
        else:
            lax.fori_loop(0, d, head_unit, 0)

        if n_u > 1:
            def tail_unit(i, _, cfg=cfg, d=d, sl=sl, n_u=n_u):
                r = i // (n_u - 1)
                u = i % (n_u - 1) + 1
                cur = sl(r, u)
                prv = sl(r, u - 1)
                unit(cfg, d, 0.0, cur, (kc_ref, prv), (kc_ref, cur), (vc_ref, prv), (vc_ref, cur))
                return 0

            lax.fori_loop(0, d * (n_u - 1), tail_unit, 0)

    rows = 256

    def merge(i, _):
        rs = pl.ds(pl.multiple_of(i * rows, rows), rows)
        m0, m1, m2 = m_s[0, rs, :], m_s[1, rs, :], m_s[2, rs, :]
        m_all = jnp.maximum(jnp.maximum(m0, m1), m2)
        e0, e1, e2 = jnp.exp(m0 - m_all), jnp.exp(m1 - m_all), jnp.exp(m2 - m_all)
        num = e0 * o_s[0, rs, :] + e1 * o_s[1, rs, :] + e2 * o_s[2, rs, :]
        den = e0 * l_s[0, rs, :] + e1 * l_s[1, rs, :] + e2 * l_s[2, rs, :]
        ha = num / den
        ha = ha * lax.rsqrt(jnp.mean(ha * ha, axis=1, keepdims=True) + EPS)
        o_ref[rs, :] = (ha * gain_ref[...]).astype(o_ref.dtype)
        return 0

    lax.fori_loop(0, ATT_BLK // rows, merge, 0)


def _attn_prompt(xp, gain, slopes, *, batch, seq):
    nb = seq // ATT_BLK
    qc, kc, vc = COL_QA // E_AH, COL_KA // E_AH, COL_VA // E_AH
    cur = lambda b, i, h, s: b * nb + i
    prev = lambda b, i, h, s: b * nb + jnp.maximum(i - 1, 0)
    grid_spec = pltpu.PrefetchScalarGridSpec(
        num_scalar_prefetch=1,
        grid=(batch, nb, N_AH),
        in_specs=[
            pl.BlockSpec((ATT_BLK, E_AH), lambda b, i, h, s: (cur(b, i, h, s), qc + h)),
            pl.BlockSpec((ATT_BLK, E_AH), lambda b, i, h, s: (cur(b, i, h, s), kc + h)),
            pl.BlockSpec((ATT_BLK, E_AH), lambda b, i, h, s: (cur(b, i, h, s), vc + h)),
            pl.BlockSpec((ATT_BLK, E_AH), lambda b, i, h, s: (prev(b, i, h, s), kc + h)),
            pl.BlockSpec((ATT_BLK, E_AH), lambda b, i, h, s: (prev(b, i, h, s), vc + h)),
            pl.BlockSpec((1, E_AH), lambda b, i, h, s: (0, h)),
        ],
        out_specs=pl.BlockSpec((ATT_BLK, E_AH), lambda b, i, h, s: (cur(b, i, h, s), h)),
        scratch_shapes=[pltpu.VMEM((3, ATT_BLK, E_AH), F32)] * 3,
    )
    return pl.pallas_call(
        _attn_body,
        grid_spec=grid_spec,
        out_shape=jax.ShapeDtypeStruct((batch * seq, N_AH * E_AH), BF16),
        compiler_params=_cparams(("arbitrary", "arbitrary", "arbitrary")),
    )(slopes, xp, xp, xp, xp, xp, gain)


DEC_HB = 4
DEC_PAD = 128


def _decode_body(sl_ref, q_ref, kn_ref, vn_ref, kc_ref, vc_ref, gain_ref,
                 ha_ref, ko_ref, vo_ref, kfull, vfull, *, w_buf, s_new):
    g = pl.program_id(1)
    cw = DEC_HB * E_AH
    n_keys = w_buf + DEC_PAD
    nr = DEC_HB * s_new

    ko_ref[pl.ds(0, w_buf - s_new), :] = kc_ref[pl.ds(s_new, w_buf - s_new), :]
    vo_ref[pl.ds(0, w_buf - s_new), :] = vc_ref[pl.ds(s_new, w_buf - s_new), :]
    ko_ref[pl.ds(w_buf - s_new, s_new), :] = kn_ref[...]
    vo_ref[pl.ds(w_buf - s_new, s_new), :] = vn_ref[...]

    kfull[pl.ds(0, w_buf), :] = kc_ref[...].astype(BF16)
    vfull[pl.ds(0, w_buf), :] = vc_ref[...].astype(BF16)
    pad_k = jnp.concatenate([kn_ref[...], jnp.zeros((DEC_PAD - s_new, cw), F32)], axis=0)
    pad_v = jnp.concatenate([vn_ref[...], jnp.zeros((DEC_PAD - s_new, cw), F32)], axis=0)
    kfull[pl.ds(w_buf, DEC_PAD), :] = pad_k.astype(BF16)
    vfull[pl.ds(w_buf, DEC_PAD), :] = pad_v.astype(BF16)

    qrow = lax.broadcasted_iota(jnp.int32, (nr, cw), 0)
    qcol = lax.broadcasted_iota(jnp.int32, (nr, cw), 1)
    qt = jnp.concatenate([q_ref[...]] * DEC_HB, axis=0)
    qexp = jnp.where(qrow // s_new == qcol // E_AH, qt, 0.0).astype(BF16)
    s = lax.dot_general(qexp, kfull[...], (((1,), (1,)), ((), ())), preferred_element_type=F32)

    key = lax.broadcasted_iota(jnp.int32, (nr, n_keys), 1)
    rsmall = lax.broadcasted_iota(jnp.int32, (nr, 1), 0)
    dist = (w_buf + rsmall % s_new) - key
    slope_col = jnp.zeros((nr, 1), F32)
    for hl in range(DEC_HB):
        slope_col = jnp.where(rsmall // s_new == hl, sl_ref[g * DEC_HB + hl], slope_col)
    sb = s * (E_AH ** -0.5) - slope_col * dist.astype(F32)
    cnt = jnp.zeros((nr, n_keys), F32)
    for win, d in DILATED_CONFIGS:
        on_grid = jnp.bitwise_and(dist, d - 1) == 0
        ok = jnp.logical_and(dist >= 0, jnp.logical_and(on_grid, dist <= win))
        cnt = cnt + jnp.where(ok, 1.0, 0.0)
    sm = jnp.where(cnt > 0.0, sb, NEG_INF)
    m = jnp.max(sm, axis=1, keepdims=True)
    p = jnp.exp(sm - m) * cnt
    den = jnp.sum(p, axis=1, keepdims=True)
    o = jnp.dot(p.astype(BF16), vfull[...], preferred_element_type=F32) / den
    outs = []
    for hl in range(DEC_HB):
        oh = o[hl * s_new:(hl + 1) * s_new, hl * E_AH:(hl + 1) * E_AH]
        oh = oh * lax.rsqrt(jnp.mean(oh * oh, axis=1, keepdims=True) + EPS)
        outs.append(oh)
    ha_ref[...] = jnp.concatenate(outs, axis=1) * gain_ref[...]


def _attn_decode(xs, cache_k, cache_v, gain, slopes, *, batch, s_new):
    w_buf = cache_k.shape[1]
    cw = DEC_HB * E_AH
    ng = N_AH // DEC_HB
    qc, kc, vc = COL_QA // cw, COL_KA // cw, COL_VA // cw
    body = functools.partial(_decode_body, w_buf=w_buf, s_new=s_new)
    grid_spec = pltpu.PrefetchScalarGridSpec(
        num_scalar_prefetch=1,
        grid=(batch, ng),
        in_specs=[
            pl.BlockSpec((s_new, cw), lambda b, g, s: (b, qc + g)),
            pl.BlockSpec((s_new, cw), lambda b, g, s: (b, kc + g)),
            pl.BlockSpec((s_new, cw), lambda b, g, s: (b, vc + g)),
            pl.BlockSpec((None, w_buf, cw), lambda b, g, s: (b, 0, g)),
            pl.BlockSpec((None, w_buf, cw), lambda b, g, s: (b, 0, g)),
            pl.BlockSpec((1, cw), lambda b, g, s: (0, g)),
        ],
        out_specs=[
            pl.BlockSpec((s_new, cw), lambda b, g, s: (b, g)),
            pl.BlockSpec((None, w_buf, cw), lambda b, g, s: (b, 0, g)),
            pl.BlockSpec((None, w_buf, cw), lambda b, g, s: (b, 0, g)),
        ],
        scratch_shapes=[pltpu.VMEM((w_buf + DEC_PAD, cw), BF16)] * 2,
    )
    return pl.pallas_call(
        body,
        grid_spec=grid_spec,
        out_shape=[
            jax.ShapeDtypeStruct((batch * s_new, N_AH * E_AH), F32),
            jax.ShapeDtypeStruct(cache_k.shape, F32),
            jax.ShapeDtypeStruct(cache_v.shape, F32),
        ],
        compiler_params=_cparams(("arbitrary", "arbitrary")),
    )(slopes, xs, xs, xs, cache_k, cache_v, gain)


OP_TM = 256
ROUTE_W = LANE


def _layer_norm(z, g, b):
    mu = jnp.mean(z, axis=1, keepdims=True)
    zc = z - mu
    var = jnp.mean(zc * zc, axis=1, keepdims=True)
    return zc * lax.rsqrt(var + EPS) * g + b


def _outproj_body(hm_ref, ha_ref, x_ref, wm_ref, wa_ref, g_ref, b_ref, wr_ref, br_ref, x1_ref, route_ref):
    y = (jnp.dot(hm_ref[...].astype(BF16), wm_ref[...], preferred_element_type=F32)
         + jnp.dot(ha_ref[...].astype(BF16), wa_ref[...], preferred_element_type=F32))
    x1 = _layer_norm(ALPHA * x_ref[...] + y, g_ref[...], b_ref[...])
    x1_ref[...] = x1
    logits = jnp.dot(x1, wr_ref[...], preferred_element_type=F32, precision=lax.Precision.HIGHEST) + br_ref[...]
    tm = logits.shape[0]
    lane = lax.broadcasted_iota(jnp.int32, (tm, ROUTE_W), 1)
    lane_f = lane.astype(F32)
    big = float(ROUTE_W)
    gl = jnp.where(lane < N_GROUPS, logits, NEG_INF)
    gmax = jnp.max(gl, axis=1, keepdims=True)
    g_w = 1.0 / jnp.sum(jnp.exp(gl - gmax), axis=1, keepdims=True)
    g_idx = jnp.min(jnp.where(gl == gmax, lane_f, big), axis=1, keepdims=True)
    lo = N_GROUPS + E_PER_GROUP * g_idx
    el = jnp.where(jnp.logical_and(lane_f >= lo, lane_f < lo + E_PER_GROUP), logits, NEG_INF)
    v1 = jnp.max(el, axis=1, keepdims=True)
    i1 = jnp.min(jnp.where(el == v1, lane_f, big), axis=1, keepdims=True)
    el2 = jnp.where(lane_f == i1, NEG_INF, el)
    v2 = jnp.max(el2, axis=1, keepdims=True)
    i2 = jnp.min(jnp.where(el2 == v2, lane_f, big), axis=1, keepdims=True)
    e2 = jnp.exp(v2 - v1)
    w1 = g_w / (1.0 + e2)
    w2 = g_w * e2 / (1.0 + e2)
    route = jnp.where(lane == 0, i1 - N_GROUPS,
                      jnp.where(lane == 1, i2 - N_GROUPS,
                                jnp.where(lane == 2, w1, jnp.where(lane == 3, w2, 0.0))))
    route_ref[...] = route


def _outproj(hm, ha, x, wm, wa, g, b, wr, br):
    n = x.shape[0]
    tm = OP_TM
    row = lambda i: (i, 0)
    fixed = lambda i: (0, 0)
    return pl.pallas_call(
        _outproj_body,
        grid=(n // tm,),
        in_specs=[
            pl.BlockSpec((tm, D_MLSTM), row),
            pl.BlockSpec((tm, N_AH * E_AH), row),
            pl.BlockSpec((tm, D_MODEL), row),
            pl.BlockSpec((D_MLSTM, D_MODEL), fixed),
            pl.BlockSpec((N_AH * E_AH, D_MODEL), fixed),
            pl.BlockSpec((1, D_MODEL), fixed),
            pl.BlockSpec((1, D_MODEL), fixed),
            pl.BlockSpec((D_MODEL, ROUTE_W), fixed),
            pl.BlockSpec((1, ROUTE_W), fixed),
        ],
        out_specs=[pl.BlockSpec((tm, D_MODEL), row), pl.BlockSpec((tm, ROUTE_W), row)],
        out_shape=[jax.ShapeDtypeStruct((n, D_MODEL), F32), jax.ShapeDtypeStruct((n, ROUTE_W), F32)],
        compiler_params=_cparams(("arbitrary",)),
    )(hm, ha, x, wm, wa, g, b, wr, br)


GATHER_TM = 128


def _gather_issue(src_ref, x_hbm, buf, sem, tile, slot, tm):
    def body(r, _):
        tok = src_ref[tile * tm + r]
        pltpu.make_async_copy(x_hbm.at[pl.ds(tok, 1), :], buf.at[slot, pl.ds(r, 1), :], sem.at[slot]).start()
        return 0

    lax.fori_loop(0, tm, body, 0)


def _gather_body(src_ref, x_hbm, o_ref, buf, sem, *, tm):
    i = pl.program_id(0)
    n = pl.num_programs(0)
    slot = i % 2

    @pl.when(i == 0)
    def _():
        _gather_issue(src_ref, x_hbm, buf, sem, 0, 0, tm)

    @pl.when(i + 1 < n)
    def _():
        _gather_issue(src_ref, x_hbm, buf, sem, i + 1, 1 - slot, tm)

    pltpu.make_async_copy(x_hbm.at[pl.ds(0, tm), :], buf.at[slot], sem.at[slot]).wait()
    o_ref[...] = buf[slot].astype(o_ref.dtype)


def _gather_rows(src, x1, n_rows):
    tm = GATHER_TM
    d = x1.shape[1]
    grid_spec = pltpu.PrefetchScalarGridSpec(
        num_scalar_prefetch=1,
        grid=(n_rows // tm,),
        in_specs=[pl.BlockSpec(memory_space=pl.ANY)],
        out_specs=pl.BlockSpec((tm, d), lambda i, s: (i, 0)),
        scratch_shapes=[pltpu.VMEM((2, tm, d), F32), pltpu.SemaphoreType.DMA((2,))],
    )
    return pl.pallas_call(
        functools.partial(_gather_body, tm=tm),
        grid_spec=grid_spec,
        out_shape=jax.ShapeDtypeStruct((n_rows, d), BF16),
        compiler_params=_cparams(("arbitrary",)),
    )(src, x1)


MOE_R = 128
MOE_MAXB = 8
MOE_FC = 256
MOE_NC = D_FF // MOE_FC


def _moe_body(ex_ref, row0_ref, nb_ref, xs_hbm, wg_ref, wu_ref, wd_ref, ys_hbm, xbuf, acc, sem_in, sem_out):
    t = pl.program_id(0)
    c = pl.program_id(1)
    nb = nb_ref[t]
    row0 = row0_ref[t]

    def in_copy(r):
        return pltpu.make_async_copy(xs_hbm.at[pl.ds(pl.multiple_of(row0 + r * MOE_R, MOE_R), MOE_R), :],
                                     xbuf.at[pl.ds(pl.multiple_of(r * MOE_R, MOE_R), MOE_R), :], sem_in)

    def out_copy(r):
        return pltpu.make_async_copy(acc.at[pl.ds(pl.multiple_of(r * MOE_R, MOE_R), MOE_R), :],
                                     ys_hbm.at[pl.ds(pl.multiple_of(row0 + r * MOE_R, MOE_R), MOE_R), :], sem_out)

    @pl.when(c == 0)
    def _load():
        def start(r, _):
            in_copy(r).start()
            return 0

        def wait(r, _):
            in_copy(r).wait()
            return 0

        lax.fori_loop(0, nb, start, 0)
        lax.fori_loop(0, nb, wait, 0)

    wg = wg_ref[...].astype(BF16)
    wu = wu_ref[...].astype(BF16)
    wd = wd_ref[...].astype(BF16)

    def block(r, _):
        rs = pl.ds(pl.multiple_of(r * MOE_R, MOE_R), MOE_R)
        x = xbuf[rs, :]
        gt = jnp.dot(x, wg, preferred_element_type=F32)
        up = jnp.dot(x, wu, preferred_element_type=F32)
        hid = (gt * jax.nn.sigmoid(gt) * up).astype(BF16)
        y = jnp.dot(hid, wd, preferred_element_type=F32)

        @pl.when(c == 0)
        def _():
            acc[rs, :] = y

        @pl.when(c > 0)
        def _():
            acc[rs, :] = acc[rs, :] + y

        return 0

    lax.fori_loop(0, nb, block, 0)

    nz = jnp.maximum(-nb, 0)

    @pl.when(jnp.logical_and(c == 0, nz > 0))
    def _zero():
        acc[...] = jnp.zeros(acc.shape, F32)

    @pl.when(c == MOE_NC - 1)
    def _store():
        def start(r, _):
            out_copy(r).start()
            return 0

        def wait(r, _):
            out_copy(r).wait()
            return 0

        n_out = jnp.maximum(nb, nz)
        lax.fori_loop(0, n_out, start, 0)
        lax.fori_loop(0, n_out, wait, 0)


def _moe(item_e, item_row0, item_nb, xs, w_gate, w_up, w_down, n_items):
    n_rows = xs.shape[0]

    def chunk(t, c, n):
        return jnp.where(n[t] > 0, c, MOE_NC - 1)

    grid_spec = pltpu.PrefetchScalarGridSpec(
        num_scalar_prefetch=3,
        grid=(n_items, MOE_NC),
        in_specs=[
            pl.BlockSpec(memory_space=pl.ANY),
            pl.BlockSpec((None, D_MODEL, MOE_FC), lambda t, c, e, r, n: (e[t], 0, chunk(t, c, n))),
            pl.BlockSpec((None, D_MODEL, MOE_FC), lambda t, c, e, r, n: (e[t], 0, chunk(t, c, n))),
            pl.BlockSpec((None, MOE_FC, D_MODEL), lambda t, c, e, r, n: (e[t], chunk(t, c, n), 0)),
        ],
        out_specs=pl.BlockSpec(memory_space=pl.ANY),
        scratch_shapes=[
            pltpu.VMEM((MOE_MAXB * MOE_R, D_MODEL), BF16),
            pltpu.VMEM((MOE_MAXB * MOE_R, D_MODEL), F32),
            pltpu.SemaphoreType.DMA,
            pltpu.SemaphoreType.DMA,
        ],
    )
    return pl.pallas_call(
        _moe_body,
        grid_spec=grid_spec,
        out_shape=jax.ShapeDtypeStruct((n_rows, D_MODEL), F32),
        compiler_params=_cparams(("arbitrary", "arbitrary")),
    )(item_e, item_row0, item_nb, xs, w_gate, w_up, w_down)


COMB_TM = 128


def _comb_issue(pos_ref, ys_hbm, buf, sem, tile, slot, tm):
    def body(r, _):
        for k in range(2):
            p = pos_ref[2 * (tile * tm + r) + k]
            pltpu.make_async_copy(ys_hbm.at[pl.ds(p, 1), :], buf.at[slot, k, pl.ds(r, 1), :], sem.at[slot]).start()
        return 0

    lax.fori_loop(0, tm, body, 0)


def _combine_body(pos_ref, ys_hbm, x1_ref, route_ref, g_ref, b_ref, o_ref, buf, sem, *, tm):
    i = pl.program_id(0)
    n = pl.num_programs(0)
    slot = i % 2

    @pl.when(i == 0)
    def _():
        _comb_issue(pos_ref, ys_hbm, buf, sem, 0, 0, tm)

    @pl.when(i + 1 < n)
    def _():
        _comb_issue(pos_ref, ys_hbm, buf, sem, i + 1, 1 - slot, tm)

    for k in range(2):
        pltpu.make_async_copy(ys_hbm.at[pl.ds(0, tm), :], buf.at[slot, k], sem.at[slot]).wait()
    f = buf[slot, 0] * route_ref[:, 2:3] + buf[slot, 1] * route_ref[:, 3:4]
    o_ref[...] = _layer_norm(ALPHA * x1_ref[...] + f, g_ref[...], b_ref[...])


def _combine(pos, ys, x1, route, g, b):
    n, d = x1.shape
    tm = COMB_TM
    grid_spec = pltpu.PrefetchScalarGridSpec(
        num_scalar_prefetch=1,
        grid=(n // tm,),
        in_specs=[
            pl.BlockSpec(memory_space=pl.ANY),
            pl.BlockSpec((tm, d), lambda i, p: (i, 0)),
            pl.BlockSpec((tm, ROUTE_W), lambda i, p: (i, 0)),
            pl.BlockSpec((1, d), lambda i, p: (0, 0)),
            pl.BlockSpec((1, d), lambda i, p: (0, 0)),
        ],
        out_specs=pl.BlockSpec((tm, d), lambda i, p: (i, 0)),
        scratch_shapes=[pltpu.VMEM((2, 2, tm, d), F32), pltpu.SemaphoreType.DMA((2,))],
    )
    return pl.pallas_call(
        functools.partial(_combine_body, tm=tm),
        grid_spec=grid_spec,
        out_shape=jax.ShapeDtypeStruct((n, d), F32),
        compiler_params=_cparams(("arbitrary",)),
    )(pos, ys, x1, route, g, b)


def _dispatch_plan(eid, n_items, n_rows):
    p_total = eid.shape[0]
    onehot = (eid[:, None] == jnp.arange(N_EXPERTS, dtype=jnp.int32)[None, :]).astype(jnp.int32)
    counts = jnp.sum(onehot, axis=0)
    rank = jnp.sum((jnp.cumsum(onehot, axis=0) - onehot) * onehot, axis=1)
    nblk = (counts + MOE_R - 1) // MOE_R
    seg_start = (jnp.cumsum(nblk) - nblk) * MOE_R
    pos = seg_start[eid] + rank
    src = jnp.zeros((n_rows,), jnp.int32).at[pos].set(jnp.arange(p_total, dtype=jnp.int32) // 2)
    items_per_e = (nblk + MOE_MAXB - 1) // MOE_MAXB
    item_end = jnp.cumsum(items_per_e)
    item_start = item_end - items_per_e
    t = jnp.arange(n_items, dtype=jnp.int32)
    e_t = jnp.minimum(jnp.searchsorted(item_end, t, side="right").astype(jnp.int32), N_EXPERTS - 1)
    live = t < item_end[-1]
    local = t - item_start[e_t]
    used = jnp.sum(nblk)
    idle0 = used + (t - item_end[-1]) * MOE_MAXB
    nz_t = jnp.clip(n_rows // MOE_R - idle0, 0, MOE_MAXB)
    nb_t = jnp.where(live, jnp.clip(nblk[e_t] - local * MOE_MAXB, 0, MOE_MAXB), -nz_t)
    row0_t = jnp.where(live, seg_start[e_t] + local * (MOE_MAXB * MOE_R),
                       jnp.minimum(idle0, n_rows // MOE_R - 1) * MOE_R)
    last_e = e_t[jnp.maximum(item_end[-1] - 1, 0)]
    e_t = jnp.where(live, e_t, last_e)
    return pos.astype(jnp.int32), src, e_t.astype(jnp.int32), row0_t.astype(jnp.int32), nb_t.astype(jnp.int32)


def _alibi_slopes():
    return jnp.asarray([2.0 ** (-8.0 * (h + 1) / N_AH) for h in range(N_AH)], dtype=F32)


def kernel(x_prompt, x_sample, state_conv, state_mlstm_C, state_mlstm_n, state_mlstm_m, cache_win_k, cache_win_v, w_in, b_gate, conv_w, conv_b, mh_gain, att_gain, w_out, ln1_g, ln1_b, w_group, b_group, w_router, b_router, w_gate, w_up, w_down, ln2_g, ln2_b):
    bp, tp, d = x_prompt.shape
    bs, ts, _ = x_sample.shape
    assert d == D_MODEL and w_in.shape[0] == 1 and tp % ATT_BLK == 0 and ts >= CONV_W - 1
    w_buf = cache_win_k.shape[2]
    n_p, n_s = bp * tp, bs * ts
    slopes = _alibi_slopes()

    wi = w_in[0]
    g0 = 4 * D_MLSTM
    g1 = g0 + 2 * N_MH
    w_pack = jnp.concatenate(
        [wi[:, :g0], wi[:, g1:], wi[:, g0:g1], jnp.zeros((d, LANE - 2 * N_MH), F32)], axis=1).astype(BF16)

    xp2 = x_prompt.reshape(n_p, d)
    xs2 = x_sample.reshape(n_s, d)
    proj_p = _in_proj(xp2, w_pack, 512)
    proj_s = _in_proj(xs2, w_pack, n_s)

    cw = conv_w[0]
    cb = conv_b[0][None, :]
    mh_g = mh_gain[0][None, :]
    att_g = att_gain[0][None, :]
    bg = b_gate[0]

    def gates_time_major(proj, batch, seq, pad_to):
        gt = proj[:, COL_G:COL_G + 2 * N_MH].reshape(batch, seq, 2, N_MH).transpose(0, 3, 2, 1)
        if pad_to > seq:
            gt = jnp.pad(gt, ((0, 0), (0, 0), (0, 0), (0, pad_to - seq)))
        return gt

    lc_p = 256
    hm_p, c_p, n_pp, m_p = _mlstm(
        proj_p, gates_time_major(proj_p, bp, tp, tp), jnp.zeros((bp, CONV_W - 1, 2 * D_MLSTM), F32), cw, cb,
        jnp.zeros((bp, N_MH, E_MH, E_MH), F32), jnp.zeros((bp, N_MH, 1, E_MH), F32),
        jnp.zeros((bp, N_MH, 1, 1), F32), mh_g, bg, batch=bp, seq=tp, lb=lc_p, lc=lc_p, out_dtype=BF16)
    ha_p = _attn_prompt(proj_p, att_g, slopes, batch=bp, seq=tp)

    lc_s = 128
    hm_s, c_s, n_ss, m_s = _mlstm(
        proj_s, gates_time_major(proj_s, bs, ts, lc_s), state_conv[0], cw, cb,
        state_mlstm_C[0], state_mlstm_n[0][:, :, None, :], state_mlstm_m[0][:, :, None, None],
        mh_g, bg, batch=bs, seq=ts, lb=ts, lc=lc_s, out_dtype=F32)
    ha_s, wk_s, wv_s = _attn_decode(
        proj_s, cache_win_k[0].reshape(bs, w_buf, N_AH * E_AH), cache_win_v[0].reshape(bs, w_buf, N_AH * E_AH),
        att_g, slopes, batch=bs, s_new=ts)

    n_all = n_p + n_s
    hm_all = jnp.concatenate([hm_p, hm_s.astype(BF16)], axis=0)
    ha_all = jnp.concatenate([ha_p, ha_s.astype(BF16)], axis=0)
    x_all = jnp.concatenate([xp2, xs2], axis=0)
    wo = w_out[0].astype(BF16)
    w_r = jnp.concatenate(
        [w_group[0], w_router[0].transpose(1, 0, 2).reshape(d, N_EXPERTS),
         jnp.zeros((d, ROUTE_W - N_GROUPS - N_EXPERTS), F32)], axis=1)
    b_r = jnp.concatenate(
        [b_group[0], b_router[0].reshape(N_EXPERTS), jnp.zeros((ROUTE_W - N_GROUPS - N_EXPERTS,), F32)])[None, :]
    x1, route = _outproj(hm_all, ha_all, x_all, wo[:D_MLSTM], wo[D_MLSTM:], ln1_g[0][None, :], ln1_b[0][None, :],
                         w_r, b_r)

    p_total = 2 * n_all
    n_rows = ((p_total + N_EXPERTS * (MOE_R - 1)) // MOE_R + 1) * MOE_R
    n_items = N_EXPERTS + n_rows // (MOE_R * MOE_MAXB)
    eid = route[:, 0:2].astype(jnp.int32).reshape(p_total)
    pos, src, item_e, item_row0, item_nb = _dispatch_plan(eid, n_items, n_rows)
    xs_sorted = _gather_rows(src, x1, n_rows)
    ys = _moe(item_e, item_row0, item_nb, xs_sorted, w_gate[0], w_up[0], w_down[0], n_items)
    y_all = _combine(pos, ys, x1, route, ln2_g[0][None, :], ln2_b[0][None, :])

    y_p = y_all[:n_p].reshape(bp, tp, d)
    y_s = y_all[n_p:].reshape(bs, ts, d)

    def tail_rows(proj, batch, seq, col, width, rows):
        return proj.reshape(batch, seq, N_PROJ)[:, seq - rows:, col:col + width]

    win = min(w_buf, tp)
    p_conv = tail_rows(proj_p, bp, tp, COL_QM, 2 * D_MLSTM, CONV_W - 1)[None]
    p_wk = tail_rows(proj_p, bp, tp, COL_KA, N_AH * E_AH, win).reshape(1, bp, win, N_AH, E_AH)
    p_wv = tail_rows(proj_p, bp, tp, COL_VA, N_AH * E_AH, win).reshape(1, bp, win, N_AH, E_AH)
    s_conv = tail_rows(proj_s, bs, ts, COL_QM, 2 * D_MLSTM, CONV_W - 1)[None]
    return (y_p, y_s,
            p_conv, c_p[None], n_pp[:, :, 0, :][None], m_p[:, :, 0, 0][None], p_wk, p_wv,
            s_conv, c_s[None], n_ss[:, :, 0, :][None], m_s[:, :, 0, 0][None],
            wk_s.reshape(1, bs, w_buf, N_AH, E_AH), wv_s.reshape(1, bs, w_buf, N_AH, E_AH))
```

```python
import functools
import math

import jax
import jax.numpy as jnp
from jax import lax
from jax.experimental import pallas as pl
from jax.experimental.pallas import tpu as pltpu

F32 = jnp.float32
BF16 = jnp.bfloat16
NEG_INF = float("-inf")

D_MODEL = 2048
D_MLSTM = 1024
N_MH = 4
E_MH = 256
N_AH = 8
E_AH = 128
DILATED_CONFIGS = ((128, 1), (512, 4), (2048, 16))
N_BACK = 128
BAND = 128
CONV_W = 4
N_GROUPS = 4
E_PER_GROUP = 8
N_EXPERTS = 32
D_FF = 1024
EPS = 1e-5
ALPHA = 2.0 ** 0.25

LANE = 128
SUBLANE = 8

COL_QM, COL_KM, COL_VM, COL_OM = 0, 1024, 2048, 3072
COL_QA, COL_KA, COL_VA, COL_G = 4096, 5120, 6144, 7168
N_PROJ = 7296
PROJ_TN = 2432

VMEM_LIMIT = 56 * 1024 * 1024


def _cparams(sem):
    return pltpu.CompilerParams(dimension_semantics=sem, vmem_limit_bytes=VMEM_LIMIT)


def _proj_body(x_ref, w_ref, o_ref):
    o_ref[...] = jnp.dot(x_ref[...].astype(BF16), w_ref[...], preferred_element_type=F32)


def _in_proj(x, w, tm):
    n = x.shape[0]
    return pl.pallas_call(
        _proj_body,
        grid=(N_PROJ // PROJ_TN, n // tm),
        in_specs=[pl.BlockSpec((tm, D_MODEL), lambda j, i: (i, 0)),
                  pl.BlockSpec((D_MODEL, PROJ_TN), lambda j, i: (0, j))],
        out_specs=pl.BlockSpec((tm, PROJ_TN), lambda j, i: (i, j)),
        out_shape=jax.ShapeDtypeStruct((n, N_PROJ), F32),
        compiler_params=_cparams(("arbitrary", "arbitrary")),
    )(x, w)


def _mlstm_body(bg_ref, xq_ref, xk_ref, v_ref, om_ref, g_ref, hq_ref, hk_ref, cwq_ref, cwk_ref,
                cbq_ref, cbk_ref, c0_ref, n0_ref, m0_ref, gain_ref,
                hm_ref, cout_ref, nout_ref, mout_ref,
                caug, m_s, uq, uk, vbuf, *, lb, lc, t_valid):
    h = pl.program_id(1)
    c = pl.program_id(2)
    nc = pl.num_programs(2)
    e = E_MH

    row_e = lax.broadcasted_iota(jnp.int32, (e, e), 0)
    col_e = lax.broadcasted_iota(jnp.int32, (e, e), 1)
    eye_e = row_e == col_e

    @pl.when(c == 0)
    def _init():
        caug[:, 0:e] = c0_ref[...]
        ncol = jnp.sum(jnp.where(eye_e, n0_ref[...], 0.0), axis=1, keepdims=True)
        caug[:, e:e + LANE] = jnp.broadcast_to(ncol, (e, LANE))
        m_s[...] = m0_ref[...]
        uq[...] = jnp.zeros(uq.shape, F32)
        uk[...] = jnp.zeros(uk.shape, F32)
        uq[pl.ds(SUBLANE - (CONV_W - 1), CONV_W - 1), :] = hq_ref[...]
        uk[pl.ds(SUBLANE - (CONV_W - 1), CONV_W - 1), :] = hk_ref[...]
        if lb != lc:
            vbuf[...] = jnp.zeros(vbuf.shape, F32)

    uq[pl.ds(SUBLANE, lb), :] = xq_ref[...]
    uk[pl.ds(SUBLANE, lb), :] = xk_ref[...]

    def conv_silu(u, cw_ref, cb_ref):
        acc = cb_ref[...]
        for j in range(CONV_W):
            acc = acc + u[pl.ds(SUBLANE - (CONV_W - 1) + j, lc), :] * cw_ref[j:j + 1, :]
        return acc * jax.nn.sigmoid(acc)

    q = conv_silu(uq, cwq_ref, cbq_ref)
    k = conv_silu(uk, cwk_ref, cbk_ref) * (e ** -0.5)
    if lb != lc:
        vbuf[pl.ds(0, lb), :] = v_ref[...]
        v = vbuf[...]
    else:
        v = v_ref[...]

    tq = uq[pl.ds(lc + SUBLANE - (CONV_W - 1), CONV_W - 1), :]
    tk = uk[pl.ds(lc + SUBLANE - (CONV_W - 1), CONV_W - 1), :]
    uq[pl.ds(SUBLANE - (CONV_W - 1), CONV_W - 1), :] = tq
    uk[pl.ds(SUBLANE - (CONV_W - 1), CONV_W - 1), :] = tk

    lane_t = lax.broadcasted_iota(jnp.int32, (1, lc), 1)
    valid = (lane_t + c * lc) < t_valid
    gi = g_ref[0:1, :] + bg_ref[h]
    gf = g_ref[1:2, :] + bg_ref[N_MH + h]
    lf = -(jnp.maximum(-gf, 0.0) + jnp.log1p(jnp.exp(-jnp.abs(gf))))
    ig_row = jnp.where(valid, gi, NEG_INF)
    lf_row = jnp.where(valid, lf, 0.0)

    row_l = lax.broadcasted_iota(jnp.int32, (lc, lc), 0)
    col_l = lax.broadcasted_iota(jnp.int32, (lc, lc), 1)
    causal = col_l <= row_l
    eye_l = col_l == row_l
    b_col = jnp.sum(jnp.where(causal, lf_row, 0.0), axis=1, keepdims=True)
    b_row = jnp.sum(jnp.where(eye_l, b_col, 0.0), axis=0, keepdims=True)
    m_prev = m_s[...]

    dlog = jnp.where(causal, b_col - b_row + ig_row, NEG_INF)
    a_col = b_col + m_prev
    m_t = jnp.maximum(a_col, jnp.max(dlog, axis=1, keepdims=True))
    w_intra = jnp.exp(dlog - m_t)
    w_inter = jnp.exp(a_col - m_t)

    qb = q.astype(BF16)
    kb = k.astype(BF16)
    vb = v.astype(BF16)
    s = lax.dot_general(qb, kb, (((1,), (1,)), ((), ())), preferred_element_type=F32) * w_intra
    qc = jnp.dot(qb, caug[...].astype(BF16), preferred_element_type=F32)
    num = jnp.dot(s.astype(BF16), vb, preferred_element_type=F32) + w_inter * qc[:, 0:e]
    den = jnp.sum(s, axis=1, keepdims=True) + w_inter * qc[:, e:e + 1]
    hh = num / jnp.maximum(jnp.abs(den), jnp.exp(-m_t))

    hh = hh - jnp.mean(hh, axis=1, keepdims=True)
    hh = hh * lax.rsqrt(jnp.mean(hh * hh, axis=1, keepdims=True) + EPS)
    if lb != lc:
        hh = hh[0:lb, :]
    hm = hh * gain_ref[...] * jax.nn.sigmoid(om_ref[...])
    hm_ref[...] = hm.astype(hm_ref.dtype)

    b_last = b_row[:, lc - 1:lc]
    logw_row = b_last - b_row + ig_row
    m_new = jnp.maximum(b_last + m_prev, jnp.max(logw_row, axis=1, keepdims=True))
    w_row = jnp.exp(logw_row - m_new)
    w_col = jnp.sum(jnp.where(eye_l, w_row, 0.0), axis=1, keepdims=True)
    decay = jnp.exp(b_last + m_prev - m_new)
    kw = (k * w_col).astype(BF16)
    vaug = jnp.concatenate([vb, jnp.ones((lc, LANE), BF16)], axis=1)
    upd = lax.dot_general(kw, vaug, (((0,), (0,)), ((), ())), preferred_element_type=F32)
    caug[...] = decay * caug[...] + upd
    m_s[...] = m_new

    @pl.when(c == nc - 1)
    def _fin():
        cout_ref[...] = caug[:, 0:e]
        nout_ref[...] = jnp.sum(jnp.where(eye_e, caug[:, e:e + 1], 0.0), axis=0, keepdims=True)
        mout_ref[...] = jnp.broadcast_to(m_s[...], (1, LANE))


def _mlstm(xp, gates_t, hist, conv_w, conv_b, c0, n0, m0, gain, b_gate, *, batch, seq, lb, lc, out_dtype):
    nc = seq // lb
    e = E_MH
    rb = lambda b, h, c, bg: b * nc + c
    qcol, kcol, vcol, ocol = COL_QM // e, COL_KM // e, COL_VM // e, COL_OM // e
    body = functools.partial(_mlstm_body, lb=lb, lc=lc, t_valid=seq if lb == lc else lb)
    grid_spec = pltpu.PrefetchScalarGridSpec(
        num_scalar_prefetch=1,
        grid=(batch, N_MH, nc),
        in_specs=[
            pl.BlockSpec((lb, e), lambda b, h, c, bg: (rb(b, h, c, bg), qcol + h)),
            pl.BlockSpec((lb, e), lambda b, h, c, bg: (rb(b, h, c, bg), kcol + h)),
            pl.BlockSpec((lb, e), lambda b, h, c, bg: (rb(b, h, c, bg), vcol + h)),
            pl.BlockSpec((lb, e), lambda b, h, c, bg: (rb(b, h, c, bg), ocol + h)),
            pl.BlockSpec((None, None, 2, lc), lambda b, h, c, bg: (b, h, 0, c)),
            pl.BlockSpec((None, CONV_W - 1, e), lambda b, h, c, bg: (b, 0, h)),
            pl.BlockSpec((None, CONV_W - 1, e), lambda b, h, c, bg: (b, 0, N_MH + h)),
            pl.BlockSpec((CONV_W, e), lambda b, h, c, bg: (0, h)),
            pl.BlockSpec((CONV_W, e), lambda b, h, c, bg: (0, N_MH + h)),
            pl.BlockSpec((1, e), lambda b, h, c, bg: (0, h)),
            pl.BlockSpec((1, e), lambda b, h, c, bg: (0, N_MH + h)),
            pl.BlockSpec((None, None, e, e), lambda b, h, c, bg: (b, h, 0, 0)),
            pl.BlockSpec((None, None, 1, e), lambda b, h, c, bg: (b, h, 0, 0)),
            pl.BlockSpec((None, None, 1, 1), lambda b, h, c, bg: (b, h, 0, 0)),
            pl.BlockSpec((1, e), lambda b, h, c, bg: (0, h)),
        ],
        out_specs=[
            pl.BlockSpec((lb, e), lambda b, h, c, bg: (rb(b, h, c, bg), h)),
            pl.BlockSpec((None, None, e, e), lambda b, h, c, bg: (b, h, 0, 0)),
            pl.BlockSpec((None, None, 1, e), lambda b, h, c, bg: (b, h, 0, 0)),
            pl.BlockSpec((None, None, 1, LANE), lambda b, h, c, bg: (b, h, 0, 0)),
        ],
        scratch_shapes=[
            pltpu.VMEM((e, e + LANE), F32),
            pltpu.VMEM((1, 1), F32),
            pltpu.VMEM((lc + 2 * SUBLANE, e), F32),
            pltpu.VMEM((lc + 2 * SUBLANE, e), F32),
            pltpu.VMEM((lc, e), F32),
        ],
    )
    return pl.pallas_call(
        body,
        grid_spec=grid_spec,
        out_shape=[
            jax.ShapeDtypeStruct((batch * seq, D_MLSTM), out_dtype),
            jax.ShapeDtypeStruct((batch, N_MH, e, e), F32),
            jax.ShapeDtypeStruct((batch, N_MH, 1, e), F32),
            jax.ShapeDtypeStruct((batch, N_MH, 1, LANE), F32),
        ],
        compiler_params=_cparams(("arbitrary", "arbitrary", "arbitrary")),
    )(b_gate, xp, xp, xp, xp, gates_t, hist, hist, conv_w, conv_w, conv_b, conv_b, c0, n0, m0, gain)


ATT_BLK = 2048
ATT_GROUP = 4


def _attn_body(sl_ref, q_ref, kc_ref, vc_ref, kp_ref, vp_ref, gain_ref, o_ref, o_s, m_s, l_s):
    blk = pl.program_id(1)
    h = pl.program_id(2)
    slope = sl_ref[h]
    scale = E_AH ** -0.5
    qi = lax.broadcasted_iota(jnp.int32, (BAND, BAND), 0)
    ci = lax.broadcasted_iota(jnp.int32, (BAND, BAND), 1)
    dist_prev = (BAND + qi - ci).astype(F32)
    dist_cur = (qi - ci).astype(F32)
    ok_prev = ci >= qi
    ok_cur = ci <= qi

    def unit(cfg, d, prev_pen, q_sl, kp_src, kc_src, vp_src, vc_src):
        sd = slope * float(d)
        bm_prev = jnp.where(ok_prev, prev_pen - sd * dist_prev, NEG_INF)
        bm_cur = jnp.where(ok_cur, -sd * dist_cur, NEG_INF)
        qv = q_ref[q_sl, :].astype(BF16)
        kp = kp_src[0][kp_src[1], :].astype(BF16)
        kc = kc_src[0][kc_src[1], :].astype(BF16)
        nt = (((1,), (1,)), ((), ()))
        s_p = lax.dot_general(qv, kp, nt, preferred_element_type=F32) * scale + bm_prev
        s_c = lax.dot_general(qv, kc, nt, preferred_element_type=F32) * scale + bm_cur
        m = jnp.maximum(jnp.max(s_p, axis=1, keepdims=True), jnp.max(s_c, axis=1, keepdims=True))
        p_p = jnp.exp(s_p - m)
        p_c = jnp.exp(s_c - m)
        l = jnp.sum(p_p, axis=1, keepdims=True) + jnp.sum(p_c, axis=1, keepdims=True)
        vp = vp_src[0][vp_src[1], :].astype(BF16)
        vc = vc_src[0][vc_src[1], :].astype(BF16)
        o = (jnp.dot(p_p.astype(BF16), vp, preferred_element_type=F32)
             + jnp.dot(p_c.astype(BF16), vc, preferred_element_type=F32))
        o_s[cfg, q_sl, :] = o
        m_s[cfg, q_sl, :] = jnp.broadcast_to(m, (BAND, E_AH))
        l_s[cfg, q_sl, :] = jnp.broadcast_to(l, (BAND, E_AH))

    first_pen = jnp.where(blk == 0, NEG_INF, 0.0)

    for cfg, (win, d) in enumerate(DILATED_CONFIGS):
        span = BAND * d
        n_u = ATT_BLK // span

        def sl(r, u, _d=d, _span=span):
            if _d == 1:
                if isinstance(u, int):
                    return pl.ds(u * _span, BAND)
                return pl.ds(pl.multiple_of(u * _span, BAND), BAND)
            return pl.ds(r + u * _span, BAND, stride=_d)

        def head_unit(r, cfg=cfg, d=d, span=span, sl=sl):
            cur = sl(r, 0)
            prv = sl(r + ATT_BLK - span, 0) if d > 1 else pl.ds(ATT_BLK - span, BAND)
            unit(cfg, d, first_pen, cur, (kp_ref, prv), (kc_ref, cur), (vp_ref, prv), (vc_ref, cur))

        def tail_unit(r, u, cfg=cfg, d=d, sl=sl):
            cur = sl(r, u)
            prv = sl(r, u - 1)
            unit(cfg, d, 0.0, cur, (kc_ref, prv), (kc_ref, cur), (vc_ref, prv), (vc_ref, cur))

        if d == 1:
            head_unit(0)
            for u in range(1, ATT_GROUP):
                tail_unit(0, u)

            def group1(g, _, tail_unit=tail_unit):
                for i in range(ATT_GROUP):
                    tail_unit(0, g * ATT_GROUP + i)
                return 0

            lax.fori_loop(1, n_u // ATT_GROUP, group1, 0)
        elif d == ATT_GROUP:
            for r in range(d):
                head_unit(r)

            def group4(u, _, tail_unit=tail_unit, d=d):
                for r in range(d):
                    tail_unit(r, u)
                return 0

            lax.fori_loop(1, n_u, group4, 0)
        else:
            assert n_u == 1 and d % ATT_GROUP == 0

            def group16(g, _, head_unit=head_unit):
                for i in range(ATT_GROUP):
                    head_unit(g * ATT_GROUP + i)
                return 0

            lax.fori_loop(0, d // ATT_GROUP, group16, 0)

    rows = 256

    def merge(i, _):
        rs = pl.ds(pl.multiple_of(i * rows, rows), rows)
        m0, m1, m2 = m_s[0, rs, :], m_s[1, rs, :], m_s[2, rs, :]
        m_all = jnp.maximum(jnp.maximum(m0, m1), m2)
        e0, e1, e2 = jnp.exp(m0 - m_all), jnp.exp(m1 - m_all), jnp.exp(m2 - m_all)
        num = e0 * o_s[0, rs, :] + e1 * o_s[1, rs, :] + e2 * o_s[2, rs, :]
        den = e0 * l_s[0, rs, :] + e1 * l_s[1, rs, :] + e2 * l_s[2, rs, :]
        ha = num / den
        ha = ha * lax.rsqrt(jnp.mean(ha * ha, axis=1, keepdims=True) + EPS)
        o_ref[rs, :] = (ha * gain_ref[...]).astype(o_ref.dtype)
        return 0

    lax.fori_loop(0, ATT_BLK // rows, merge, 0)


def _attn_prompt(xp, gain, slopes, *, batch, seq):
    nb = seq // ATT_BLK
    qc, kc, vc = COL_QA // E_AH, COL_KA // E_AH, COL_VA // E_AH
    cur = lambda b, i, h, s: b * nb + i
    prev = lambda b, i, h, s: b * nb + jnp.maximum(i - 1, 0)
    grid_spec = pltpu.PrefetchScalarGridSpec(
        num_scalar_prefetch=1,
        grid=(batch, nb, N_AH),
        in_specs=[
            pl.BlockSpec((ATT_BLK, E_AH), lambda b, i, h, s: (cur(b, i, h, s), qc + h)),
            pl.BlockSpec((ATT_BLK, E_AH), lambda b, i, h, s: (cur(b, i, h, s), kc + h)),
            pl.BlockSpec((ATT_BLK, E_AH), lambda b, i, h, s: (cur(b, i, h, s), vc + h)),
            pl.BlockSpec((ATT_BLK, E_AH), lambda b, i, h, s: (prev(b, i, h, s), kc + h)),
            pl.BlockSpec((ATT_BLK, E_AH), lambda b, i, h, s: (prev(b, i, h, s), vc + h)),
            pl.BlockSpec((1, E_AH), lambda b, i, h, s: (0, h)),
        ],
        out_specs=pl.BlockSpec((ATT_BLK, E_AH), lambda b, i, h, s: (cur(b, i, h, s), h)),
        scratch_shapes=[pltpu.VMEM((3, ATT_BLK, E_AH), F32)] * 3,
    )
    return pl.pallas_call(
        _attn_body,
        grid_spec=grid_spec,
        out_shape=jax.ShapeDtypeStruct((batch * seq, N_AH * E_AH), BF16),
        compiler_params=_cparams(("arbitrary", "arbitrary", "arbitrary")),
    )(slopes, xp, xp, xp, xp, xp, gain)


DEC_NEAR = 8


def _near_multiplicity():
    return [sum(1 for win, d in DILATED_CONFIGS if dist % d == 0 and dist <= win) for dist in range(DEC_NEAR)]


def _decode_body(q_ref, kn_ref, vn_ref, kc_ref, vc_ref, slope_ref, gain_ref,
                 ha_ref, ko_hbm, vo_hbm, ktail, vtail, sem, *, w_buf, s_new):
    b = pl.program_id(0)
    keep = w_buf - s_new
    copies = [
        pltpu.make_async_copy(kc_ref.at[0, 0, pl.ds(s_new, keep)], ko_hbm.at[0, b, pl.ds(0, keep)], sem.at[0]),
        pltpu.make_async_copy(vc_ref.at[0, 0, pl.ds(s_new, keep)], vo_hbm.at[0, b, pl.ds(0, keep)], sem.at[1]),
        pltpu.make_async_copy(kn_ref.at[0], ko_hbm.at[0, b, pl.ds(keep, s_new)], sem.at[2]),
        pltpu.make_async_copy(vn_ref.at[0], vo_hbm.at[0, b, pl.ds(keep, s_new)], sem.at[3]),
    ]
    for cp in copies:
        cp.start()

    ktail[pl.ds(0, DEC_NEAR)] = kc_ref[0, 0, pl.ds(w_buf - DEC_NEAR, DEC_NEAR)]
    vtail[pl.ds(0, DEC_NEAR)] = vc_ref[0, 0, pl.ds(w_buf - DEC_NEAR, DEC_NEAR)]
    ktail[pl.ds(DEC_NEAR, s_new)] = kn_ref[0]
    vtail[pl.ds(DEC_NEAR, s_new)] = vn_ref[0]

    scale = E_AH ** -0.5
    slope = slope_ref[:, 0:1]
    near_i = lax.broadcasted_iota(jnp.int32, (DEC_NEAR, 1, 1), 0)
    near_dist = (DEC_NEAR - 1 - near_i).astype(F32)
    near_mult = jnp.zeros((DEC_NEAR, 1, 1), F32)
    for dist, c in enumerate(_near_multiplicity()):
        near_mult = jnp.where(near_i == DEC_NEAR - 1 - dist, float(c), near_mult)

    def far_part(s, win, d):
        n = N_BACK - (DEC_NEAR - 1) // d
        sl = pl.ds(w_buf + s - win, n, stride=d) if d > 1 else pl.ds(w_buf + s - win, n)
        i = lax.broadcasted_iota(jnp.int32, (n, 1, 1), 0)
        return sl, ((N_BACK - i) * d).astype(F32)

    def one_query(s, _):
        q = q_ref[0, s]
        parts = []
        for win, d in DILATED_CONFIGS:
            sl, dist = far_part(s, win, d)
            sc = jnp.sum(kc_ref[0, 0, sl] * q, axis=-1, keepdims=True) * scale - slope * dist
            parts.append((sc, None, lambda sl=sl: vc_ref[0, 0, sl]))
        nsl = pl.ds(s + 1, DEC_NEAR)
        sc = jnp.sum(ktail[nsl] * q, axis=-1, keepdims=True) * scale - slope * near_dist
        parts.append((sc, near_mult, lambda: vtail[nsl]))
        m = functools.reduce(jnp.maximum, [jnp.max(p[0], axis=0, keepdims=True) for p in parts])
        den = jnp.zeros((1, N_AH, 1), F32)
        o = jnp.zeros((1, N_AH, E_AH), F32)
        for sc, mu, load_v in parts:
            p = jnp.exp(sc - m)
            if mu is not None:
                p = p * mu
            den = den + jnp.sum(p, axis=0, keepdims=True)
            o = o + jnp.sum(p * load_v(), axis=0, keepdims=True)
        o = (o / den)[0]
        o = o * lax.rsqrt(jnp.mean(o * o, axis=-1, keepdims=True) + EPS)
        ha_ref[0, s] = o * gain_ref[...]
        return 0

    lax.fori_loop(0, s_new, one_query, 0)
    for cp in copies:
        cp.wait()


def _attn_decode(q3, kn3, vn3, cache_k, cache_v, gain, slopes, *, batch, s_new):
    w_buf = cache_k.shape[2]
    assert w_buf >= max(w for w, _ in DILATED_CONFIGS) and s_new <= DEC_NEAR
    assert all(w // d == N_BACK and d & (d - 1) == 0 for w, d in DILATED_CONFIGS)
    body = functools.partial(_decode_body, w_buf=w_buf, s_new=s_new)
    new_spec = pl.BlockSpec((1, s_new, N_AH, E_AH), lambda b: (b, 0, 0, 0))
    cache_spec = pl.BlockSpec((1, 1, w_buf, N_AH, E_AH), lambda b: (0, b, 0, 0, 0))
    tile_spec = pl.BlockSpec((N_AH, E_AH), lambda b: (0, 0))
    return pl.pallas_call(
        body,
        grid=(batch,),
        in_specs=[new_spec, new_spec, new_spec, cache_spec, cache_spec, tile_spec, tile_spec],
        out_specs=[new_spec, pl.BlockSpec(memory_space=pl.ANY), pl.BlockSpec(memory_space=pl.ANY)],
        out_shape=[
            jax.ShapeDtypeStruct((batch, s_new, N_AH, E_AH), F32),
            jax.ShapeDtypeStruct(cache_k.shape, F32),
            jax.ShapeDtypeStruct(cache_v.shape, F32),
        ],
        scratch_shapes=[pltpu.VMEM((DEC_NEAR + s_new, N_AH, E_AH), F32)] * 2 + [pltpu.SemaphoreType.DMA((4,))],
        compiler_params=_cparams(("arbitrary",)),
    )(q3, kn3, vn3, cache_k, cache_v, slopes, gain)


OP_TM = 768
ROUTE_W = LANE


def _layer_norm(z, g, b):
    mu = jnp.mean(z, axis=1, keepdims=True)
    zc = z - mu
    var = jnp.mean(zc * zc, axis=1, keepdims=True)
    return zc * lax.rsqrt(var + EPS) * g + b


def _outproj_body(hm_ref, ha_ref, x_ref, wm_ref, wa_ref, g_ref, b_ref, wrh_ref, wrl_ref, br_ref, x1_ref, route_ref):
    y = (jnp.dot(hm_ref[...].astype(BF16), wm_ref[...], preferred_element_type=F32)
         + jnp.dot(ha_ref[...].astype(BF16), wa_ref[...], preferred_element_type=F32))
    x1 = _layer_norm(ALPHA * x_ref[...] + y, g_ref[...], b_ref[...])
    x1_ref[...] = x1
    x1_hi = x1.astype(BF16)
    x1_lo = (x1 - x1_hi.astype(F32)).astype(BF16)
    logits = (jnp.dot(x1_hi, wrh_ref[...], preferred_element_type=F32)
              + jnp.dot(x1_lo, wrh_ref[...], preferred_element_type=F32)
              + jnp.dot(x1_hi, wrl_ref[...], preferred_element_type=F32)) + br_ref[...]
    tm = logits.shape[0]
    lane = lax.broadcasted_iota(jnp.int32, (tm, ROUTE_W), 1)
    lane_f = lane.astype(F32)
    big = float(ROUTE_W)
    gl = jnp.where(lane < N_GROUPS, logits, NEG_INF)
    gmax = jnp.max(gl, axis=1, keepdims=True)
    g_w = 1.0 / jnp.sum(jnp.exp(gl - gmax), axis=1, keepdims=True)
    g_idx = jnp.min(jnp.where(gl == gmax, lane_f, big), axis=1, keepdims=True)
    lo = N_GROUPS + E_PER_GROUP * g_idx
    el = jnp.where(jnp.logical_and(lane_f >= lo, lane_f < lo + E_PER_GROUP), logits, NEG_INF)
    v1 = jnp.max(el, axis=1, keepdims=True)
    i1 = jnp.min(jnp.where(el == v1, lane_f, big), axis=1, keepdims=True)
    el2 = jnp.where(lane_f == i1, NEG_INF, el)
    v2 = jnp.max(el2, axis=1, keepdims=True)
    i2 = jnp.min(jnp.where(el2 == v2, lane_f, big), axis=1, keepdims=True)
    e2 = jnp.exp(v2 - v1)
    w1 = g_w / (1.0 + e2)
    w2 = g_w * e2 / (1.0 + e2)
    route = jnp.where(lane == 0, i1 - N_GROUPS,
                      jnp.where(lane == 1, i2 - N_GROUPS,
                                jnp.where(lane == 2, w1, jnp.where(lane == 3, w2, 0.0))))
    route_ref[...] = route


def _outproj(hm, ha, x, wm, wa, g, b, wr, br):
    n = x.shape[0]
    tm = OP_TM if n % OP_TM == 0 else 256
    row = lambda i: (i, 0)
    fixed = lambda i: (0, 0)
    once = pl.Buffered(1)
    wr_hi = wr.astype(BF16)
    wr_lo = (wr - wr_hi.astype(F32)).astype(BF16)
    return pl.pallas_call(
        _outproj_body,
        grid=(n // tm,),
        in_specs=[
            pl.BlockSpec((tm, D_MLSTM), row),
            pl.BlockSpec((tm, N_AH * E_AH), row),
            pl.BlockSpec((tm, D_MODEL), row),
            pl.BlockSpec((D_MLSTM, D_MODEL), fixed, pipeline_mode=once),
            pl.BlockSpec((N_AH * E_AH, D_MODEL), fixed, pipeline_mode=once),
            pl.BlockSpec((1, D_MODEL), fixed),
            pl.BlockSpec((1, D_MODEL), fixed),
            pl.BlockSpec((D_MODEL, ROUTE_W), fixed),
            pl.BlockSpec((D_MODEL, ROUTE_W), fixed),
            pl.BlockSpec((1, ROUTE_W), fixed),
        ],
        out_specs=[pl.BlockSpec((tm, D_MODEL), row), pl.BlockSpec((tm, ROUTE_W), row)],
        out_shape=[jax.ShapeDtypeStruct((n, D_MODEL), F32), jax.ShapeDtypeStruct((n, ROUTE_W), F32)],
        compiler_params=_cparams(("arbitrary",)),
    )(hm, ha, x, wm, wa, g, b, wr_hi, wr_lo, br)


GATHER_TM = 128


def _gather_issue(src_ref, x_hbm, buf, sem, tile, slot, tm):
    def body(r, _):
        tok = src_ref[tile * tm + r]
        pltpu.make_async_copy(x_hbm.at[pl.ds(tok, 1), :], buf.at[slot, pl.ds(r, 1), :], sem.at[slot]).start()
        return 0

    lax.fori_loop(0, tm, body, 0)


def _gather_body(src_ref, x_hbm, o_ref, buf, sem, *, tm):
    i = pl.program_id(0)
    n = pl.num_programs(0)
    slot = i % 2

    @pl.when(i == 0)
    def _():
        _gather_issue(src_ref, x_hbm, buf, sem, 0, 0, tm)

    @pl.when(i + 1 < n)
    def _():
        _gather_issue(src_ref, x_hbm, buf, sem, i + 1, 1 - slot, tm)

    pltpu.make_async_copy(x_hbm.at[pl.ds(0, tm), :], buf.at[slot], sem.at[slot]).wait()
    o_ref[...] = buf[slot].astype(o_ref.dtype)


def _gather_rows(src, x1, n_rows):
    tm = GATHER_TM
    d = x1.shape[1]
    grid_spec = pltpu.PrefetchScalarGridSpec(
        num_scalar_prefetch=1,
        grid=(n_rows // tm,),
        in_specs=[pl.BlockSpec(memory_space=pl.ANY)],
        out_specs=pl.BlockSpec((tm, d), lambda i, s: (i, 0)),
        scratch_shapes=[pltpu.VMEM((2, tm, d), F32), pltpu.SemaphoreType.DMA((2,))],
    )
    return pl.pallas_call(
        functools.partial(_gather_body, tm=tm),
        grid_spec=grid_spec,
        out_shape=jax.ShapeDtypeStruct((n_rows, d), BF16),
        compiler_params=_cparams(("arbitrary",)),
    )(src, x1)


MOE_R = 128
MOE_MAXB = 8
MOE_FC = 256
MOE_NC = D_FF // MOE_FC


def _moe_body(ex_ref, row0_ref, nb_ref, xs_hbm, wg_ref, wu_ref, wd_ref, ys_hbm, xbuf, acc, sem_in, sem_out):
    t = pl.program_id(0)
    c = pl.program_id(1)
    n_items = pl.num_programs(0)
    slot = t % 2
    nb = nb_ref[t]

    def rows(base, r):
        return pl.ds(pl.multiple_of(base + r * MOE_R, MOE_R), MOE_R)

    def in_copy(tt, sl, r):
        return pltpu.make_async_copy(xs_hbm.at[rows(row0_ref[tt], r), :], xbuf.at[sl, rows(0, r), :], sem_in.at[sl])

    def out_copy(tt, sl, r):
        return pltpu.make_async_copy(acc.at[sl, rows(0, r), :], ys_hbm.at[rows(row0_ref[tt], r), :], sem_out.at[sl])

    def for_blocks(n, fn):
        def body(r, _):
            fn(r)
            return 0

        lax.fori_loop(0, n, body, 0)

    def n_in(tt):
        return jnp.maximum(nb_ref[tt], 0)

    def n_out(tt):
        return jnp.abs(nb_ref[tt])

    @pl.when(c == 0)
    def _begin():
        @pl.when(t == 0)
        def _():
            for_blocks(n_in(0), lambda r: in_copy(0, 0, r).start())

        @pl.when(t + 1 < n_items)
        def _():
            for_blocks(n_in(t + 1), lambda r: in_copy(t + 1, 1 - slot, r).start())

        for_blocks(n_in(t), lambda r: in_copy(t, slot, r).wait())

        @pl.when(t >= 2)
        def _():
            for_blocks(n_out(t - 2), lambda r: out_copy(t - 2, slot, r).wait())

        @pl.when(nb < 0)
        def _():
            acc[slot] = jnp.zeros(acc.shape[1:], F32)

    wg = wg_ref[...].astype(BF16)
    wu = wu_ref[...].astype(BF16)
    wd = wd_ref[...].astype(BF16)

    def piece(off, size):
        rs = pl.ds(pl.multiple_of(off, MOE_R), size)
        x = xbuf[slot, rs, :]
        gt = jnp.dot(x, wg, preferred_element_type=F32)
        up = jnp.dot(x, wu, preferred_element_type=F32)
        hid = (gt * jax.nn.sigmoid(gt) * up).astype(BF16)
        y = jnp.dot(hid, wd, preferred_element_type=F32)

        @pl.when(c == 0)
        def _():
            acc[slot, rs, :] = y

        @pl.when(c > 0)
        def _():
            acc[slot, rs, :] = acc[slot, rs, :] + y

    nbp = jnp.maximum(nb, 0)
    n4 = nbp // 4
    rem = nbp - 4 * n4
    for_blocks(n4, lambda i: piece(i * (4 * MOE_R), 4 * MOE_R))

    @pl.when(rem >= 2)
    def _():
        piece(n4 * (4 * MOE_R), 2 * MOE_R)

    @pl.when(rem % 2 == 1)
    def _():
        piece(n4 * (4 * MOE_R) + (rem // 2) * (2 * MOE_R), MOE_R)

    @pl.when(c == MOE_NC - 1)
    def _end():
        for_blocks(n_out(t), lambda r: out_copy(t, slot, r).start())

        @pl.when(t == n_items - 1)
        def _():
            for_blocks(n_out(t), lambda r: out_copy(t, slot, r).wait())

            @pl.when(t >= 1)
            def _():
                for_blocks(n_out(t - 1), lambda r: out_copy(t - 1, 1 - slot, r).wait())


def _moe(item_e, item_row0, item_nb, xs, w_gate, w_up, w_down, n_items):
    n_rows = xs.shape[0]

    def chunk(t, c, n):
        return jnp.where(n[t] > 0, c, MOE_NC - 1)

    grid_spec = pltpu.PrefetchScalarGridSpec(
        num_scalar_prefetch=3,
        grid=(n_items, MOE_NC),
        in_specs=[
            pl.BlockSpec(memory_space=pl.ANY),
            pl.BlockSpec((None, D_MODEL, MOE_FC), lambda t, c, e, r, n: (e[t], 0, chunk(t, c, n))),
            pl.BlockSpec((None, D_MODEL, MOE_FC), lambda t, c, e, r, n: (e[t], 0, chunk(t, c, n))),
            pl.BlockSpec((None, MOE_FC, D_MODEL), lambda t, c, e, r, n: (e[t], chunk(t, c, n), 0)),
        ],
        out_specs=pl.BlockSpec(memory_space=pl.ANY),
        scratch_shapes=[
            pltpu.VMEM((2, MOE_MAXB * MOE_R, D_MODEL), BF16),
            pltpu.VMEM((2, MOE_MAXB * MOE_R, D_MODEL), F32),
            pltpu.SemaphoreType.DMA((2,)),
            pltpu.SemaphoreType.DMA((2,)),
        ],
    )
    return pl.pallas_call(
        _moe_body,
        grid_spec=grid_spec,
        out_shape=jax.ShapeDtypeStruct((n_rows, D_MODEL), F32),
        compiler_params=_cparams(("arbitrary", "arbitrary")),
    )(item_e, item_row0, item_nb, xs, w_gate, w_up, w_down)


COMB_TM = 128


def _comb_issue(pos_ref, ys_hbm, buf, sem, tile, slot, tm):
    def body(r, _):
        for k in range(2):
            p = pos_ref[2 * (tile * tm + r) + k]
            pltpu.make_async_copy(ys_hbm.at[pl.ds(p, 1), :], buf.at[slot, k, pl.ds(r, 1), :], sem.at[slot]).start()
        return 0

    lax.fori_loop(0, tm, body, 0)


def _combine_body(pos_ref, ys_hbm, x1_ref, route_ref, g_ref, b_ref, o_ref, buf, sem, *, tm):
    i = pl.program_id(0)
    n = pl.num_programs(0)
    slot = i % 2

    @pl.when(i == 0)
    def _():
        _comb_issue(pos_ref, ys_hbm, buf, sem, 0, 0, tm)

    @pl.when(i + 1 < n)
    def _():
        _comb_issue(pos_ref, ys_hbm, buf, sem, i + 1, 1 - slot, tm)

    for k in range(2):
        pltpu.make_async_copy(ys_hbm.at[pl.ds(0, tm), :], buf.at[slot, k], sem.at[slot]).wait()
    f = buf[slot, 0] * route_ref[:, 2:3] + buf[slot, 1] * route_ref[:, 3:4]
    o_ref[...] = _layer_norm(ALPHA * x1_ref[...] + f, g_ref[...], b_ref[...])


def _combine(pos, ys, x1, route, g, b):
    n, d = x1.shape
    tm = COMB_TM
    grid_spec = pltpu.PrefetchScalarGridSpec(
        num_scalar_prefetch=1,
        grid=(n // tm,),
        in_specs=[
            pl.BlockSpec(memory_space=pl.ANY),
            pl.BlockSpec((tm, d), lambda i, p: (i, 0)),
            pl.BlockSpec((tm, ROUTE_W), lambda i, p: (i, 0)),
            pl.BlockSpec((1, d), lambda i, p: (0, 0)),
            pl.BlockSpec((1, d), lambda i, p: (0, 0)),
        ],
        out_specs=pl.BlockSpec((tm, d), lambda i, p: (i, 0)),
        scratch_shapes=[pltpu.VMEM((2, 2, tm, d), F32), pltpu.SemaphoreType.DMA((2,))],
    )
    return pl.pallas_call(
        functools.partial(_combine_body, tm=tm),
        grid_spec=grid_spec,
        out_shape=jax.ShapeDtypeStruct((n, d), F32),
        compiler_params=_cparams(("arbitrary",)),
    )(pos, ys, x1, route, g, b)


def _dispatch_plan(eid, n_items, n_rows):
    p_total = eid.shape[0]
    onehot = (eid[:, None] == jnp.arange(N_EXPERTS, dtype=jnp.int32)[None, :]).astype(jnp.int32)
    counts = jnp.sum(onehot, axis=0)
    rank = jnp.sum((jnp.cumsum(onehot, axis=0) - onehot) * onehot, axis=1)
    nblk = (counts + MOE_R - 1) // MOE_R
    seg_start = (jnp.cumsum(nblk) - nblk) * MOE_R
    pos = seg_start[eid] + rank
    src = jnp.zeros((n_rows,), jnp.int32).at[pos].set(jnp.arange(p_total, dtype=jnp.int32) // 2)
    items_per_e = (nblk + MOE_MAXB - 1) // MOE_MAXB
    item_end = jnp.cumsum(items_per_e)
    item_start = item_end - items_per_e
    t = jnp.arange(n_items, dtype=jnp.int32)
    e_t = jnp.minimum(jnp.sum((item_end[None, :] <= t[:, None]).astype(jnp.int32), axis=1), N_EXPERTS - 1)
    live = t < item_end[-1]
    local = t - item_start[e_t]
    used = jnp.sum(nblk)
    idle0 = used + (t - item_end[-1]) * MOE_MAXB
    nz_t = jnp.clip(n_rows // MOE_R - idle0, 0, MOE_MAXB)
    nb_t = jnp.where(live, jnp.clip(nblk[e_t] - local * MOE_MAXB, 0, MOE_MAXB), -nz_t)
    row0_t = jnp.where(live, seg_start[e_t] + local * (MOE_MAXB * MOE_R),
                       jnp.minimum(idle0, n_rows // MOE_R - 1) * MOE_R)
    last_e = e_t[jnp.maximum(item_end[-1] - 1, 0)]
    e_t = jnp.where(live, e_t, last_e)
    return pos.astype(jnp.int32), src, e_t.astype(jnp.int32), row0_t.astype(jnp.int32), nb_t.astype(jnp.int32)


def _alibi_slopes():
    return jnp.asarray([2.0 ** (-8.0 * (h + 1) / N_AH) for h in range(N_AH)], dtype=F32)


def kernel(x_prompt, x_sample, state_conv, state_mlstm_C, state_mlstm_n, state_mlstm_m, cache_win_k, cache_win_v, w_in, b_gate, conv_w, conv_b, mh_gain, att_gain, w_out, ln1_g, ln1_b, w_group, b_group, w_router, b_router, w_gate, w_up, w_down, ln2_g, ln2_b):
    bp, tp, d = x_prompt.shape
    bs, ts, _ = x_sample.shape
    assert d == D_MODEL and w_in.shape[0] == 1 and tp % ATT_BLK == 0 and ts >= CONV_W - 1
    w_buf = cache_win_k.shape[2]
    n_p, n_s = bp * tp, bs * ts
    slopes = _alibi_slopes()

    wi = w_in[0]
    g0 = 4 * D_MLSTM
    g1 = g0 + 2 * N_MH
    w_pack = jnp.concatenate(
        [wi[:, :g0], wi[:, g1:], wi[:, g0:g1], jnp.zeros((d, LANE - 2 * N_MH), F32)], axis=1).astype(BF16)

    xp2 = x_prompt.reshape(n_p, d)
    xs2 = x_sample.reshape(n_s, d)
    proj_p = _in_proj(xp2, w_pack, 512)
    proj_s = _in_proj(xs2, w_pack, n_s)

    cw = conv_w[0]
    cb = conv_b[0][None, :]
    mh_g = mh_gain[0][None, :]
    att_g = att_gain[0][None, :]
    bg = b_gate[0]

    def gates_time_major(proj, batch, seq, pad_to):
        gt = proj[:, COL_G:COL_G + 2 * N_MH].reshape(batch, seq, 2, N_MH).transpose(0, 3, 2, 1)
        if pad_to > seq:
            gt = jnp.pad(gt, ((0, 0), (0, 0), (0, 0), (0, pad_to - seq)))
        return gt

    lc_p = 256
    hm_p, c_p, n_pp, m_p = _mlstm(
        proj_p, gates_time_major(proj_p, bp, tp, tp), jnp.zeros((bp, CONV_W - 1, 2 * D_MLSTM), F32), cw, cb,
        jnp.zeros((bp, N_MH, E_MH, E_MH), F32), jnp.zeros((bp, N_MH, 1, E_MH), F32),
        jnp.zeros((bp, N_MH, 1, 1), F32), mh_g, bg, batch=bp, seq=tp, lb=lc_p, lc=lc_p, out_dtype=BF16)
    ha_p = _attn_prompt(proj_p, att_g, slopes, batch=bp, seq=tp)

    lc_s = 128
    hm_s, c_s, n_ss, m_s = _mlstm(
        proj_s, gates_time_major(proj_s, bs, ts, lc_s), state_conv[0], cw, cb,
        state_mlstm_C[0], state_mlstm_n[0][:, :, None, :], state_mlstm_m[0][:, :, None, None],
        mh_g, bg, batch=bs, seq=ts, lb=ts, lc=lc_s, out_dtype=F32)
    new_rows = lambda col: proj_s[:, col:col + N_AH * E_AH].reshape(bs, ts, N_AH, E_AH)
    ha_s, wk_s, wv_s = _attn_decode(
        new_rows(COL_QA), new_rows(COL_KA), new_rows(COL_VA), cache_win_k, cache_win_v,
        att_gain[0].reshape(N_AH, E_AH), jnp.broadcast_to(slopes[:, None], (N_AH, E_AH)), batch=bs, s_new=ts)
    ha_s = ha_s.reshape(n_s, N_AH * E_AH)

    n_all = n_p + n_s
    hm_all = jnp.concatenate([hm_p, hm_s.astype(BF16)], axis=0)
    ha_all = jnp.concatenate([ha_p, ha_s.astype(BF16)], axis=0)
    x_all = jnp.concatenate([xp2, xs2], axis=0)
    wo = w_out[0].astype(BF16)
    w_r = jnp.concatenate(
        [w_group[0], w_router[0].transpose(1, 0, 2).reshape(d, N_EXPERTS),
         jnp.zeros((d, ROUTE_W - N_GROUPS - N_EXPERTS), F32)], axis=1)
    b_r = jnp.concatenate(
        [b_group[0], b_router[0].reshape(N_EXPERTS), jnp.zeros((ROUTE_W - N_GROUPS - N_EXPERTS,), F32)])[None, :]
    x1, route = _outproj(hm_all, ha_all, x_all, wo[:D_MLSTM], wo[D_MLSTM:], ln1_g[0][None, :], ln1_b[0][None, :],
                         w_r, b_r)

    p_total = 2 * n_all
    n_rows = ((p_total + N_EXPERTS * (MOE_R - 1)) // MOE_R + 1) * MOE_R
    n_items = N_EXPERTS + n_rows // (MOE_R * MOE_MAXB)
    eid = route[:, 0:2].astype(jnp.int32).reshape(p_total)
    pos, src, item_e, item_row0, item_nb = _dispatch_plan(eid, n_items, n_rows)
    xs_sorted = _gather_rows(src, x1, n_rows)
    ys = _moe(item_e, item_row0, item_nb, xs_sorted, w_gate[0], w_up[0], w_down[0], n_items)
    y_all = _combine(pos, ys, x1, route, ln2_g[0][None, :], ln2_b[0][None, :])

    y_p = y_all[:n_p].reshape(bp, tp, d)
    y_s = y_all[n_p:].reshape(bs, ts, d)

    def tail_rows(proj, batch, seq, col, width, rows):
        return proj.reshape(batch, seq, N_PROJ)[:, seq - rows:, col:col + width]

    win = min(w_buf, tp)
    p_conv = tail_rows(proj_p, bp, tp, COL_QM, 2 * D_MLSTM, CONV_W - 1)[None]
    p_wk = tail_rows(proj_p, bp, tp, COL_KA, N_AH * E_AH, win).reshape(1, bp, win, N_AH, E_AH)
    p_wv = tail_rows(proj_p, bp, tp, COL_VA, N_AH * E_AH, win).reshape(1, bp, win, N_AH, E_AH)
    s_conv = tail_rows(proj_s, bs, ts, COL_QM, 2 * D_MLSTM, CONV_W - 1)[None]
    return (y_p, y_s,
            p_conv, c_p[None], n_pp[:, :, 0, :][None], m_p[:, :, 0, 0][None], p_wk, p_wv,
            s_conv, c_s[None], n_ss[:, :, 0, :][None], m_s[:, :, 0, 0][None], wk_s, wv_s)
```

```python
import functools
import math

import jax
import jax.numpy as jnp
from jax import lax
from jax.experimental import pallas as pl
from jax.experimental.pallas import tpu as pltpu

F32 = jnp.float32
BF16 = jnp.bfloat16
NEG_INF = float("-inf")

D_MODEL = 2048
D_MLSTM = 1024
N_MH = 4
E_MH = 256
N_AH = 8
E_AH = 128
DILATED_CONFIGS = ((128, 1), (512, 4), (2048, 16))
N_BACK = 128
BAND = 128
CONV_W = 4
N_GROUPS = 4
E_PER_GROUP = 8
N_EXPERTS = 32
D_FF = 1024
EPS = 1e-5
ALPHA = 2.0 ** 0.25

LANE = 128
SUBLANE = 8

COL_QM, COL_KM, COL_VM, COL_OM = 0, 1024, 2048, 3072
COL_QA, COL_KA, COL_VA, COL_G = 4096, 5120, 6144, 7168
N_PROJ = 7296
PROJ_TN = 2432

VMEM_LIMIT = 56 * 1024 * 1024


def _cparams(sem):
    return pltpu.CompilerParams(dimension_semantics=sem, vmem_limit_bytes=VMEM_LIMIT)


def _proj_body(x_ref, w_ref, o_ref):
    o_ref[...] = jnp.dot(x_ref[...].astype(BF16), w_ref[...], preferred_element_type=F32)


def _in_proj(x, w, tm):
    n = x.shape[0]
    return pl.pallas_call(
        _proj_body,
        grid=(N_PROJ // PROJ_TN, n // tm),
        in_specs=[pl.BlockSpec((tm, D_MODEL), lambda j, i: (i, 0)),
                  pl.BlockSpec((D_MODEL, PROJ_TN), lambda j, i: (0, j))],
        out_specs=pl.BlockSpec((tm, PROJ_TN), lambda j, i: (i, j)),
        out_shape=jax.ShapeDtypeStruct((n, N_PROJ), F32),
        compiler_params=_cparams(("arbitrary", "arbitrary")),
    )(x, w)


def _mlstm_body(bg_ref, xq_ref, xk_ref, v_ref, om_ref, g_ref, hq_ref, hk_ref, cwq_ref, cwk_ref,
                cbq_ref, cbk_ref, c0_ref, n0_ref, m0_ref, gain_ref,
                hm_ref, cout_ref, nout_ref, mout_ref,
                caug, m_s, uq, uk, vbuf, *, lb, lc, t_valid):
    h = pl.program_id(1)
    c = pl.program_id(2)
    nc = pl.num_programs(2)
    e = E_MH

    row_e = lax.broadcasted_iota(jnp.int32, (e, e), 0)
    col_e = lax.broadcasted_iota(jnp.int32, (e, e), 1)
    eye_e = row_e == col_e

    @pl.when(c == 0)
    def _init():
        caug[:, 0:e] = c0_ref[...]
        ncol = jnp.sum(jnp.where(eye_e, n0_ref[...], 0.0), axis=1, keepdims=True)
        caug[:, e:e + LANE] = jnp.broadcast_to(ncol, (e, LANE))
        m_s[...] = m0_ref[...]
        uq[...] = jnp.zeros(uq.shape, F32)
        uk[...] = jnp.zeros(uk.shape, F32)
        uq[pl.ds(SUBLANE - (CONV_W - 1), CONV_W - 1), :] = hq_ref[...]
        uk[pl.ds(SUBLANE - (CONV_W - 1), CONV_W - 1), :] = hk_ref[...]
        if lb != lc:
            vbuf[...] = jnp.zeros(vbuf.shape, F32)

    uq[pl.ds(SUBLANE, lb), :] = xq_ref[...]
    uk[pl.ds(SUBLANE, lb), :] = xk_ref[...]

    def conv_silu(u, cw_ref, cb_ref):
        acc = cb_ref[...]
        for j in range(CONV_W):
            acc = acc + u[pl.ds(SUBLANE - (CONV_W - 1) + j, lc), :] * cw_ref[j:j + 1, :]
        return acc * jax.nn.sigmoid(acc)

    q = conv_silu(uq, cwq_ref, cbq_ref)
    k = conv_silu(uk, cwk_ref, cbk_ref) * (e ** -0.5)
    if lb != lc:
        vbuf[pl.ds(0, lb), :] = v_ref[...]
        v = vbuf[...]
    else:
        v = v_ref[...]

    tq = uq[pl.ds(lc + SUBLANE - (CONV_W - 1), CONV_W - 1), :]
    tk = uk[pl.ds(lc + SUBLANE - (CONV_W - 1), CONV_W - 1), :]
    uq[pl.ds(SUBLANE - (CONV_W - 1), CONV_W - 1), :] = tq
    uk[pl.ds(SUBLANE - (CONV_W - 1), CONV_W - 1), :] = tk

    lane_t = lax.broadcasted_iota(jnp.int32, (1, lc), 1)
    valid = (lane_t + c * lc) < t_valid
    gi = g_ref[0:1, :] + bg_ref[h]
    gf = g_ref[1:2, :] + bg_ref[N_MH + h]
    lf = -(jnp.maximum(-gf, 0.0) + jnp.log1p(jnp.exp(-jnp.abs(gf))))
    ig_row = jnp.where(valid, gi, NEG_INF)
    lf_row = jnp.where(valid, lf, 0.0)

    row_l = lax.broadcasted_iota(jnp.int32, (lc, lc), 0)
    col_l = lax.broadcasted_iota(jnp.int32, (lc, lc), 1)
    causal = col_l <= row_l
    eye_l = col_l == row_l
    b_col = jnp.sum(jnp.where(causal, lf_row, 0.0), axis=1, keepdims=True)
    b_row = jnp.sum(jnp.where(eye_l, b_col, 0.0), axis=0, keepdims=True)
    m_prev = m_s[...]

    dlog = jnp.where(causal, b_col - b_row + ig_row, NEG_INF)
    a_col = b_col + m_prev
    m_t = jnp.maximum(a_col, jnp.max(dlog, axis=1, keepdims=True))
    w_intra = jnp.exp(dlog - m_t)
    w_inter = jnp.exp(a_col - m_t)

    qb = q.astype(BF16)
    kb = k.astype(BF16)
    vb = v.astype(BF16)
    s = lax.dot_general(qb, kb, (((1,), (1,)), ((), ())), preferred_element_type=F32) * w_intra
    qc = jnp.dot(qb, caug[...].astype(BF16), preferred_element_type=F32)
    num = jnp.dot(s.astype(BF16), vb, preferred_element_type=F32) + w_inter * qc[:, 0:e]
    den = jnp.sum(s, axis=1, keepdims=True) + w_inter * qc[:, e:e + 1]
    hh = num / jnp.maximum(jnp.abs(den), jnp.exp(-m_t))

    hh = hh - jnp.mean(hh, axis=1, keepdims=True)
    hh = hh * lax.rsqrt(jnp.mean(hh * hh, axis=1, keepdims=True) + EPS)
    if lb != lc:
        hh = hh[0:lb, :]
    hm = hh * gain_ref[...] * jax.nn.sigmoid(om_ref[...])
    hm_ref[...] = hm.astype(hm_ref.dtype)

    b_last = b_row[:, lc - 1:lc]
    logw_row = b_last - b_row + ig_row
    m_new = jnp.maximum(b_last + m_prev, jnp.max(logw_row, axis=1, keepdims=True))
    w_row = jnp.exp(logw_row - m_new)
    w_col = jnp.sum(jnp.where(eye_l, w_row, 0.0), axis=1, keepdims=True)
    decay = jnp.exp(b_last + m_prev - m_new)
    kw = (k * w_col).astype(BF16)
    vaug = jnp.concatenate([vb, jnp.ones((lc, LANE), BF16)], axis=1)
    upd = lax.dot_general(kw, vaug, (((0,), (0,)), ((), ())), preferred_element_type=F32)
    caug[...] = decay * caug[...] + upd
    m_s[...] = m_new

    @pl.when(c == nc - 1)
    def _fin():
        cout_ref[...] = caug[:, 0:e]
        nout_ref[...] = jnp.sum(jnp.where(eye_e, caug[:, e:e + 1], 0.0), axis=0, keepdims=True)
        mout_ref[...] = jnp.broadcast_to(m_s[...], (1, LANE))


def _mlstm(xp, gates_t, hist, conv_w, conv_b, c0, n0, m0, gain, b_gate, *, batch, seq, lb, lc, out_dtype):
    nc = seq // lb
    e = E_MH
    rb = lambda b, h, c, bg: b * nc + c
    qcol, kcol, vcol, ocol = COL_QM // e, COL_KM // e, COL_VM // e, COL_OM // e
    body = functools.partial(_mlstm_body, lb=lb, lc=lc, t_valid=seq if lb == lc else lb)
    grid_spec = pltpu.PrefetchScalarGridSpec(
        num_scalar_prefetch=1,
        grid=(batch, N_MH, nc),
        in_specs=[
            pl.BlockSpec((lb, e), lambda b, h, c, bg: (rb(b, h, c, bg), qcol + h)),
            pl.BlockSpec((lb, e), lambda b, h, c, bg: (rb(b, h, c, bg), kcol + h)),
            pl.BlockSpec((lb, e), lambda b, h, c, bg: (rb(b, h, c, bg), vcol + h)),
            pl.BlockSpec((lb, e), lambda b, h, c, bg: (rb(b, h, c, bg), ocol + h)),
            pl.BlockSpec((None, None, 2, lc), lambda b, h, c, bg: (b, h, 0, c)),
            pl.BlockSpec((None, CONV_W - 1, e), lambda b, h, c, bg: (b, 0, h)),
            pl.BlockSpec((None, CONV_W - 1, e), lambda b, h, c, bg: (b, 0, N_MH + h)),
            pl.BlockSpec((CONV_W, e), lambda b, h, c, bg: (0, h)),
            pl.BlockSpec((CONV_W, e), lambda b, h, c, bg: (0, N_MH + h)),
            pl.BlockSpec((1, e), lambda b, h, c, bg: (0, h)),
            pl.BlockSpec((1, e), lambda b, h, c, bg: (0, N_MH + h)),
            pl.BlockSpec((None, None, e, e), lambda b, h, c, bg: (b, h, 0, 0)),
            pl.BlockSpec((None, None, 1, e), lambda b, h, c, bg: (b, h, 0, 0)),
            pl.BlockSpec((None, None, 1, 1), lambda b, h, c, bg: (b, h, 0, 0)),
            pl.BlockSpec((1, e), lambda b, h, c, bg: (0, h)),
        ],
        out_specs=[
            pl.BlockSpec((lb, e), lambda b, h, c, bg: (rb(b, h, c, bg), h)),
            pl.BlockSpec((None, None, e, e), lambda b, h, c, bg: (b, h, 0, 0)),
            pl.BlockSpec((None, None, 1, e), lambda b, h, c, bg: (b, h, 0, 0)),
            pl.BlockSpec((None, None, 1, LANE), lambda b, h, c, bg: (b, h, 0, 0)),
        ],
        scratch_shapes=[
            pltpu.VMEM((e, e + LANE), F32),
            pltpu.VMEM((1, 1), F32),
            pltpu.VMEM((lc + 2 * SUBLANE, e), F32),
            pltpu.VMEM((lc + 2 * SUBLANE, e), F32),
            pltpu.VMEM((lc, e), F32),
        ],
    )
    return pl.pallas_call(
        body,
        grid_spec=grid_spec,
        out_shape=[
            jax.ShapeDtypeStruct((batch * seq, D_MLSTM), out_dtype),
            jax.ShapeDtypeStruct((batch, N_MH, e, e), F32),
            jax.ShapeDtypeStruct((batch, N_MH, 1, e), F32),
            jax.ShapeDtypeStruct((batch, N_MH, 1, LANE), F32),
        ],
        compiler_params=_cparams(("arbitrary", "arbitrary", "arbitrary")),
    )(b_gate, xp, xp, xp, xp, gates_t, hist, hist, conv_w, conv_w, conv_b, conv_b, c0, n0, m0, gain)


ATT_BLK = 2048
ATT_GROUP = 4


def _attn_body(sl_ref, q_ref, kc_ref, vc_ref, kp_ref, vp_ref, gain_ref, o_ref, o_s, m_s, l_s):
    blk = pl.program_id(1)
    h = pl.program_id(2)
    slope = sl_ref[h]
    scale = E_AH ** -0.5
    qi = lax.broadcasted_iota(jnp.int32, (BAND, BAND), 0)
    ci = lax.broadcasted_iota(jnp.int32, (BAND, BAND), 1)
    dist_prev = (BAND + qi - ci).astype(F32)
    dist_cur = (qi - ci).astype(F32)
    ok_prev = ci >= qi
    ok_cur = ci <= qi

    def unit(cfg, d, prev_pen, q_sl, kp_src, kc_src, vp_src, vc_src):
        sd = slope * float(d)
        bm_prev = jnp.where(ok_prev, prev_pen - sd * dist_prev, NEG_INF)
        bm_cur = jnp.where(ok_cur, -sd * dist_cur, NEG_INF)
        qv = q_ref[q_sl, :].astype(BF16)
        kp = kp_src[0][kp_src[1], :].astype(BF16)
        kc = kc_src[0][kc_src[1], :].astype(BF16)
        nt = (((1,), (1,)), ((), ()))
        s_p = lax.dot_general(qv, kp, nt, preferred_element_type=F32) * scale + bm_prev
        s_c = lax.dot_general(qv, kc, nt, preferred_element_type=F32) * scale + bm_cur
        m = jnp.max(jnp.maximum(s_p, s_c), axis=1, keepdims=True)
        p_p = jnp.exp(s_p - m)
        p_c = jnp.exp(s_c - m)
        l = jnp.sum(p_p + p_c, axis=1, keepdims=True)
        vp = vp_src[0][vp_src[1], :].astype(BF16)
        vc = vc_src[0][vc_src[1], :].astype(BF16)
        o = (jnp.dot(p_p.astype(BF16), vp, preferred_element_type=F32)
             + jnp.dot(p_c.astype(BF16), vc, preferred_element_type=F32))
        o_s[cfg, q_sl, :] = o
        m_s[cfg, q_sl, :] = jnp.broadcast_to(m, (BAND, E_AH))
        l_s[cfg, q_sl, :] = jnp.broadcast_to(l, (BAND, E_AH))

    first_pen = jnp.where(blk == 0, NEG_INF, 0.0)

    for cfg, (win, d) in enumerate(DILATED_CONFIGS):
        span = BAND * d
        n_u = ATT_BLK // span

        def sl(r, u, _d=d, _span=span):
            if _d == 1:
                if isinstance(u, int):
                    return pl.ds(u * _span, BAND)
                return pl.ds(pl.multiple_of(u * _span, BAND), BAND)
            return pl.ds(r + u * _span, BAND, stride=_d)

        def head_unit(r, cfg=cfg, d=d, span=span, sl=sl):
            cur = sl(r, 0)
            prv = sl(r + ATT_BLK - span, 0) if d > 1 else pl.ds(ATT_BLK - span, BAND)
            unit(cfg, d, first_pen, cur, (kp_ref, prv), (kc_ref, cur), (vp_ref, prv), (vc_ref, cur))

        def tail_unit(r, u, cfg=cfg, d=d, sl=sl):
            cur = sl(r, u)
            prv = sl(r, u - 1)
            unit(cfg, d, 0.0, cur, (kc_ref, prv), (kc_ref, cur), (vc_ref, prv), (vc_ref, cur))

        if d == 1:
            head_unit(0)
            for u in range(1, ATT_GROUP):
                tail_unit(0, u)

            def group1(g, _, tail_unit=tail_unit):
                for i in range(ATT_GROUP):
                    tail_unit(0, g * ATT_GROUP + i)
                return 0

            lax.fori_loop(1, n_u // ATT_GROUP, group1, 0)
        elif d == ATT_GROUP:
            for r in range(d):
                head_unit(r)

            def group4(u, _, tail_unit=tail_unit, d=d):
                for r in range(d):
                    tail_unit(r, u)
                return 0

            lax.fori_loop(1, n_u, group4, 0)
        else:
            assert n_u == 1 and d % ATT_GROUP == 0

            def group16(g, _, head_unit=head_unit):
                for i in range(ATT_GROUP):
                    head_unit(g * ATT_GROUP + i)
                return 0

            lax.fori_loop(0, d // ATT_GROUP, group16, 0)

    rows = 256

    def merge(i, _):
        rs = pl.ds(pl.multiple_of(i * rows, rows), rows)
        m0, m1, m2 = m_s[0, rs, :], m_s[1, rs, :], m_s[2, rs, :]
        m_all = jnp.maximum(jnp.maximum(m0, m1), m2)
        e0, e1, e2 = jnp.exp(m0 - m_all), jnp.exp(m1 - m_all), jnp.exp(m2 - m_all)
        num = e0 * o_s[0, rs, :] + e1 * o_s[1, rs, :] + e2 * o_s[2, rs, :]
        den = e0 * l_s[0, rs, :] + e1 * l_s[1, rs, :] + e2 * l_s[2, rs, :]
        ha = num / den
        ha = ha * lax.rsqrt(jnp.mean(ha * ha, axis=1, keepdims=True) + EPS)
        o_ref[rs, :] = (ha * gain_ref[...]).astype(o_ref.dtype)
        return 0

    lax.fori_loop(0, ATT_BLK // rows, merge, 0)


def _attn_prompt(xp, gain, slopes, *, batch, seq):
    nb = seq // ATT_BLK
    qc, kc, vc = COL_QA // E_AH, COL_KA // E_AH, COL_VA // E_AH
    cur = lambda b, i, h, s: b * nb + i
    prev = lambda b, i, h, s: b * nb + jnp.maximum(i - 1, 0)
    grid_spec = pltpu.PrefetchScalarGridSpec(
        num_scalar_prefetch=1,
        grid=(batch, nb, N_AH),
        in_specs=[
            pl.BlockSpec((ATT_BLK, E_AH), lambda b, i, h, s: (cur(b, i, h, s), qc + h)),
            pl.BlockSpec((ATT_BLK, E_AH), lambda b, i, h, s: (cur(b, i, h, s), kc + h)),
            pl.BlockSpec((ATT_BLK, E_AH), lambda b, i, h, s: (cur(b, i, h, s), vc + h)),
            pl.BlockSpec((ATT_BLK, E_AH), lambda b, i, h, s: (prev(b, i, h, s), kc + h)),
            pl.BlockSpec((ATT_BLK, E_AH), lambda b, i, h, s: (prev(b, i, h, s), vc + h)),
            pl.BlockSpec((1, E_AH), lambda b, i, h, s: (0, h)),
        ],
        out_specs=pl.BlockSpec((ATT_BLK, E_AH), lambda b, i, h, s: (cur(b, i, h, s), h)),
        scratch_shapes=[pltpu.VMEM((3, ATT_BLK, E_AH), F32)] * 3,
    )
    return pl.pallas_call(
        _attn_body,
        grid_spec=grid_spec,
        out_shape=jax.ShapeDtypeStruct((batch * seq, N_AH * E_AH), BF16),
        compiler_params=_cparams(("arbitrary", "arbitrary", "arbitrary")),
    )(slopes, xp, xp, xp, xp, xp, gain)


DEC_NEAR = 8


def _near_multiplicity():
    return [sum(1 for win, d in DILATED_CONFIGS if dist % d == 0 and dist <= win) for dist in range(DEC_NEAR)]


def _decode_body(q_ref, kn_ref, vn_ref, kc_ref, vc_ref, slope_ref, gain_ref,
                 ha_ref, ko_hbm, vo_hbm, ktail, vtail, sem, *, w_buf, s_new):
    b = pl.program_id(0)
    keep = w_buf - s_new
    copies = [
        pltpu.make_async_copy(kc_ref.at[0, 0, pl.ds(s_new, keep)], ko_hbm.at[0, b, pl.ds(0, keep)], sem.at[0]),
        pltpu.make_async_copy(vc_ref.at[0, 0, pl.ds(s_new, keep)], vo_hbm.at[0, b, pl.ds(0, keep)], sem.at[1]),
        pltpu.make_async_copy(kn_ref.at[0], ko_hbm.at[0, b, pl.ds(keep, s_new)], sem.at[2]),
        pltpu.make_async_copy(vn_ref.at[0], vo_hbm.at[0, b, pl.ds(keep, s_new)], sem.at[3]),
    ]
    for cp in copies:
        cp.start()

    ktail[pl.ds(0, DEC_NEAR)] = kc_ref[0, 0, pl.ds(w_buf - DEC_NEAR, DEC_NEAR)]
    vtail[pl.ds(0, DEC_NEAR)] = vc_ref[0, 0, pl.ds(w_buf - DEC_NEAR, DEC_NEAR)]
    ktail[pl.ds(DEC_NEAR, s_new)] = kn_ref[0]
    vtail[pl.ds(DEC_NEAR, s_new)] = vn_ref[0]

    scale = E_AH ** -0.5
    slope = slope_ref[:, 0:1]
    near_i = lax.broadcasted_iota(jnp.int32, (DEC_NEAR, 1, 1), 0)
    near_dist = (DEC_NEAR - 1 - near_i).astype(F32)
    near_mult = jnp.zeros((DEC_NEAR, 1, 1), F32)
    for dist, c in enumerate(_near_multiplicity()):
        near_mult = jnp.where(near_i == DEC_NEAR - 1 - dist, float(c), near_mult)

    def far_part(s, win, d):
        n = N_BACK - (DEC_NEAR - 1) // d
        sl = pl.ds(w_buf + s - win, n, stride=d) if d > 1 else pl.ds(w_buf + s - win, n)
        i = lax.broadcasted_iota(jnp.int32, (n, 1, 1), 0)
        return sl, ((N_BACK - i) * d).astype(F32)

    def one_query(s, _):
        q = q_ref[0, s]
        parts = []
        for win, d in DILATED_CONFIGS:
            sl, dist = far_part(s, win, d)
            sc = jnp.sum(kc_ref[0, 0, sl] * q, axis=-1, keepdims=True) * scale - slope * dist
            parts.append((sc, None, lambda sl=sl: vc_ref[0, 0, sl]))
        nsl = pl.ds(s + 1, DEC_NEAR)
        sc = jnp.sum(ktail[nsl] * q, axis=-1, keepdims=True) * scale - slope * near_dist
        parts.append((sc, near_mult, lambda: vtail[nsl]))
        m = functools.reduce(jnp.maximum, [jnp.max(p[0], axis=0, keepdims=True) for p in parts])
        den = jnp.zeros((1, N_AH, 1), F32)
        o = jnp.zeros((1, N_AH, E_AH), F32)
        for sc, mu, load_v in parts:
            p = jnp.exp(sc - m)
            if mu is not None:
                p = p * mu
            den = den + jnp.sum(p, axis=0, keepdims=True)
            o = o + jnp.sum(p * load_v(), axis=0, keepdims=True)
        o = (o / den)[0]
        o = o * lax.rsqrt(jnp.mean(o * o, axis=-1, keepdims=True) + EPS)
        ha_ref[0, s] = o * gain_ref[...]
        return 0

    lax.fori_loop(0, s_new, one_query, 0)
    for cp in copies:
        cp.wait()


def _attn_decode(q3, kn3, vn3, cache_k, cache_v, gain, slopes, *, batch, s_new):
    w_buf = cache_k.shape[2]
    assert w_buf >= max(w for w, _ in DILATED_CONFIGS) and s_new <= DEC_NEAR
    assert all(w // d == N_BACK and d & (d - 1) == 0 for w, d in DILATED_CONFIGS)
    body = functools.partial(_decode_body, w_buf=w_buf, s_new=s_new)
    new_spec = pl.BlockSpec((1, s_new, N_AH, E_AH), lambda b: (b, 0, 0, 0))
    cache_spec = pl.BlockSpec((1, 1, w_buf, N_AH, E_AH), lambda b: (0, b, 0, 0, 0))
    tile_spec = pl.BlockSpec((N_AH, E_AH), lambda b: (0, 0))
    return pl.pallas_call(
        body,
        grid=(batch,),
        in_specs=[new_spec, new_spec, new_spec, cache_spec, cache_spec, tile_spec, tile_spec],
        out_specs=[new_spec, pl.BlockSpec(memory_space=pl.ANY), pl.BlockSpec(memory_space=pl.ANY)],
        out_shape=[
            jax.ShapeDtypeStruct((batch, s_new, N_AH, E_AH), F32),
            jax.ShapeDtypeStruct(cache_k.shape, F32),
            jax.ShapeDtypeStruct(cache_v.shape, F32),
        ],
        scratch_shapes=[pltpu.VMEM((DEC_NEAR + s_new, N_AH, E_AH), F32)] * 2 + [pltpu.SemaphoreType.DMA((4,))],
        compiler_params=_cparams(("arbitrary",)),
    )(q3, kn3, vn3, cache_k, cache_v, slopes, gain)


OP_TM = 512
ROUTE_W = LANE
PACK_ROWS = D_MODEL // LANE


def _layer_norm(z, g, b):
    mu = jnp.mean(z, axis=1, keepdims=True)
    zc = z - mu
    var = jnp.mean(zc * zc, axis=1, keepdims=True)
    return zc * lax.rsqrt(var + EPS) * g + b


def _outproj_body(hm_ref, ha_ref, x_ref, wm_ref, wa_ref, g_ref, b_ref, wrh_ref, wrl_ref, br_ref,
                  x1_ref, x1p_ref, route_ref):
    y = (jnp.dot(hm_ref[...].astype(BF16), wm_ref[...], preferred_element_type=F32)
         + jnp.dot(ha_ref[...].astype(BF16), wa_ref[...], preferred_element_type=F32))
    x1 = _layer_norm(ALPHA * x_ref[...] + y, g_ref[...], b_ref[...])
    x1_ref[...] = x1
    tm = x1.shape[0]
    for j in range(PACK_ROWS):
        x1p_ref[pl.ds(j, tm, stride=PACK_ROWS), :] = x1[:, j * LANE:(j + 1) * LANE]
    x1_hi = x1.astype(BF16)
    x1_lo = (x1 - x1_hi.astype(F32)).astype(BF16)
    logits = (jnp.dot(x1_hi, wrh_ref[...], preferred_element_type=F32)
              + jnp.dot(x1_lo, wrh_ref[...], preferred_element_type=F32)
              + jnp.dot(x1_hi, wrl_ref[...], preferred_element_type=F32)) + br_ref[...]
    lane = lax.broadcasted_iota(jnp.int32, (tm, ROUTE_W), 1)
    lane_f = lane.astype(F32)
    big = float(ROUTE_W)
    gl = jnp.where(lane < N_GROUPS, logits, NEG_INF)
    gmax = jnp.max(gl, axis=1, keepdims=True)
    g_w = 1.0 / jnp.sum(jnp.exp(gl - gmax), axis=1, keepdims=True)
    g_idx = jnp.min(jnp.where(gl == gmax, lane_f, big), axis=1, keepdims=True)
    lo = N_GROUPS + E_PER_GROUP * g_idx
    el = jnp.where(jnp.logical_and(lane_f >= lo, lane_f < lo + E_PER_GROUP), logits, NEG_INF)
    v1 = jnp.max(el, axis=1, keepdims=True)
    i1 = jnp.min(jnp.where(el == v1, lane_f, big), axis=1, keepdims=True)
    el2 = jnp.where(lane_f == i1, NEG_INF, el)
    v2 = jnp.max(el2, axis=1, keepdims=True)
    i2 = jnp.min(jnp.where(el2 == v2, lane_f, big), axis=1, keepdims=True)
    e2 = jnp.exp(v2 - v1)
    w1 = g_w / (1.0 + e2)
    w2 = g_w * e2 / (1.0 + e2)
    route = jnp.where(lane == 0, i1 - N_GROUPS,
                      jnp.where(lane == 1, i2 - N_GROUPS,
                                jnp.where(lane == 2, w1, jnp.where(lane == 3, w2, 0.0))))
    route_ref[...] = route


def _outproj(hm, ha, x, wm, wa, g, b, wr_hi, wr_lo, br):
    n = x.shape[0]
    tm = min(OP_TM, n)
    assert n % tm == 0
    row = lambda i: (i, 0)
    fixed = lambda i: (0, 0)
    once = pl.Buffered(1)
    return pl.pallas_call(
        _outproj_body,
        grid=(n // tm,),
        in_specs=[
            pl.BlockSpec((tm, D_MLSTM), row),
            pl.BlockSpec((tm, N_AH * E_AH), row),
            pl.BlockSpec((tm, D_MODEL), row),
            pl.BlockSpec((D_MLSTM, D_MODEL), fixed, pipeline_mode=once),
            pl.BlockSpec((N_AH * E_AH, D_MODEL), fixed, pipeline_mode=once),
            pl.BlockSpec((1, D_MODEL), fixed),
            pl.BlockSpec((1, D_MODEL), fixed),
            pl.BlockSpec((D_MODEL, ROUTE_W), fixed),
            pl.BlockSpec((D_MODEL, ROUTE_W), fixed),
            pl.BlockSpec((1, ROUTE_W), fixed),
        ],
        out_specs=[pl.BlockSpec((tm, D_MODEL), row), pl.BlockSpec((tm * PACK_ROWS, LANE), row),
                   pl.BlockSpec((tm, ROUTE_W), row)],
        out_shape=[jax.ShapeDtypeStruct((n, D_MODEL), F32), jax.ShapeDtypeStruct((n * PACK_ROWS, LANE), F32),
                   jax.ShapeDtypeStruct((n, ROUTE_W), F32)],
        compiler_params=_cparams(("arbitrary",)),
    )(hm, ha, x, wm, wa, g, b, wr_hi, wr_lo, br)


MOE_R = 128
MOE_MAXB = 6
MOE_FC = 256
MOE_NC = D_FF // MOE_FC
MOE_ISSUE_UNROLL = 8


def _moe_body(ex_ref, row0_ref, nb_ref, src_ref, x1p_hbm, wg_ref, wu_ref, wd_ref, ys_hbm,
              ubuf, xbuf, acc, sem_in, sem_out):
    t = pl.program_id(0)
    c = pl.program_id(1)
    n_items = pl.num_programs(0)
    slot = t % 2
    nb = nb_ref[t]

    def rows(base, r):
        return pl.ds(pl.multiple_of(base + r * MOE_R, MOE_R), MOE_R)

    def tile_rows(i):
        return pl.ds(pl.multiple_of(i * PACK_ROWS, PACK_ROWS), PACK_ROWS)

    def out_copy(tt, sl, r):
        return pltpu.make_async_copy(acc.at[sl, rows(0, r), :], ys_hbm.at[rows(row0_ref[tt], r), :], sem_out.at[sl])

    def for_blocks(n, fn):
        def body(r, _):
            fn(r)
            return 0

        lax.fori_loop(0, n, body, 0)

    def n_in(tt):
        return jnp.maximum(nb_ref[tt], 0)

    def n_out(tt):
        return jnp.abs(nb_ref[tt])

    def start_gather(tt, sl):
        base = row0_ref[tt]

        def group(gi):
            for k in range(MOE_ISSUE_UNROLL):
                i = gi * MOE_ISSUE_UNROLL + k
                tok = src_ref[base + i]
                pltpu.make_async_copy(x1p_hbm.at[tile_rows(tok), :], ubuf.at[sl, tile_rows(i), :], sem_in.at[sl]).start()

        for_blocks(n_in(tt) * (MOE_R // MOE_ISSUE_UNROLL), group)

    def wait_gather(tt, sl):
        def block(r):
            span = pl.ds(pl.multiple_of(r * (MOE_R * PACK_ROWS), MOE_R * PACK_ROWS), MOE_R * PACK_ROWS)
            pltpu.make_async_copy(x1p_hbm.at[pl.ds(0, MOE_R * PACK_ROWS), :], ubuf.at[sl, span, :], sem_in.at[sl]).wait()

        for_blocks(n_in(tt), block)

    def unpack_block(r):
        for j in range(PACK_ROWS):
            start = r * (MOE_R * PACK_ROWS) + j
            u = ubuf[slot, pl.ds(start, MOE_R, stride=PACK_ROWS), :]
            xbuf[rows(0, r), j * LANE:(j + 1) * LANE] = u.astype(BF16)

    @pl.when(c == 0)
    def _begin():
        @pl.when(t == 0)
        def _():
            start_gather(0, 0)

        @pl.when(t + 1 < n_items)
        def _():
            start_gather(t + 1, 1 - slot)

        wait_gather(t, slot)
        for_blocks(n_in(t), unpack_block)

        @pl.when(t >= 2)
        def _():
            for_blocks(n_out(t - 2), lambda r: out_copy(t - 2, slot, r).wait())

        @pl.when(nb < 0)
        def _():
            acc[slot] = jnp.zeros(acc.shape[1:], F32)

    wg = wg_ref[...].astype(BF16)
    wu = wu_ref[...].astype(BF16)
    wd = wd_ref[...].astype(BF16)

    def piece(off, size):
        rs = pl.ds(pl.multiple_of(off, MOE_R), size)
        x = xbuf[rs, :]
        gt = jnp.dot(x, wg, preferred_element_type=F32)
        up = jnp.dot(x, wu, preferred_element_type=F32)
        hid = (gt * jax.nn.sigmoid(gt) * up).astype(BF16)
        y = jnp.dot(hid, wd, preferred_element_type=F32)

        @pl.when(c == 0)
        def _():
            acc[slot, rs, :] = y

        @pl.when(c > 0)
        def _():
            acc[slot, rs, :] = acc[slot, rs, :] + y

    nbp = jnp.maximum(nb, 0)
    n4 = nbp // 4
    rem = nbp - 4 * n4
    for_blocks(n4, lambda i: piece(i * (4 * MOE_R), 4 * MOE_R))

    @pl.when(rem >= 2)
    def _():
        piece(n4 * (4 * MOE_R), 2 * MOE_R)

    @pl.when(rem % 2 == 1)
    def _():
        piece(n4 * (4 * MOE_R) + (rem // 2) * (2 * MOE_R), MOE_R)

    @pl.when(c == MOE_NC - 1)
    def _end():
        for_blocks(n_out(t), lambda r: out_copy(t, slot, r).start())

        @pl.when(t == n_items - 1)
        def _():
            for_blocks(n_out(t), lambda r: out_copy(t, slot, r).wait())

            @pl.when(t >= 1)
            def _():
                for_blocks(n_out(t - 1), lambda r: out_copy(t - 1, 1 - slot, r).wait())


def _moe(item_e, item_row0, item_nb, src, x1p, w_gate, w_up, w_down, n_items, n_rows):
    def chunk(t, c, n):
        return jnp.where(n[t] > 0, c, MOE_NC - 1)

    grid_spec = pltpu.PrefetchScalarGridSpec(
        num_scalar_prefetch=4,
        grid=(n_items, MOE_NC),
        in_specs=[
            pl.BlockSpec(memory_space=pl.ANY),
            pl.BlockSpec((None, D_MODEL, MOE_FC), lambda t, c, e, r, n, s: (e[t], 0, chunk(t, c, n))),
            pl.BlockSpec((None, D_MODEL, MOE_FC), lambda t, c, e, r, n, s: (e[t], 0, chunk(t, c, n))),
            pl.BlockSpec((None, MOE_FC, D_MODEL), lambda t, c, e, r, n, s: (e[t], chunk(t, c, n), 0)),
        ],
        out_specs=pl.BlockSpec(memory_space=pl.ANY),
        scratch_shapes=[
            pltpu.VMEM((2, MOE_MAXB * MOE_R * PACK_ROWS, LANE), F32),
            pltpu.VMEM((MOE_MAXB * MOE_R, D_MODEL), BF16),
            pltpu.VMEM((2, MOE_MAXB * MOE_R, D_MODEL), F32),
            pltpu.SemaphoreType.DMA((2,)),
            pltpu.SemaphoreType.DMA((2,)),
        ],
    )
    return pl.pallas_call(
        _moe_body,
        grid_spec=grid_spec,
        out_shape=jax.ShapeDtypeStruct((n_rows, D_MODEL), F32),
        compiler_params=_cparams(("arbitrary", "arbitrary")),
    )(item_e, item_row0, item_nb, src, x1p, w_gate, w_up, w_down)


COMB_TM = 128


def _comb_issue(pos_ref, ys_hbm, buf, sem, tile, slot, tm):
    def body(r, _):
        for k in range(2):
            p = pos_ref[2 * (tile * tm + r) + k]
            pltpu.make_async_copy(ys_hbm.at[pl.ds(p, 1), :], buf.at[slot, k, pl.ds(r, 1), :], sem.at[slot]).start()
        return 0

    lax.fori_loop(0, tm, body, 0)


def _combine_body(pos_ref, ys_hbm, x1_ref, route_ref, g_ref, b_ref, o_ref, buf, sem, *, tm):
    i = pl.program_id(0)
    n = pl.num_programs(0)
    slot = i % 2

    @pl.when(i == 0)
    def _():
        _comb_issue(pos_ref, ys_hbm, buf, sem, 0, 0, tm)

    @pl.when(i + 1 < n)
    def _():
        _comb_issue(pos_ref, ys_hbm, buf, sem, i + 1, 1 - slot, tm)

    for k in range(2):
        pltpu.make_async_copy(ys_hbm.at[pl.ds(0, tm), :], buf.at[slot, k], sem.at[slot]).wait()
    f = buf[slot, 0] * route_ref[:, 2:3] + buf[slot, 1] * route_ref[:, 3:4]
    o_ref[...] = _layer_norm(ALPHA * x1_ref[...] + f, g_ref[...], b_ref[...])


def _combine(pos, ys, x1, route, g, b):
    n, d = x1.shape
    tm = COMB_TM
    grid_spec = pltpu.PrefetchScalarGridSpec(
        num_scalar_prefetch=1,
        grid=(n // tm,),
        in_specs=[
            pl.BlockSpec(memory_space=pl.ANY),
            pl.BlockSpec((tm, d), lambda i, p: (i, 0)),
            pl.BlockSpec((tm, ROUTE_W), lambda i, p: (i, 0)),
            pl.BlockSpec((1, d), lambda i, p: (0, 0)),
            pl.BlockSpec((1, d), lambda i, p: (0, 0)),
        ],
        out_specs=pl.BlockSpec((tm, d), lambda i, p: (i, 0)),
        scratch_shapes=[pltpu.VMEM((2, 2, tm, d), F32), pltpu.SemaphoreType.DMA((2,))],
    )
    return pl.pallas_call(
        functools.partial(_combine_body, tm=tm),
        grid_spec=grid_spec,
        out_shape=jax.ShapeDtypeStruct((n, d), F32),
        compiler_params=_cparams(("arbitrary",)),
    )(pos, ys, x1, route, g, b)


def _dispatch_plan(eid, n_items, n_rows):
    p_total = eid.shape[0]
    blk = 128
    assert p_total % blk == 0
    onehot = (eid[:, None] == jnp.arange(N_EXPERTS, dtype=jnp.int32)[None, :]).astype(F32)
    counts = jnp.sum(onehot, axis=0).astype(jnp.int32)
    ohb = onehot.reshape(p_total // blk, blk, N_EXPERTS)
    earlier = (jnp.arange(blk)[:, None] > jnp.arange(blk)[None, :]).astype(F32)
    within = jnp.einsum("ij,bjk->bik", earlier, ohb, precision=lax.Precision.HIGHEST)
    blk_tot = jnp.sum(ohb, axis=1)
    blk_off = jnp.cumsum(blk_tot, axis=0) - blk_tot
    rank = jnp.sum((within + blk_off[:, None, :]) * ohb, axis=2).reshape(p_total).astype(jnp.int32)
    nblk = (counts + MOE_R - 1) // MOE_R
    seg_start = (jnp.cumsum(nblk) - nblk) * MOE_R
    pos = seg_start[eid] + rank
    src = jnp.zeros((n_rows,), jnp.int32).at[pos].set(jnp.arange(p_total, dtype=jnp.int32) // 2)
    items_per_e = (nblk + MOE_MAXB - 1) // MOE_MAXB
    item_end = jnp.cumsum(items_per_e)
    item_start = item_end - items_per_e
    t = jnp.arange(n_items, dtype=jnp.int32)
    e_t = jnp.minimum(jnp.sum((item_end[None, :] <= t[:, None]).astype(jnp.int32), axis=1), N_EXPERTS - 1)
    live = t < item_end[-1]
    local = t - item_start[e_t]
    used = jnp.sum(nblk)
    idle0 = used + (t - item_end[-1]) * MOE_MAXB
    nz_t = jnp.clip(n_rows // MOE_R - idle0, 0, MOE_MAXB)
    nb_t = jnp.where(live, jnp.clip(nblk[e_t] - local * MOE_MAXB, 0, MOE_MAXB), -nz_t)
    row0_t = jnp.where(live, seg_start[e_t] + local * (MOE_MAXB * MOE_R),
                       jnp.minimum(idle0, n_rows // MOE_R - 1) * MOE_R)
    last_e = e_t[jnp.maximum(item_end[-1] - 1, 0)]
    e_t = jnp.where(live, e_t, last_e)
    return pos.astype(jnp.int32), src, e_t.astype(jnp.int32), row0_t.astype(jnp.int32), nb_t.astype(jnp.int32)


def _alibi_slopes():
    return jnp.asarray([2.0 ** (-8.0 * (h + 1) / N_AH) for h in range(N_AH)], dtype=F32)


def kernel(x_prompt, x_sample, state_conv, state_mlstm_C, state_mlstm_n, state_mlstm_m, cache_win_k, cache_win_v, w_in, b_gate, conv_w, conv_b, mh_gain, att_gain, w_out, ln1_g, ln1_b, w_group, b_group, w_router, b_router, w_gate, w_up, w_down, ln2_g, ln2_b):
    bp, tp, d = x_prompt.shape
    bs, ts, _ = x_sample.shape
    assert d == D_MODEL and w_in.shape[0] == 1 and tp % ATT_BLK == 0 and ts >= CONV_W - 1
    w_buf = cache_win_k.shape[2]
    n_p, n_s = bp * tp, bs * ts
    slopes = _alibi_slopes()

    wi = w_in[0]
    g0 = 4 * D_MLSTM
    g1 = g0 + 2 * N_MH
    w_pack = jnp.concatenate(
        [wi[:, :g0], wi[:, g1:], wi[:, g0:g1], jnp.zeros((d, LANE - 2 * N_MH), F32)], axis=1).astype(BF16)

    xp2 = x_prompt.reshape(n_p, d)
    xs2 = x_sample.reshape(n_s, d)
    proj_p = _in_proj(xp2, w_pack, 512)
    proj_s = _in_proj(xs2, w_pack, n_s)

    cw = conv_w[0]
    cb = conv_b[0][None, :]
    mh_g = mh_gain[0][None, :]
    att_g = att_gain[0][None, :]
    bg = b_gate[0]

    def gates_time_major(proj, batch, seq, pad_to):
        gt = proj[:, COL_G:COL_G + 2 * N_MH].reshape(batch, seq, 2, N_MH).transpose(0, 3, 2, 1)
        if pad_to > seq:
            gt = jnp.pad(gt, ((0, 0), (0, 0), (0, 0), (0, pad_to - seq)))
        return gt

    lc_p = 256
    hm_p, c_p, n_pp, m_p = _mlstm(
        proj_p, gates_time_major(proj_p, bp, tp, tp), jnp.zeros((bp, CONV_W - 1, 2 * D_MLSTM), F32), cw, cb,
        jnp.zeros((bp, N_MH, E_MH, E_MH), F32), jnp.zeros((bp, N_MH, 1, E_MH), F32),
        jnp.zeros((bp, N_MH, 1, 1), F32), mh_g, bg, batch=bp, seq=tp, lb=lc_p, lc=lc_p, out_dtype=BF16)
    ha_p = _attn_prompt(proj_p, att_g, slopes, batch=bp, seq=tp)

    lc_s = 128
    hm_s, c_s, n_ss, m_s = _mlstm(
        proj_s, gates_time_major(proj_s, bs, ts, lc_s), state_conv[0], cw, cb,
        state_mlstm_C[0], state_mlstm_n[0][:, :, None, :], state_mlstm_m[0][:, :, None, None],
        mh_g, bg, batch=bs, seq=ts, lb=ts, lc=lc_s, out_dtype=F32)
    new_rows = lambda col: proj_s[:, col:col + N_AH * E_AH].reshape(bs, ts, N_AH, E_AH)
    ha_s, wk_s, wv_s = _attn_decode(
        new_rows(COL_QA), new_rows(COL_KA), new_rows(COL_VA), cache_win_k, cache_win_v,
        att_gain[0].reshape(N_AH, E_AH), jnp.broadcast_to(slopes[:, None], (N_AH, E_AH)), batch=bs, s_new=ts)
    ha_s = ha_s.reshape(n_s, N_AH * E_AH)

    n_all = n_p + n_s
    wo = w_out[0].astype(BF16)
    w_r = jnp.concatenate(
        [w_group[0], w_router[0].transpose(1, 0, 2).reshape(d, N_EXPERTS),
         jnp.zeros((d, ROUTE_W - N_GROUPS - N_EXPERTS), F32)], axis=1)
    b_r = jnp.concatenate(
        [b_group[0], b_router[0].reshape(N_EXPERTS), jnp.zeros((ROUTE_W - N_GROUPS - N_EXPERTS,), F32)])[None, :]
    wr_hi = w_r.astype(BF16)
    wr_lo = (w_r - wr_hi.astype(F32)).astype(BF16)
    ln1 = (ln1_g[0][None, :], ln1_b[0][None, :])
    x1_p, x1p_p, route_p = _outproj(hm_p, ha_p, xp2, wo[:D_MLSTM], wo[D_MLSTM:], *ln1, wr_hi, wr_lo, b_r)
    x1_s, x1p_s, route_s = _outproj(hm_s, ha_s, xs2, wo[:D_MLSTM], wo[D_MLSTM:], *ln1, wr_hi, wr_lo, b_r)

    p_total = 2 * n_all
    n_rows = ((p_total + N_EXPERTS * (MOE_R - 1)) // MOE_R + 1) * MOE_R
    n_items = N_EXPERTS + n_rows // (MOE_R * MOE_MAXB)
    eid = jnp.concatenate([route_p[:, 0:2], route_s[:, 0:2]], axis=0).astype(jnp.int32).reshape(p_total)
    pos, src, item_e, item_row0, item_nb = _dispatch_plan(eid, n_items, n_rows)
    x1p = jnp.concatenate([x1p_p, x1p_s], axis=0)
    ys = _moe(item_e, item_row0, item_nb, src, x1p, w_gate[0], w_up[0], w_down[0], n_items, n_rows)
    ln2 = (ln2_g[0][None, :], ln2_b[0][None, :])
    y_p = _combine(pos[:2 * n_p], ys, x1_p, route_p, *ln2).reshape(bp, tp, d)
    y_s = _combine(pos[2 * n_p:], ys, x1_s, route_s, *ln2).reshape(bs, ts, d)

    def tail_rows(proj, batch, seq, col, width, rows):
        return proj.reshape(batch, seq, N_PROJ)[:, seq - rows:, col:col + width]

    win = min(w_buf, tp)
    p_conv = tail_rows(proj_p, bp, tp, COL_QM, 2 * D_MLSTM, CONV_W - 1)[None]
    p_wk = tail_rows(proj_p, bp, tp, COL_KA, N_AH * E_AH, win).reshape(1, bp, win, N_AH, E_AH)
    p_wv = tail_rows(proj_p, bp, tp, COL_VA, N_AH * E_AH, win).reshape(1, bp, win, N_AH, E_AH)
    s_conv = tail_rows(proj_s, bs, ts, COL_QM, 2 * D_MLSTM, CONV_W - 1)[None]
    return (y_p, y_s,
            p_conv, c_p[None], n_pp[:, :, 0, :][None], m_p[:, :, 0, 0][None], p_wk, p_wv,
            s_conv, c_s[None], n_ss[:, :, 0, :][None], m_s[:, :, 0, 0][None], wk_s, wv_s)
```

```python
import functools
import math

import jax
import jax.numpy as jnp
from jax import lax
from jax.experimental import pallas as pl
from jax.experimental.pallas import tpu as pltpu

F32 = jnp.float32
BF16 = jnp.bfloat16
NEG_INF = float("-inf")

D_MODEL = 2048
D_MLSTM = 1024
N_MH = 4
E_MH = 256
N_AH = 8
E_AH = 128
DILATED_CONFIGS = ((128, 1), (512, 4), (2048, 16))
N_BACK = 128
BAND = 128
CONV_W = 4
N_GROUPS = 4
E_PER_GROUP = 8
N_EXPERTS = 32
D_FF = 1024
EPS = 1e-5
ALPHA = 2.0 ** 0.25

LANE = 128
SUBLANE = 8

COL_QM, COL_KM, COL_VM, COL_OM = 0, 1024, 2048, 3072
COL_QA, COL_KA, COL_VA, COL_G = 4096, 5120, 6144, 7168
N_PROJ = 7296
PROJ_TN = 2432

VMEM_LIMIT = 56 * 1024 * 1024


def _cparams(sem):
    return pltpu.CompilerParams(dimension_semantics=sem, vmem_limit_bytes=VMEM_LIMIT)


def _proj_body(x_ref, w_ref, o_ref):
    o_ref[...] = jnp.dot(x_ref[...].astype(BF16), w_ref[...], preferred_element_type=F32)


def _in_proj(x, w, tm):
    n = x.shape[0]
    return pl.pallas_call(
        _proj_body,
        grid=(N_PROJ // PROJ_TN, n // tm),
        in_specs=[pl.BlockSpec((tm, D_MODEL), lambda j, i: (i, 0)),
                  pl.BlockSpec((D_MODEL, PROJ_TN), lambda j, i: (0, j))],
        out_specs=pl.BlockSpec((tm, PROJ_TN), lambda j, i: (i, j)),
        out_shape=jax.ShapeDtypeStruct((n, N_PROJ), F32),
        compiler_params=_cparams(("arbitrary", "arbitrary")),
    )(x, w)


def _mlstm_body(bg_ref, xq_ref, xk_ref, v_ref, om_ref, g_ref, hq_ref, hk_ref, cwq_ref, cwk_ref,
                cbq_ref, cbk_ref, c0_ref, n0_ref, m0_ref, gain_ref,
                hm_ref, cout_ref, nout_ref, mout_ref,
                caug, m_s, uq, uk, vbuf, *, lb, lc, t_valid):
    h = pl.program_id(1)
    c = pl.program_id(2)
    nc = pl.num_programs(2)
    e = E_MH

    row_e = lax.broadcasted_iota(jnp.int32, (e, e), 0)
    col_e = lax.broadcasted_iota(jnp.int32, (e, e), 1)
    eye_e = row_e == col_e

    @pl.when(c == 0)
    def _init():
        caug[:, 0:e] = c0_ref[...]
        ncol = jnp.sum(jnp.where(eye_e, n0_ref[...], 0.0), axis=1, keepdims=True)
        caug[:, e:e + LANE] = jnp.broadcast_to(ncol, (e, LANE))
        m_s[...] = m0_ref[...]
        uq[...] = jnp.zeros(uq.shape, F32)
        uk[...] = jnp.zeros(uk.shape, F32)
        uq[pl.ds(SUBLANE - (CONV_W - 1), CONV_W - 1), :] = hq_ref[...]
        uk[pl.ds(SUBLANE - (CONV_W - 1), CONV_W - 1), :] = hk_ref[...]
        if lb != lc:
            vbuf[...] = jnp.zeros(vbuf.shape, F32)

    uq[pl.ds(SUBLANE, lb), :] = xq_ref[...]
    uk[pl.ds(SUBLANE, lb), :] = xk_ref[...]

    def conv_silu(u, cw_ref, cb_ref):
        acc = cb_ref[...]
        for j in range(CONV_W):
            acc = acc + u[pl.ds(SUBLANE - (CONV_W - 1) + j, lc), :] * cw_ref[j:j + 1, :]
        return acc * jax.nn.sigmoid(acc)

    q = conv_silu(uq, cwq_ref, cbq_ref)
    k = conv_silu(uk, cwk_ref, cbk_ref) * (e ** -0.5)
    if lb != lc:
        vbuf[pl.ds(0, lb), :] = v_ref[...]
        v = vbuf[...]
    else:
        v = v_ref[...]

    tq = uq[pl.ds(lc + SUBLANE - (CONV_W - 1), CONV_W - 1), :]
    tk = uk[pl.ds(lc + SUBLANE - (CONV_W - 1), CONV_W - 1), :]
    uq[pl.ds(SUBLANE - (CONV_W - 1), CONV_W - 1), :] = tq
    uk[pl.ds(SUBLANE - (CONV_W - 1), CONV_W - 1), :] = tk

    lane_t = lax.broadcasted_iota(jnp.int32, (1, lc), 1)
    valid = (lane_t + c * lc) < t_valid
    gi = g_ref[0:1, :] + bg_ref[h]
    gf = g_ref[1:2, :] + bg_ref[N_MH + h]
    lf = -(jnp.maximum(-gf, 0.0) + jnp.log1p(jnp.exp(-jnp.abs(gf))))
    ig_row = jnp.where(valid, gi, NEG_INF)
    lf_row = jnp.where(valid, lf, 0.0)

    row_l = lax.broadcasted_iota(jnp.int32, (lc, lc), 0)
    col_l = lax.broadcasted_iota(jnp.int32, (lc, lc), 1)
    causal = col_l <= row_l
    eye_l = col_l == row_l
    b_col = jnp.sum(jnp.where(causal, lf_row, 0.0), axis=1, keepdims=True)
    b_row = jnp.sum(jnp.where(eye_l, b_col, 0.0), axis=0, keepdims=True)
    m_prev = m_s[...]

    dlog = jnp.where(causal, b_col - b_row + ig_row, NEG_INF)
    a_col = b_col + m_prev
    m_t = jnp.maximum(a_col, jnp.max(dlog, axis=1, keepdims=True))
    w_intra = jnp.exp(dlog - m_t)
    w_inter = jnp.exp(a_col - m_t)

    qb = q.astype(BF16)
    kb = k.astype(BF16)
    vb = v.astype(BF16)
    s = lax.dot_general(qb, kb, (((1,), (1,)), ((), ())), preferred_element_type=F32) * w_intra
    qc = jnp.dot(qb, caug[...].astype(BF16), preferred_element_type=F32)
    num = jnp.dot(s.astype(BF16), vb, preferred_element_type=F32) + w_inter * qc[:, 0:e]
    den = jnp.sum(s, axis=1, keepdims=True) + w_inter * qc[:, e:e + 1]
    hh = num / jnp.maximum(jnp.abs(den), jnp.exp(-m_t))

    hh = hh - jnp.mean(hh, axis=1, keepdims=True)
    hh = hh * lax.rsqrt(jnp.mean(hh * hh, axis=1, keepdims=True) + EPS)
    if lb != lc:
        hh = hh[0:lb, :]
    hm = hh * gain_ref[...] * jax.nn.sigmoid(om_ref[...])
    hm_ref[...] = hm.astype(hm_ref.dtype)

    b_last = b_row[:, lc - 1:lc]
    logw_row = b_last - b_row + ig_row
    m_new = jnp.maximum(b_last + m_prev, jnp.max(logw_row, axis=1, keepdims=True))
    w_row = jnp.exp(logw_row - m_new)
    w_col = jnp.sum(jnp.where(eye_l, w_row, 0.0), axis=1, keepdims=True)
    decay = jnp.exp(b_last + m_prev - m_new)
    kw = (k * w_col).astype(BF16)
    vaug = jnp.concatenate([vb, jnp.ones((lc, LANE), BF16)], axis=1)
    upd = lax.dot_general(kw, vaug, (((0,), (0,)), ((), ())), preferred_element_type=F32)
    caug[...] = decay * caug[...] + upd
    m_s[...] = m_new

    @pl.when(c == nc - 1)
    def _fin():
        cout_ref[...] = caug[:, 0:e]
        nout_ref[...] = jnp.sum(jnp.where(eye_e, caug[:, e:e + 1], 0.0), axis=0, keepdims=True)
        mout_ref[...] = jnp.broadcast_to(m_s[...], (1, LANE))


def _mlstm(xp, gates_t, hist, conv_w, conv_b, c0, n0, m0, gain, b_gate, *, batch, seq, lb, lc, out_dtype):
    nc = seq // lb
    e = E_MH
    rb = lambda b, h, c, bg: b * nc + c
    qcol, kcol, vcol, ocol = COL_QM // e, COL_KM // e, COL_VM // e, COL_OM // e
    body = functools.partial(_mlstm_body, lb=lb, lc=lc, t_valid=seq if lb == lc else lb)
    grid_spec = pltpu.PrefetchScalarGridSpec(
        num_scalar_prefetch=1,
        grid=(batch, N_MH, nc),
        in_specs=[
            pl.BlockSpec((lb, e), lambda b, h, c, bg: (rb(b, h, c, bg), qcol + h)),
            pl.BlockSpec((lb, e), lambda b, h, c, bg: (rb(b, h, c, bg), kcol + h)),
            pl.BlockSpec((lb, e), lambda b, h, c, bg: (rb(b, h, c, bg), vcol + h)),
            pl.BlockSpec((lb, e), lambda b, h, c, bg: (rb(b, h, c, bg), ocol + h)),
            pl.BlockSpec((None, None, 2, lc), lambda b, h, c, bg: (b, h, 0, c)),
            pl.BlockSpec((None, CONV_W - 1, e), lambda b, h, c, bg: (b, 0, h)),
            pl.BlockSpec((None, CONV_W - 1, e), lambda b, h, c, bg: (b, 0, N_MH + h)),
            pl.BlockSpec((CONV_W, e), lambda b, h, c, bg: (0, h)),
            pl.BlockSpec((CONV_W, e), lambda b, h, c, bg: (0, N_MH + h)),
            pl.BlockSpec((1, e), lambda b, h, c, bg: (0, h)),
            pl.BlockSpec((1, e), lambda b, h, c, bg: (0, N_MH + h)),
            pl.BlockSpec((None, None, e, e), lambda b, h, c, bg: (b, h, 0, 0)),
            pl.BlockSpec((None, None, 1, e), lambda b, h, c, bg: (b, h, 0, 0)),
            pl.BlockSpec((None, None, 1, 1), lambda b, h, c, bg: (b, h, 0, 0)),
            pl.BlockSpec((1, e), lambda b, h, c, bg: (0, h)),
        ],
        out_specs=[
            pl.BlockSpec((lb, e), lambda b, h, c, bg: (rb(b, h, c, bg), h)),
            pl.BlockSpec((None, None, e, e), lambda b, h, c, bg: (b, h, 0, 0)),
            pl.BlockSpec((None, None, 1, e), lambda b, h, c, bg: (b, h, 0, 0)),
            pl.BlockSpec((None, None, 1, LANE), lambda b, h, c, bg: (b, h, 0, 0)),
        ],
        scratch_shapes=[
            pltpu.VMEM((e, e + LANE), F32),
            pltpu.VMEM((1, 1), F32),
            pltpu.VMEM((lc + 2 * SUBLANE, e), F32),
            pltpu.VMEM((lc + 2 * SUBLANE, e), F32),
            pltpu.VMEM((lc, e), F32),
        ],
    )
    return pl.pallas_call(
        body,
        grid_spec=grid_spec,
        out_shape=[
            jax.ShapeDtypeStruct((batch * seq, D_MLSTM), out_dtype),
            jax.ShapeDtypeStruct((batch, N_MH, e, e), F32),
            jax.ShapeDtypeStruct((batch, N_MH, 1, e), F32),
            jax.ShapeDtypeStruct((batch, N_MH, 1, LANE), F32),
        ],
        compiler_params=_cparams(("arbitrary", "arbitrary", "arbitrary")),
    )(b_gate, xp, xp, xp, xp, gates_t, hist, hist, conv_w, conv_w, conv_b, conv_b, c0, n0, m0, gain)


ATT_BLK = 2048
ATT_GROUP = 8


def _attn_body(sl_ref, q_ref, kc_ref, vc_ref, kp_ref, vp_ref, gain_ref, o_ref, o_s, m_s, l_s):
    blk = pl.program_id(1)
    h = pl.program_id(2)
    slope = sl_ref[h]
    scale = E_AH ** -0.5
    qi = lax.broadcasted_iota(jnp.int32, (BAND, BAND), 0)
    ci = lax.broadcasted_iota(jnp.int32, (BAND, BAND), 1)
    dist_prev = (BAND + qi - ci).astype(F32)
    dist_cur = (qi - ci).astype(F32)
    ok_prev = ci >= qi
    ok_cur = ci <= qi

    def run_units(cfg, d, specs):
        sd = slope * float(d)
        bm_cur = jnp.where(ok_cur, -sd * dist_cur, NEG_INF)
        nt = (((1,), (1,)), ((), ()))
        scores = []
        for prev_pen, q_sl, kp_src, kc_src, _, _ in specs:
            bm_prev = jnp.where(ok_prev, prev_pen - sd * dist_prev, NEG_INF)
            qv = q_ref[q_sl, :].astype(BF16)
            kp = kp_src[0][kp_src[1], :].astype(BF16)
            kc = kc_src[0][kc_src[1], :].astype(BF16)
            s_p = lax.dot_general(qv, kp, nt, preferred_element_type=F32) * scale + bm_prev
            s_c = lax.dot_general(qv, kc, nt, preferred_element_type=F32) * scale + bm_cur
            scores.append((s_p, s_c))
        probs = []
        for s_p, s_c in scores:
            m = jnp.max(jnp.maximum(s_p, s_c), axis=1, keepdims=True)
            p_p = jnp.exp(s_p - m)
            p_c = jnp.exp(s_c - m)
            l = jnp.sum(p_p + p_c, axis=1, keepdims=True)
            probs.append((m, l, p_p.astype(BF16), p_c.astype(BF16)))
        for (_, q_sl, _, _, vp_src, vc_src), (m, l, p_p, p_c) in zip(specs, probs):
            vp = vp_src[0][vp_src[1], :].astype(BF16)
            vc = vc_src[0][vc_src[1], :].astype(BF16)
            o = jnp.dot(p_p, vp, preferred_element_type=F32) + jnp.dot(p_c, vc, preferred_element_type=F32)
            o_s[cfg, q_sl, :] = o
            m_s[cfg, q_sl, :] = jnp.broadcast_to(m, (BAND, E_AH))
            l_s[cfg, q_sl, :] = jnp.broadcast_to(l, (BAND, E_AH))

    first_pen = jnp.where(blk == 0, NEG_INF, 0.0)

    for cfg, (win, d) in enumerate(DILATED_CONFIGS):
        span = BAND * d
        n_u = ATT_BLK // span

        def sl(r, u, _d=d, _span=span):
            if _d == 1:
                if isinstance(u, int):
                    return pl.ds(u * _span, BAND)
                return pl.ds(pl.multiple_of(u * _span, BAND), BAND)
            return pl.ds(r + u * _span, BAND, stride=_d)

        def head_unit(r, d=d, span=span, sl=sl):
            cur = sl(r, 0)
            prv = sl(r + ATT_BLK - span, 0) if d > 1 else pl.ds(ATT_BLK - span, BAND)
            return (first_pen, cur, (kp_ref, prv), (kc_ref, cur), (vp_ref, prv), (vc_ref, cur))

        def tail_unit(r, u, sl=sl):
            cur = sl(r, u)
            prv = sl(r, u - 1)
            return (0.0, cur, (kc_ref, prv), (kc_ref, cur), (vc_ref, prv), (vc_ref, cur))

        specs = [head_unit(r) if u == 0 else tail_unit(r, u) for u in range(n_u) for r in range(d)]
        for g in range(0, len(specs), ATT_GROUP):
            run_units(cfg, d, specs[g:g + ATT_GROUP])

    rows = 256

    def merge(i, _):
        rs = pl.ds(pl.multiple_of(i * rows, rows), rows)
        m0, m1, m2 = m_s[0, rs, :], m_s[1, rs, :], m_s[2, rs, :]
        m_all = jnp.maximum(jnp.maximum(m0, m1), m2)
        e0, e1, e2 = jnp.exp(m0 - m_all), jnp.exp(m1 - m_all), jnp.exp(m2 - m_all)
        num = e0 * o_s[0, rs, :] + e1 * o_s[1, rs, :] + e2 * o_s[2, rs, :]
        den = e0 * l_s[0, rs, :] + e1 * l_s[1, rs, :] + e2 * l_s[2, rs, :]
        ha = num / den
        ha = ha * lax.rsqrt(jnp.mean(ha * ha, axis=1, keepdims=True) + EPS)
        o_ref[rs, :] = (ha * gain_ref[...]).astype(o_ref.dtype)
        return 0

    lax.fori_loop(0, ATT_BLK // rows, merge, 0)


def _attn_prompt(xp, gain, slopes, *, batch, seq):
    nb = seq // ATT_BLK
    qc, kc, vc = COL_QA // E_AH, COL_KA // E_AH, COL_VA // E_AH
    cur = lambda b, i, h, s: b * nb + i
    prev = lambda b, i, h, s: b * nb + jnp.maximum(i - 1, 0)
    grid_spec = pltpu.PrefetchScalarGridSpec(
        num_scalar_prefetch=1,
        grid=(batch, nb, N_AH),
        in_specs=[
            pl.BlockSpec((ATT_BLK, E_AH), lambda b, i, h, s: (cur(b, i, h, s), qc + h)),
            pl.BlockSpec((ATT_BLK, E_AH), lambda b, i, h, s: (cur(b, i, h, s), kc + h)),
            pl.BlockSpec((ATT_BLK, E_AH), lambda b, i, h, s: (cur(b, i, h, s), vc + h)),
            pl.BlockSpec((ATT_BLK, E_AH), lambda b, i, h, s: (prev(b, i, h, s), kc + h)),
            pl.BlockSpec((ATT_BLK, E_AH), lambda b, i, h, s: (prev(b, i, h, s), vc + h)),
            pl.BlockSpec((1, E_AH), lambda b, i, h, s: (0, h)),
        ],
        out_specs=pl.BlockSpec((ATT_BLK, E_AH), lambda b, i, h, s: (cur(b, i, h, s), h)),
        scratch_shapes=[pltpu.VMEM((3, ATT_BLK, E_AH), F32)] * 3,
    )
    return pl.pallas_call(
        _attn_body,
        grid_spec=grid_spec,
        out_shape=jax.ShapeDtypeStruct((batch * seq, N_AH * E_AH), BF16),
        compiler_params=_cparams(("arbitrary", "arbitrary", "arbitrary")),
    )(slopes, xp, xp, xp, xp, xp, gain)


DEC_NEAR = 8


def _near_multiplicity():
    return [sum(1 for win, d in DILATED_CONFIGS if dist % d == 0 and dist <= win) for dist in range(DEC_NEAR)]


def _decode_body(q_ref, kn_ref, vn_ref, kc_ref, vc_ref, slope_ref, gain_ref,
                 ha_ref, ko_hbm, vo_hbm, ktail, vtail, sem, *, w_buf, s_new):
    b = pl.program_id(0)
    keep = w_buf - s_new
    copies = [
        pltpu.make_async_copy(kc_ref.at[0, 0, pl.ds(s_new, keep)], ko_hbm.at[0, b, pl.ds(0, keep)], sem.at[0]),
        pltpu.make_async_copy(vc_ref.at[0, 0, pl.ds(s_new, keep)], vo_hbm.at[0, b, pl.ds(0, keep)], sem.at[1]),
        pltpu.make_async_copy(kn_ref.at[0], ko_hbm.at[0, b, pl.ds(keep, s_new)], sem.at[2]),
        pltpu.make_async_copy(vn_ref.at[0], vo_hbm.at[0, b, pl.ds(keep, s_new)], sem.at[3]),
    ]
    for cp in copies:
        cp.start()

    ktail[pl.ds(0, DEC_NEAR)] = kc_ref[0, 0, pl.ds(w_buf - DEC_NEAR, DEC_NEAR)]
    vtail[pl.ds(0, DEC_NEAR)] = vc_ref[0, 0, pl.ds(w_buf - DEC_NEAR, DEC_NEAR)]
    ktail[pl.ds(DEC_NEAR, s_new)] = kn_ref[0]
    vtail[pl.ds(DEC_NEAR, s_new)] = vn_ref[0]

    scale = E_AH ** -0.5
    slope = slope_ref[:, 0:1]
    near_i = lax.broadcasted_iota(jnp.int32, (DEC_NEAR, 1, 1), 0)
    near_dist = (DEC_NEAR - 1 - near_i).astype(F32)
    near_mult = jnp.zeros((DEC_NEAR, 1, 1), F32)
    for dist, c in enumerate(_near_multiplicity()):
        near_mult = jnp.where(near_i == DEC_NEAR - 1 - dist, float(c), near_mult)

    def far_part(s, win, d):
        n = N_BACK - (DEC_NEAR - 1) // d
        sl = pl.ds(w_buf + s - win, n, stride=d) if d > 1 else pl.ds(w_buf + s - win, n)
        i = lax.broadcasted_iota(jnp.int32, (n, 1, 1), 0)
        return sl, ((N_BACK - i) * d).astype(F32)

    def one_query(s, _):
        q = q_ref[0, s]
        parts = []
        for win, d in DILATED_CONFIGS:
            sl, dist = far_part(s, win, d)
            sc = jnp.sum(kc_ref[0, 0, sl] * q, axis=-1, keepdims=True) * scale - slope * dist
            parts.append((sc, None, lambda sl=sl: vc_ref[0, 0, sl]))
        nsl = pl.ds(s + 1, DEC_NEAR)
        sc = jnp.sum(ktail[nsl] * q, axis=-1, keepdims=True) * scale - slope * near_dist
        parts.append((sc, near_mult, lambda: vtail[nsl]))
        m = functools.reduce(jnp.maximum, [jnp.max(p[0], axis=0, keepdims=True) for p in parts])
        den = jnp.zeros((1, N_AH, 1), F32)
        o = jnp.zeros((1, N_AH, E_AH), F32)
        for sc, mu, load_v in parts:
            p = jnp.exp(sc - m)
            if mu is not None:
                p = p * mu
            den = den + jnp.sum(p, axis=0, keepdims=True)
            o = o + jnp.sum(p * load_v(), axis=0, keepdims=True)
        o = (o / den)[0]
        o = o * lax.rsqrt(jnp.mean(o * o, axis=-1, keepdims=True) + EPS)
        ha_ref[0, s] = o * gain_ref[...]
        return 0

    lax.fori_loop(0, s_new, one_query, 0)
    for cp in copies:
        cp.wait()


def _attn_decode(q3, kn3, vn3, cache_k, cache_v, gain, slopes, *, batch, s_new):
    w_buf = cache_k.shape[2]
    assert w_buf >= max(w for w, _ in DILATED_CONFIGS) and s_new <= DEC_NEAR
    assert all(w // d == N_BACK and d & (d - 1) == 0 for w, d in DILATED_CONFIGS)
    body = functools.partial(_decode_body, w_buf=w_buf, s_new=s_new)
    new_spec = pl.BlockSpec((1, s_new, N_AH, E_AH), lambda b: (b, 0, 0, 0))
    cache_spec = pl.BlockSpec((1, 1, w_buf, N_AH, E_AH), lambda b: (0, b, 0, 0, 0))
    tile_spec = pl.BlockSpec((N_AH, E_AH), lambda b: (0, 0))
    return pl.pallas_call(
        body,
        grid=(batch,),
        in_specs=[new_spec, new_spec, new_spec, cache_spec, cache_spec, tile_spec, tile_spec],
        out_specs=[new_spec, pl.BlockSpec(memory_space=pl.ANY), pl.BlockSpec(memory_space=pl.ANY)],
        out_shape=[
            jax.ShapeDtypeStruct((batch, s_new, N_AH, E_AH), F32),
            jax.ShapeDtypeStruct(cache_k.shape, F32),
            jax.ShapeDtypeStruct(cache_v.shape, F32),
        ],
        scratch_shapes=[pltpu.VMEM((DEC_NEAR + s_new, N_AH, E_AH), F32)] * 2 + [pltpu.SemaphoreType.DMA((4,))],
        compiler_params=_cparams(("arbitrary",)),
    )(q3, kn3, vn3, cache_k, cache_v, slopes, gain)


OP_TM = 512
ROUTE_W = LANE
PACK_ROWS = D_MODEL // LANE


def _layer_norm(z, g, b):
    mu = jnp.mean(z, axis=1, keepdims=True)
    zc = z - mu
    var = jnp.mean(zc * zc, axis=1, keepdims=True)
    return zc * lax.rsqrt(var + EPS) * g + b


def _outproj_body(hm_ref, ha_ref, x_ref, wm_ref, wa_ref, g_ref, b_ref, wrh_ref, wrl_ref, br_ref,
                  x1_ref, x1p_ref, route_ref):
    y = (jnp.dot(hm_ref[...].astype(BF16), wm_ref[...], preferred_element_type=F32)
         + jnp.dot(ha_ref[...].astype(BF16), wa_ref[...], preferred_element_type=F32))
    x1 = _layer_norm(ALPHA * x_ref[...] + y, g_ref[...], b_ref[...])
    x1_ref[...] = x1
    tm = x1.shape[0]
    for j in range(PACK_ROWS):
        x1p_ref[pl.ds(j, tm, stride=PACK_ROWS), :] = x1[:, j * LANE:(j + 1) * LANE]
    x1_hi = x1.astype(BF16)
    x1_lo = (x1 - x1_hi.astype(F32)).astype(BF16)
    logits = (jnp.dot(x1_hi, wrh_ref[...], preferred_element_type=F32)
              + jnp.dot(x1_lo, wrh_ref[...], preferred_element_type=F32)
              + jnp.dot(x1_hi, wrl_ref[...], preferred_element_type=F32)) + br_ref[...]
    lane = lax.broadcasted_iota(jnp.int32, (tm, ROUTE_W), 1)
    lane_f = lane.astype(F32)
    big = float(ROUTE_W)
    gl = jnp.where(lane < N_GROUPS, logits, NEG_INF)
    gmax = jnp.max(gl, axis=1, keepdims=True)
    g_w = 1.0 / jnp.sum(jnp.exp(gl - gmax), axis=1, keepdims=True)
    g_idx = jnp.min(jnp.where(gl == gmax, lane_f, big), axis=1, keepdims=True)
    lo = N_GROUPS + E_PER_GROUP * g_idx
    el = jnp.where(jnp.logical_and(lane_f >= lo, lane_f < lo + E_PER_GROUP), logits, NEG_INF)
    v1 = jnp.max(el, axis=1, keepdims=True)
    i1 = jnp.min(jnp.where(el == v1, lane_f, big), axis=1, keepdims=True)
    el2 = jnp.where(lane_f == i1, NEG_INF, el)
    v2 = jnp.max(el2, axis=1, keepdims=True)
    i2 = jnp.min(jnp.where(el2 == v2, lane_f, big), axis=1, keepdims=True)
    e2 = jnp.exp(v2 - v1)
    w1 = g_w / (1.0 + e2)
    w2 = g_w * e2 / (1.0 + e2)
    route = jnp.where(lane == 0, i1 - N_GROUPS,
                      jnp.where(lane == 1, i2 - N_GROUPS,
                                jnp.where(lane == 2, w1, jnp.where(lane == 3, w2, 0.0))))
    route_ref[...] = route


def _outproj(hm, ha, x, wm, wa, g, b, wr_hi, wr_lo, br):
    n = x.shape[0]
    tm = min(OP_TM, n)
    assert n % tm == 0
    row = lambda i: (i, 0)
    fixed = lambda i: (0, 0)
    once = pl.Buffered(1)
    return pl.pallas_call(
        _outproj_body,
        grid=(n // tm,),
        in_specs=[
            pl.BlockSpec((tm, D_MLSTM), row),
            pl.BlockSpec((tm, N_AH * E_AH), row),
            pl.BlockSpec((tm, D_MODEL), row),
            pl.BlockSpec((D_MLSTM, D_MODEL), fixed, pipeline_mode=once),
            pl.BlockSpec((N_AH * E_AH, D_MODEL), fixed, pipeline_mode=once),
            pl.BlockSpec((1, D_MODEL), fixed),
            pl.BlockSpec((1, D_MODEL), fixed),
            pl.BlockSpec((D_MODEL, ROUTE_W), fixed),
            pl.BlockSpec((D_MODEL, ROUTE_W), fixed),
            pl.BlockSpec((1, ROUTE_W), fixed),
        ],
        out_specs=[pl.BlockSpec((tm, D_MODEL), row), pl.BlockSpec((tm * PACK_ROWS, LANE), row),
                   pl.BlockSpec((tm, ROUTE_W), row)],
        out_shape=[jax.ShapeDtypeStruct((n, D_MODEL), F32), jax.ShapeDtypeStruct((n * PACK_ROWS, LANE), F32),
                   jax.ShapeDtypeStruct((n, ROUTE_W), F32)],
        compiler_params=_cparams(("arbitrary",)),
    )(hm, ha, x, wm, wa, g, b, wr_hi, wr_lo, br)


MOE_R = 128
MOE_MAXB = 6
MOE_FC = 256
MOE_NC = D_FF // MOE_FC
MOE_ISSUE_UNROLL = 8


def _moe_body(ex_ref, row0_ref, nb_ref, src_ref, x1p_hbm, wg_ref, wu_ref, wd_ref, ys_hbm,
              ubuf, xbuf, acc, sem_in, sem_out):
    t = pl.program_id(0)
    c = pl.program_id(1)
    n_items = pl.num_programs(0)
    slot = t % 2
    nb = nb_ref[t]

    def rows(base, r):
        return pl.ds(pl.multiple_of(base + r * MOE_R, MOE_R), MOE_R)

    def tile_rows(i):
        return pl.ds(pl.multiple_of(i * PACK_ROWS, PACK_ROWS), PACK_ROWS)

    def out_copy(tt, sl, r):
        return pltpu.make_async_copy(acc.at[sl, rows(0, r), :], ys_hbm.at[rows(row0_ref[tt], r), :], sem_out.at[sl])

    def for_blocks(n, fn):
        def body(r, _):
            fn(r)
            return 0

        lax.fori_loop(0, n, body, 0)

    def n_in(tt):
        return jnp.maximum(nb_ref[tt], 0)

    def n_out(tt):
        return jnp.abs(nb_ref[tt])

    def start_gather(tt, sl):
        base = row0_ref[tt]

        def group(gi):
            for k in range(MOE_ISSUE_UNROLL):
                i = gi * MOE_ISSUE_UNROLL + k
                tok = src_ref[base + i]
                pltpu.make_async_copy(x1p_hbm.at[tile_rows(tok), :], ubuf.at[sl, tile_rows(i), :], sem_in.at[sl]).start()

        for_blocks(n_in(tt) * (MOE_R // MOE_ISSUE_UNROLL), group)

    def wait_gather(tt, sl):
        def block(r):
            span = pl.ds(pl.multiple_of(r * (MOE_R * PACK_ROWS), MOE_R * PACK_ROWS), MOE_R * PACK_ROWS)
            pltpu.make_async_copy(x1p_hbm.at[pl.ds(0, MOE_R * PACK_ROWS), :], ubuf.at[sl, span, :], sem_in.at[sl]).wait()

        for_blocks(n_in(tt), block)

    def unpack_block(r):
        for j in range(PACK_ROWS):
            start = r * (MOE_R * PACK_ROWS) + j
            u = ubuf[slot, pl.ds(start, MOE_R, stride=PACK_ROWS), :]
            xbuf[rows(0, r), j * LANE:(j + 1) * LANE] = u.astype(BF16)

    @pl.when(c == 0)
    def _begin():
        @pl.when(t == 0)
        def _():
            start_gather(0, 0)

        @pl.when(t + 1 < n_items)
        def _():
            start_gather(t + 1, 1 - slot)

        wait_gather(t, slot)
        for_blocks(n_in(t), unpack_block)

        @pl.when(t >= 2)
        def _():
            for_blocks(n_out(t - 2), lambda r: out_copy(t - 2, slot, r).wait())

        @pl.when(nb < 0)
        def _():
            acc[slot] = jnp.zeros(acc.shape[1:], F32)

    wg = wg_ref[...].astype(BF16)
    wu = wu_ref[...].astype(BF16)
    wd = wd_ref[...].astype(BF16)

    def piece(off, size):
        rs = pl.ds(pl.multiple_of(off, MOE_R), size)
        x = xbuf[rs, :]
        gt = jnp.dot(x, wg, preferred_element_type=F32)
        up = jnp.dot(x, wu, preferred_element_type=F32)
        hid = (gt * jax.nn.sigmoid(gt) * up).astype(BF16)
        y = jnp.dot(hid, wd, preferred_element_type=F32)

        @pl.when(c == 0)
        def _():
            acc[slot, rs, :] = y

        @pl.when(c > 0)
        def _():
            acc[slot, rs, :] = acc[slot, rs, :] + y

    nbp = jnp.maximum(nb, 0)
    n4 = nbp // 4
    rem = nbp - 4 * n4
    for_blocks(n4, lambda i: piece(i * (4 * MOE_R), 4 * MOE_R))

    @pl.when(rem >= 2)
    def _():
        piece(n4 * (4 * MOE_R), 2 * MOE_R)

    @pl.when(rem % 2 == 1)
    def _():
        piece(n4 * (4 * MOE_R) + (rem // 2) * (2 * MOE_R), MOE_R)

    @pl.when(c == MOE_NC - 1)
    def _end():
        for_blocks(n_out(t), lambda r: out_copy(t, slot, r).start())

        @pl.when(t == n_items - 1)
        def _():
            for_blocks(n_out(t), lambda r: out_copy(t, slot, r).wait())

            @pl.when(t >= 1)
            def _():
                for_blocks(n_out(t - 1), lambda r: out_copy(t - 1, 1 - slot, r).wait())


def _moe(item_e, item_row0, item_nb, src, x1p, w_gate, w_up, w_down, n_items, n_rows):
    def chunk(t, c, n):
        return jnp.where(n[t] > 0, c, MOE_NC - 1)

    grid_spec = pltpu.PrefetchScalarGridSpec(
        num_scalar_prefetch=4,
        grid=(n_items, MOE_NC),
        in_specs=[
            pl.BlockSpec(memory_space=pl.ANY),
            pl.BlockSpec((None, D_MODEL, MOE_FC), lambda t, c, e, r, n, s: (e[t], 0, chunk(t, c, n))),
            pl.BlockSpec((None, D_MODEL, MOE_FC), lambda t, c, e, r, n, s: (e[t], 0, chunk(t, c, n))),
            pl.BlockSpec((None, MOE_FC, D_MODEL), lambda t, c, e, r, n, s: (e[t], chunk(t, c, n), 0)),
        ],
        out_specs=pl.BlockSpec(memory_space=pl.ANY),
        scratch_shapes=[
            pltpu.VMEM((2, MOE_MAXB * MOE_R * PACK_ROWS, LANE), F32),
            pltpu.VMEM((MOE_MAXB * MOE_R, D_MODEL), BF16),
            pltpu.VMEM((2, MOE_MAXB * MOE_R, D_MODEL), F32),
            pltpu.SemaphoreType.DMA((2,)),
            pltpu.SemaphoreType.DMA((2,)),
        ],
    )
    return pl.pallas_call(
        _moe_body,
        grid_spec=grid_spec,
        out_shape=jax.ShapeDtypeStruct((n_rows, D_MODEL), F32),
        compiler_params=_cparams(("arbitrary", "arbitrary")),
    )(item_e, item_row0, item_nb, src, x1p, w_gate, w_up, w_down)


COMB_TM = 128


def _comb_issue(pos_ref, ys_hbm, buf, sem, tile, slot, tm):
    def body(r, _):
        for k in range(2):
            p = pos_ref[2 * (tile * tm + r) + k]
            pltpu.make_async_copy(ys_hbm.at[pl.ds(p, 1), :], buf.at[slot, k, pl.ds(r, 1), :], sem.at[slot]).start()
        return 0

    lax.fori_loop(0, tm, body, 0)


def _combine_body(pos_ref, ys_hbm, x1_ref, route_ref, g_ref, b_ref, o_ref, buf, sem, *, tm):
    i = pl.program_id(0)
    n = pl.num_programs(0)
    slot = i % 2

    @pl.when(i == 0)
    def _():
        _comb_issue(pos_ref, ys_hbm, buf, sem, 0, 0, tm)

    @pl.when(i + 1 < n)
    def _():
        _comb_issue(pos_ref, ys_hbm, buf, sem, i + 1, 1 - slot, tm)

    for k in range(2):
        pltpu.make_async_copy(ys_hbm.at[pl.ds(0, tm), :], buf.at[slot, k], sem.at[slot]).wait()
    f = buf[slot, 0] * route_ref[:, 2:3] + buf[slot, 1] * route_ref[:, 3:4]
    o_ref[...] = _layer_norm(ALPHA * x1_ref[...] + f, g_ref[...], b_ref[...])


def _combine(pos, ys, x1, route, g, b):
    n, d = x1.shape
    tm = COMB_TM
    grid_spec = pltpu.PrefetchScalarGridSpec(
        num_scalar_prefetch=1,
        grid=(n // tm,),
        in_specs=[
            pl.BlockSpec(memory_space=pl.ANY),
            pl.BlockSpec((tm, d), lambda i, p: (i, 0)),
            pl.BlockSpec((tm, ROUTE_W), lambda i, p: (i, 0)),
            pl.BlockSpec((1, d), lambda i, p: (0, 0)),
            pl.BlockSpec((1, d), lambda i, p: (0, 0)),
        ],
        out_specs=pl.BlockSpec((tm, d), lambda i, p: (i, 0)),
        scratch_shapes=[pltpu.VMEM((2, 2, tm, d), F32), pltpu.SemaphoreType.DMA((2,))],
    )
    return pl.pallas_call(
        functools.partial(_combine_body, tm=tm),
        grid_spec=grid_spec,
        out_shape=jax.ShapeDtypeStruct((n, d), F32),
        compiler_params=_cparams(("arbitrary",)),
    )(pos, ys, x1, route, g, b)


def _dispatch_plan(eid, n_items, n_rows):
    p_total = eid.shape[0]
    blk = 128
    assert p_total % blk == 0
    onehot = (eid[:, None] == jnp.arange(N_EXPERTS, dtype=jnp.int32)[None, :]).astype(F32)
    counts = jnp.sum(onehot, axis=0).astype(jnp.int32)
    ohb = onehot.reshape(p_total // blk, blk, N_EXPERTS)
    earlier = (jnp.arange(blk)[:, None] > jnp.arange(blk)[None, :]).astype(F32)
    within = jnp.einsum("ij,bjk->bik", earlier, ohb, precision=lax.Precision.HIGHEST)
    blk_tot = jnp.sum(ohb, axis=1)
    blk_off = jnp.cumsum(blk_tot, axis=0) - blk_tot
    rank = jnp.sum((within + blk_off[:, None, :]) * ohb, axis=2).reshape(p_total).astype(jnp.int32)
    nblk = (counts + MOE_R - 1) // MOE_R
    seg_start = (jnp.cumsum(nblk) - nblk) * MOE_R
    pos = seg_start[eid] + rank
    src = jnp.zeros((n_rows,), jnp.int32).at[pos].set(jnp.arange(p_total, dtype=jnp.int32) // 2)
    items_per_e = (nblk + MOE_MAXB - 1) // MOE_MAXB
    item_end = jnp.cumsum(items_per_e)
    item_start = item_end - items_per_e
    t = jnp.arange(n_items, dtype=jnp.int32)
    e_t = jnp.minimum(jnp.sum((item_end[None, :] <= t[:, None]).astype(jnp.int32), axis=1), N_EXPERTS - 1)
    live = t < item_end[-1]
    local = t - item_start[e_t]
    used = jnp.sum(nblk)
    idle0 = used + (t - item_end[-1]) * MOE_MAXB
    nz_t = jnp.clip(n_rows // MOE_R - idle0, 0, MOE_MAXB)
    nb_t = jnp.where(live, jnp.clip(nblk[e_t] - local * MOE_MAXB, 0, MOE_MAXB), -nz_t)
    row0_t = jnp.where(live, seg_start[e_t] + local * (MOE_MAXB * MOE_R),
                       jnp.minimum(idle0, n_rows // MOE_R - 1) * MOE_R)
    last_e = e_t[jnp.maximum(item_end[-1] - 1, 0)]
    e_t = jnp.where(live, e_t, last_e)
    return pos.astype(jnp.int32), src, e_t.astype(jnp.int32), row0_t.astype(jnp.int32), nb_t.astype(jnp.int32)


def _alibi_slopes():
    return jnp.asarray([2.0 ** (-8.0 * (h + 1) / N_AH) for h in range(N_AH)], dtype=F32)


def kernel(x_prompt, x_sample, state_conv, state_mlstm_C, state_mlstm_n, state_mlstm_m, cache_win_k, cache_win_v, w_in, b_gate, conv_w, conv_b, mh_gain, att_gain, w_out, ln1_g, ln1_b, w_group, b_group, w_router, b_router, w_gate, w_up, w_down, ln2_g, ln2_b):
    bp, tp, d = x_prompt.shape
    bs, ts, _ = x_sample.shape
    assert d == D_MODEL and w_in.shape[0] == 1 and tp % ATT_BLK == 0 and ts >= CONV_W - 1
    w_buf = cache_win_k.shape[2]
    n_p, n_s = bp * tp, bs * ts
    slopes = _alibi_slopes()

    wi = w_in[0]
    g0 = 4 * D_MLSTM
    g1 = g0 + 2 * N_MH
    w_pack = jnp.concatenate(
        [wi[:, :g0], wi[:, g1:], wi[:, g0:g1], jnp.zeros((d, LANE - 2 * N_MH), F32)], axis=1).astype(BF16)

    xp2 = x_prompt.reshape(n_p, d)
    xs2 = x_sample.reshape(n_s, d)
    proj_p = _in_proj(xp2, w_pack, 512)
    proj_s = _in_proj(xs2, w_pack, n_s)

    cw = conv_w[0]
    cb = conv_b[0][None, :]
    mh_g = mh_gain[0][None, :]
    att_g = att_gain[0][None, :]
    bg = b_gate[0]

    def gates_time_major(proj, batch, seq, pad_to):
        gt = proj[:, COL_G:COL_G + 2 * N_MH].reshape(batch, seq, 2, N_MH).transpose(0, 3, 2, 1)
        if pad_to > seq:
            gt = jnp.pad(gt, ((0, 0), (0, 0), (0, 0), (0, pad_to - seq)))
        return gt

    lc_p = 256
    hm_p, c_p, n_pp, m_p = _mlstm(
        proj_p, gates_time_major(proj_p, bp, tp, tp), jnp.zeros((bp, CONV_W - 1, 2 * D_MLSTM), F32), cw, cb,
        jnp.zeros((bp, N_MH, E_MH, E_MH), F32), jnp.zeros((bp, N_MH, 1, E_MH), F32),
        jnp.zeros((bp, N_MH, 1, 1), F32), mh_g, bg, batch=bp, seq=tp, lb=lc_p, lc=lc_p, out_dtype=BF16)
    ha_p = _attn_prompt(proj_p, att_g, slopes, batch=bp, seq=tp)

    lc_s = 128
    hm_s, c_s, n_ss, m_s = _mlstm(
        proj_s, gates_time_major(proj_s, bs, ts, lc_s), state_conv[0], cw, cb,
        state_mlstm_C[0], state_mlstm_n[0][:, :, None, :], state_mlstm_m[0][:, :, None, None],
        mh_g, bg, batch=bs, seq=ts, lb=ts, lc=lc_s, out_dtype=F32)
    new_rows = lambda col: proj_s[:, col:col + N_AH * E_AH].reshape(bs, ts, N_AH, E_AH)
    ha_s, wk_s, wv_s = _attn_decode(
        new_rows(COL_QA), new_rows(COL_KA), new_rows(COL_VA), cache_win_k, cache_win_v,
        att_gain[0].reshape(N_AH, E_AH), jnp.broadcast_to(slopes[:, None], (N_AH, E_AH)), batch=bs, s_new=ts)
    ha_s = ha_s.reshape(n_s, N_AH * E_AH)

    n_all = n_p + n_s
    wo = w_out[0].astype(BF16)
    w_r = jnp.concatenate(
        [w_group[0], w_router[0].transpose(1, 0, 2).reshape(d, N_EXPERTS),
         jnp.zeros((d, ROUTE_W - N_GROUPS - N_EXPERTS), F32)], axis=1)
    b_r = jnp.concatenate(
        [b_group[0], b_router[0].reshape(N_EXPERTS), jnp.zeros((ROUTE_W - N_GROUPS - N_EXPERTS,), F32)])[None, :]
    wr_hi = w_r.astype(BF16)
    wr_lo = (w_r - wr_hi.astype(F32)).astype(BF16)
    ln1 = (ln1_g[0][None, :], ln1_b[0][None, :])
    x1_p, x1p_p, route_p = _outproj(hm_p, ha_p, xp2, wo[:D_MLSTM], wo[D_MLSTM:], *ln1, wr_hi, wr_lo, b_r)
    x1_s, x1p_s, route_s = _outproj(hm_s, ha_s, xs2, wo[:D_MLSTM], wo[D_MLSTM:], *ln1, wr_hi, wr_lo, b_r)

    p_total = 2 * n_all
    n_rows = ((p_total + N_EXPERTS * (MOE_R - 1)) // MOE_R + 1) * MOE_R
    n_items = N_EXPERTS + n_rows // (MOE_R * MOE_MAXB)
    eid = jnp.concatenate([route_p[:, 0:2], route_s[:, 0:2]], axis=0).astype(jnp.int32).reshape(p_total)
    pos, src, item_e, item_row0, item_nb = _dispatch_plan(eid, n_items, n_rows)
    x1p = jnp.concatenate([x1p_p, x1p_s], axis=0)
    ys = _moe(item_e, item_row0, item_nb, src, x1p, w_gate[0], w_up[0], w_down[0], n_items, n_rows)
    ln2 = (ln2_g[0][None, :], ln2_b[0][None, :])
    y_p = _combine(pos[:2 * n_p], ys, x1_p, route_p, *ln2).reshape(bp, tp, d)
    y_s = _combine(pos[2 * n_p:], ys, x1_s, route_s, *ln2).reshape(bs, ts, d)

    def tail_rows(proj, batch, seq, col, width, rows):
        return proj.reshape(batch, seq, N_PROJ)[:, seq - rows:, col:col + width]

    win = min(w_buf, tp)
    p_conv = tail_rows(proj_p, bp, tp, COL_QM, 2 * D_MLSTM, CONV_W - 1)[None]
    p_wk = tail_rows(proj_p, bp, tp, COL_KA, N_AH * E_AH, win).reshape(1, bp, win, N_AH, E_AH)
    p_wv = tail_rows(proj_p, bp, tp, COL_VA, N_AH * E_AH, win).reshape(1, bp, win, N_AH, E_AH)
    s_conv = tail_rows(proj_s, bs, ts, COL_QM, 2 * D_MLSTM, CONV_W - 1)[None]
    return (y_p, y_s,
            p_conv, c_p[None], n_pp[:, :, 0, :][None], m_p[:, :, 0, 0][None], p_wk, p_wv,
            s_conv, c_s[None], n_ss[:, :, 0, :][None], m_s[:, :, 0, 0][None], wk_s, wv_s)
```

```python
import functools
import math

import jax
import jax.numpy as jnp
from jax import lax
from jax.experimental import pallas as pl
from jax.experimental.pallas import tpu as pltpu

F32 = jnp.float32
BF16 = jnp.bfloat16
NEG_INF = float("-inf")

D_MODEL = 2048
D_MLSTM = 1024
N_MH = 4
E_MH = 256
N_AH = 8
E_AH = 128
DILATED_CONFIGS = ((128, 1), (512, 4), (2048, 16))
N_BACK = 128
BAND = 128
CONV_W = 4
N_GROUPS = 4
E_PER_GROUP = 8
N_EXPERTS = 32
D_FF = 1024
EPS = 1e-5
ALPHA = 2.0 ** 0.25

LANE = 128
SUBLANE = 8

COL_QM, COL_KM, COL_VM, COL_OM = 0, 1024, 2048, 3072
COL_QA, COL_KA, COL_VA, COL_G = 4096, 5120, 6144, 7168
N_PROJ = 7296
PROJ_TN = 2432

VMEM_LIMIT = 56 * 1024 * 1024


def _cparams(sem):
    return pltpu.CompilerParams(dimension_semantics=sem, vmem_limit_bytes=VMEM_LIMIT)


def _proj_body(x_ref, w_ref, o_ref):
    o_ref[...] = jnp.dot(x_ref[...].astype(BF16), w_ref[...], preferred_element_type=F32)


def _in_proj(x, w, tm):
    n = x.shape[0]
    return pl.pallas_call(
        _proj_body,
        grid=(N_PROJ // PROJ_TN, n // tm),
        in_specs=[pl.BlockSpec((tm, D_MODEL), lambda j, i: (i, 0)),
                  pl.BlockSpec((D_MODEL, PROJ_TN), lambda j, i: (0, j))],
        out_specs=pl.BlockSpec((tm, PROJ_TN), lambda j, i: (i, j)),
        out_shape=jax.ShapeDtypeStruct((n, N_PROJ), F32),
        compiler_params=_cparams(("arbitrary", "arbitrary")),
    )(x, w)


def _mlstm_body(bg_ref, xq_ref, xk_ref, v_ref, om_ref, g_ref, hq_ref, hk_ref, cwq_ref, cwk_ref,
                cbq_ref, cbk_ref, c0_ref, n0_ref, m0_ref, gain_ref,
                hm_ref, cout_ref, nout_ref, mout_ref,
                caug, m_s, uq, uk, vbuf, *, lb, lc, t_valid, hb):
    hg = pl.program_id(1)
    c = pl.program_id(2)
    nc = pl.num_programs(2)
    e = E_MH
    heads = range(hb)
    cols = lambda i: slice(i * e, (i + 1) * e)

    row_e = lax.broadcasted_iota(jnp.int32, (e, e), 0)
    col_e = lax.broadcasted_iota(jnp.int32, (e, e), 1)
    eye_e = row_e == col_e

    @pl.when(c == 0)
    def _init():
        for i in heads:
            caug[i, :, 0:e] = c0_ref[i]
            ncol = jnp.sum(jnp.where(eye_e, n0_ref[i], 0.0), axis=1, keepdims=True)
            caug[i, :, e:e + LANE] = jnp.broadcast_to(ncol, (e, LANE))
            m_s[i] = m0_ref[i]
        uq[...] = jnp.zeros(uq.shape, F32)
        uk[...] = jnp.zeros(uk.shape, F32)
        uq[pl.ds(SUBLANE - (CONV_W - 1), CONV_W - 1), :] = hq_ref[...]
        uk[pl.ds(SUBLANE - (CONV_W - 1), CONV_W - 1), :] = hk_ref[...]
        if lb != lc:
            vbuf[...] = jnp.zeros(vbuf.shape, F32)

    uq[pl.ds(SUBLANE, lb), :] = xq_ref[...]
    uk[pl.ds(SUBLANE, lb), :] = xk_ref[...]

    def conv_silu(u, cw_ref, cb_ref):
        acc = cb_ref[...]
        for j in range(CONV_W):
            acc = acc + u[pl.ds(SUBLANE - (CONV_W - 1) + j, lc), :] * cw_ref[j:j + 1, :]
        return acc * jax.nn.sigmoid(acc)

    q_all = conv_silu(uq, cwq_ref, cbq_ref)
    k_all = conv_silu(uk, cwk_ref, cbk_ref) * (e ** -0.5)
    if lb != lc:
        vbuf[pl.ds(0, lb), :] = v_ref[...]
        v_all = vbuf[...]
    else:
        v_all = v_ref[...]

    tq = uq[pl.ds(lc + SUBLANE - (CONV_W - 1), CONV_W - 1), :]
    tk = uk[pl.ds(lc + SUBLANE - (CONV_W - 1), CONV_W - 1), :]
    uq[pl.ds(SUBLANE - (CONV_W - 1), CONV_W - 1), :] = tq
    uk[pl.ds(SUBLANE - (CONV_W - 1), CONV_W - 1), :] = tk

    lane_t = lax.broadcasted_iota(jnp.int32, (1, lc), 1)
    valid = (lane_t + c * lc) < t_valid
    row_l = lax.broadcasted_iota(jnp.int32, (lc, lc), 0)
    col_l = lax.broadcasted_iota(jnp.int32, (lc, lc), 1)
    causal = col_l <= row_l
    eye_l = col_l == row_l
    nt = (((1,), (1,)), ((), ()))
    tn = (((0,), (0,)), ((), ()))

    st = []
    for i in heads:
        h = hg * hb + i
        gi = g_ref[i, 0:1, :] + bg_ref[h]
        gf = g_ref[i, 1:2, :] + bg_ref[N_MH + h]
        lf = -(jnp.maximum(-gf, 0.0) + jnp.log1p(jnp.exp(-jnp.abs(gf))))
        ig_row = jnp.where(valid, gi, NEG_INF)
        lf_row = jnp.where(valid, lf, 0.0)
        b_col = jnp.sum(jnp.where(causal, lf_row, 0.0), axis=1, keepdims=True)
        b_row = jnp.sum(jnp.where(eye_l, b_col, 0.0), axis=0, keepdims=True)
        m_prev = m_s[i]
        dlog = jnp.where(causal, b_col - b_row + ig_row, NEG_INF)
        a_col = b_col + m_prev
        m_t = jnp.maximum(a_col, jnp.max(dlog, axis=1, keepdims=True))
        st.append(dict(ig_row=ig_row, b_row=b_row, m_prev=m_prev, m_t=m_t,
                       w_intra=jnp.exp(dlog - m_t), w_inter=jnp.exp(a_col - m_t),
                       qb=q_all[:, cols(i)].astype(BF16), kb=k_all[:, cols(i)].astype(BF16),
                       vb=v_all[:, cols(i)].astype(BF16)))

    for i in heads:
        d = st[i]
        d["s"] = lax.dot_general(d["qb"], d["kb"], nt, preferred_element_type=F32) * d["w_intra"]
        d["qc"] = jnp.dot(d["qb"], caug[i].astype(BF16), preferred_element_type=F32)

    for i in heads:
        d = st[i]
        num = jnp.dot(d["s"].astype(BF16), d["vb"], preferred_element_type=F32) + d["w_inter"] * d["qc"][:, 0:e]
        den = jnp.sum(d["s"], axis=1, keepdims=True) + d["w_inter"] * d["qc"][:, e:e + 1]
        hh = num / jnp.maximum(jnp.abs(den), jnp.exp(-d["m_t"]))
        hh = hh - jnp.mean(hh, axis=1, keepdims=True)
        hh = hh * lax.rsqrt(jnp.mean(hh * hh, axis=1, keepdims=True) + EPS)
        if lb != lc:
            hh = hh[0:lb, :]
        hm = hh * gain_ref[:, cols(i)] * jax.nn.sigmoid(om_ref[:, cols(i)])
        hm_ref[:, cols(i)] = hm.astype(hm_ref.dtype)

    for i in heads:
        d = st[i]
        b_last = d["b_row"][:, lc - 1:lc]
        logw_row = b_last - d["b_row"] + d["ig_row"]
        m_new = jnp.maximum(b_last + d["m_prev"], jnp.max(logw_row, axis=1, keepdims=True))
        w_row = jnp.exp(logw_row - m_new)
        w_col = jnp.sum(jnp.where(eye_l, w_row, 0.0), axis=1, keepdims=True)
        decay = jnp.exp(b_last + d["m_prev"] - m_new)
        kw = (k_all[:, cols(i)] * w_col).astype(BF16)
        vaug = jnp.concatenate([d["vb"], jnp.ones((lc, LANE), BF16)], axis=1)
        upd = lax.dot_general(kw, vaug, tn, preferred_element_type=F32)
        caug[i] = decay * caug[i] + upd
        m_s[i] = m_new

    @pl.when(c == nc - 1)
    def _fin():
        for i in heads:
            cout_ref[i] = caug[i, :, 0:e]
            nout_ref[i] = jnp.sum(jnp.where(eye_e, caug[i, :, e:e + 1], 0.0), axis=0, keepdims=True)
            mout_ref[i] = jnp.broadcast_to(m_s[i], (1, LANE))


def _mlstm(xp, gates_t, hist, conv_w, conv_b, c0, n0, m0, gain, b_gate, *, batch, seq, lb, lc, hb, out_dtype):
    nc = seq // lb
    e = E_MH
    w = hb * e
    ng = N_MH // hb
    assert N_MH % hb == 0
    rb = lambda b, h, c, bg: b * nc + c
    qcol, kcol, vcol, ocol = COL_QM // w, COL_KM // w, COL_VM // w, COL_OM // w
    body = functools.partial(_mlstm_body, lb=lb, lc=lc, t_valid=seq if lb == lc else lb, hb=hb)
    grid_spec = pltpu.PrefetchScalarGridSpec(
        num_scalar_prefetch=1,
        grid=(batch, ng, nc),
        in_specs=[
            pl.BlockSpec((lb, w), lambda b, h, c, bg: (rb(b, h, c, bg), qcol + h)),
            pl.BlockSpec((lb, w), lambda b, h, c, bg: (rb(b, h, c, bg), kcol + h)),
            pl.BlockSpec((lb, w), lambda b, h, c, bg: (rb(b, h, c, bg), vcol + h)),
            pl.BlockSpec((lb, w), lambda b, h, c, bg: (rb(b, h, c, bg), ocol + h)),
            pl.BlockSpec((None, hb, 2, lc), lambda b, h, c, bg: (b, h, 0, c)),
            pl.BlockSpec((None, CONV_W - 1, w), lambda b, h, c, bg: (b, 0, h)),
            pl.BlockSpec((None, CONV_W - 1, w), lambda b, h, c, bg: (b, 0, ng + h)),
            pl.BlockSpec((CONV_W, w), lambda b, h, c, bg: (0, h)),
            pl.BlockSpec((CONV_W, w), lambda b, h, c, bg: (0, ng + h)),
            pl.BlockSpec((1, w), lambda b, h, c, bg: (0, h)),
            pl.BlockSpec((1, w), lambda b, h, c, bg: (0, ng + h)),
            pl.BlockSpec((None, hb, e, e), lambda b, h, c, bg: (b, h, 0, 0)),
            pl.BlockSpec((None, hb, 1, e), lambda b, h, c, bg: (b, h, 0, 0)),
            pl.BlockSpec((None, hb, 1, 1), lambda b, h, c, bg: (b, h, 0, 0)),
            pl.BlockSpec((1, w), lambda b, h, c, bg: (0, h)),
        ],
        out_specs=[
            pl.BlockSpec((lb, w), lambda b, h, c, bg: (rb(b, h, c, bg), h)),
            pl.BlockSpec((None, hb, e, e), lambda b, h, c, bg: (b, h, 0, 0)),
            pl.BlockSpec((None, hb, 1, e), lambda b, h, c, bg: (b, h, 0, 0)),
            pl.BlockSpec((None, hb, 1, LANE), lambda b, h, c, bg: (b, h, 0, 0)),
        ],
        scratch_shapes=[
            pltpu.VMEM((hb, e, e + LANE), F32),
            pltpu.VMEM((hb, 1, 1), F32),
            pltpu.VMEM((lc + 2 * SUBLANE, w), F32),
            pltpu.VMEM((lc + 2 * SUBLANE, w), F32),
            pltpu.VMEM((lc, w), F32),
        ],
    )
    return pl.pallas_call(
        body,
        grid_spec=grid_spec,
        out_shape=[
            jax.ShapeDtypeStruct((batch * seq, D_MLSTM), out_dtype),
            jax.ShapeDtypeStruct((batch, N_MH, e, e), F32),
            jax.ShapeDtypeStruct((batch, N_MH, 1, e), F32),
            jax.ShapeDtypeStruct((batch, N_MH, 1, LANE), F32),
        ],
        compiler_params=_cparams(("arbitrary", "arbitrary", "arbitrary")),
    )(b_gate, xp, xp, xp, xp, gates_t, hist, hist, conv_w, conv_w, conv_b, conv_b, c0, n0, m0, gain)


ATT_BLK = 2048
ATT_GROUP = 8


def _attn_body(sl_ref, q_ref, kc_ref, vc_ref, kp_ref, vp_ref, gain_ref, o_ref, o_s, m_s, l_s):
    blk = pl.program_id(1)
    h = pl.program_id(2)
    slope = sl_ref[h]
    scale = E_AH ** -0.5
    qi = lax.broadcasted_iota(jnp.int32, (BAND, BAND), 0)
    ci = lax.broadcasted_iota(jnp.int32, (BAND, BAND), 1)
    dist_prev = (BAND + qi - ci).astype(F32)
    dist_cur = (qi - ci).astype(F32)
    ok_prev = ci >= qi
    ok_cur = ci <= qi

    def run_units(cfg, d, specs):
        sd = slope * float(d)
        bm_cur = jnp.where(ok_cur, -sd * dist_cur, NEG_INF)
        nt = (((1,), (1,)), ((), ()))
        scores = []
        for prev_pen, q_sl, kp_src, kc_src, _, _ in specs:
            bm_prev = jnp.where(ok_prev, prev_pen - sd * dist_prev, NEG_INF)
            qv = q_ref[q_sl, :].astype(BF16)
            kp = kp_src[0][kp_src[1], :].astype(BF16)
            kc = kc_src[0][kc_src[1], :].astype(BF16)
            s_p = lax.dot_general(qv, kp, nt, preferred_element_type=F32) * scale + bm_prev
            s_c = lax.dot_general(qv, kc, nt, preferred_element_type=F32) * scale + bm_cur
            scores.append((s_p, s_c))
        probs = []
        for s_p, s_c in scores:
            m = jnp.max(jnp.maximum(s_p, s_c), axis=1, keepdims=True)
            p_p = jnp.exp(s_p - m)
            p_c = jnp.exp(s_c - m)
            l = jnp.sum(p_p + p_c, axis=1, keepdims=True)
            probs.append((m, l, p_p.astype(BF16), p_c.astype(BF16)))
        for (_, q_sl, _, _, vp_src, vc_src), (m, l, p_p, p_c) in zip(specs, probs):
            vp = vp_src[0][vp_src[1], :].astype(BF16)
            vc = vc_src[0][vc_src[1], :].astype(BF16)
            o = jnp.dot(p_p, vp, preferred_element_type=F32) + jnp.dot(p_c, vc, preferred_element_type=F32)
            o_s[cfg, q_sl, :] = o
            m_s[cfg, q_sl, :] = jnp.broadcast_to(m, (BAND, E_AH))
            l_s[cfg, q_sl, :] = jnp.broadcast_to(l, (BAND, E_AH))

    first_pen = jnp.where(blk == 0, NEG_INF, 0.0)

    for cfg, (win, d) in enumerate(DILATED_CONFIGS):
        span = BAND * d
        n_u = ATT_BLK // span

        def sl(r, u, _d=d, _span=span):
            if _d == 1:
                if isinstance(u, int):
                    return pl.ds(u * _span, BAND)
                return pl.ds(pl.multiple_of(u * _span, BAND), BAND)
            return pl.ds(r + u * _span, BAND, stride=_d)

        def head_unit(r, d=d, span=span, sl=sl):
            cur = sl(r, 0)
            prv = sl(r + ATT_BLK - span, 0) if d > 1 else pl.ds(ATT_BLK - span, BAND)
            return (first_pen, cur, (kp_ref, prv), (kc_ref, cur), (vp_ref, prv), (vc_ref, cur))

        def tail_unit(r, u, sl=sl):
            cur = sl(r, u)
            prv = sl(r, u - 1)
            return (0.0, cur, (kc_ref, prv), (kc_ref, cur), (vc_ref, prv), (vc_ref, cur))

        specs = [head_unit(r) if u == 0 else tail_unit(r, u) for u in range(n_u) for r in range(d)]
        for g in range(0, len(specs), ATT_GROUP):
            run_units(cfg, d, specs[g:g + ATT_GROUP])

    rows = 256

    def merge(i, _):
        rs = pl.ds(pl.multiple_of(i * rows, rows), rows)
        m0, m1, m2 = m_s[0, rs, :], m_s[1, rs, :], m_s[2, rs, :]
        m_all = jnp.maximum(jnp.maximum(m0, m1), m2)
        e0, e1, e2 = jnp.exp(m0 - m_all), jnp.exp(m1 - m_all), jnp.exp(m2 - m_all)
        num = e0 * o_s[0, rs, :] + e1 * o_s[1, rs, :] + e2 * o_s[2, rs, :]
        den = e0 * l_s[0, rs, :] + e1 * l_s[1, rs, :] + e2 * l_s[2, rs, :]
        ha = num / den
        ha = ha * lax.rsqrt(jnp.mean(ha * ha, axis=1, keepdims=True) + EPS)
        o_ref[rs, :] = (ha * gain_ref[...]).astype(o_ref.dtype)
        return 0

    lax.fori_loop(0, ATT_BLK // rows, merge, 0)


def _attn_prompt(xp, gain, slopes, *, batch, seq):
    nb = seq // ATT_BLK
    qc, kc, vc = COL_QA // E_AH, COL_KA // E_AH, COL_VA // E_AH
    cur = lambda b, i, h, s: b * nb + i
    prev = lambda b, i, h, s: b * nb + jnp.maximum(i - 1, 0)
    grid_spec = pltpu.PrefetchScalarGridSpec(
        num_scalar_prefetch=1,
        grid=(batch, nb, N_AH),
        in_specs=[
            pl.BlockSpec((ATT_BLK, E_AH), lambda b, i, h, s: (cur(b, i, h, s), qc + h)),
            pl.BlockSpec((ATT_BLK, E_AH), lambda b, i, h, s: (cur(b, i, h, s), kc + h)),
            pl.BlockSpec((ATT_BLK, E_AH), lambda b, i, h, s: (cur(b, i, h, s), vc + h)),
            pl.BlockSpec((ATT_BLK, E_AH), lambda b, i, h, s: (prev(b, i, h, s), kc + h)),
            pl.BlockSpec((ATT_BLK, E_AH), lambda b, i, h, s: (prev(b, i, h, s), vc + h)),
            pl.BlockSpec((1, E_AH), lambda b, i, h, s: (0, h)),
        ],
        out_specs=pl.BlockSpec((ATT_BLK, E_AH), lambda b, i, h, s: (cur(b, i, h, s), h)),
        scratch_shapes=[pltpu.VMEM((3, ATT_BLK, E_AH), F32)] * 3,
    )
    return pl.pallas_call(
        _attn_body,
        grid_spec=grid_spec,
        out_shape=jax.ShapeDtypeStruct((batch * seq, N_AH * E_AH), BF16),
        compiler_params=_cparams(("arbitrary", "arbitrary", "arbitrary")),
    )(slopes, xp, xp, xp, xp, xp, gain)


DEC_NEAR = 8


def _near_multiplicity():
    return [sum(1 for win, d in DILATED_CONFIGS if dist % d == 0 and dist <= win) for dist in range(DEC_NEAR)]


def _decode_body(q_ref, kn_ref, vn_ref, kc_ref, vc_ref, slope_ref, gain_ref,
                 ha_ref, ko_hbm, vo_hbm, ktail, vtail, sem, *, w_buf, s_new):
    b = pl.program_id(0)
    keep = w_buf - s_new
    copies = [
        pltpu.make_async_copy(kc_ref.at[0, 0, pl.ds(s_new, keep)], ko_hbm.at[0, b, pl.ds(0, keep)], sem.at[0]),
        pltpu.make_async_copy(vc_ref.at[0, 0, pl.ds(s_new, keep)], vo_hbm.at[0, b, pl.ds(0, keep)], sem.at[1]),
        pltpu.make_async_copy(kn_ref.at[0], ko_hbm.at[0, b, pl.ds(keep, s_new)], sem.at[2]),
        pltpu.make_async_copy(vn_ref.at[0], vo_hbm.at[0, b, pl.ds(keep, s_new)], sem.at[3]),
    ]
    for cp in copies:
        cp.start()

    ktail[pl.ds(0, DEC_NEAR)] = kc_ref[0, 0, pl.ds(w_buf - DEC_NEAR, DEC_NEAR)]
    vtail[pl.ds(0, DEC_NEAR)] = vc_ref[0, 0, pl.ds(w_buf - DEC_NEAR, DEC_NEAR)]
    ktail[pl.ds(DEC_NEAR, s_new)] = kn_ref[0]
    vtail[pl.ds(DEC_NEAR, s_new)] = vn_ref[0]

    scale = E_AH ** -0.5
    slope = slope_ref[:, 0:1]
    near_i = lax.broadcasted_iota(jnp.int32, (DEC_NEAR, 1, 1), 0)
    near_dist = (DEC_NEAR - 1 - near_i).astype(F32)
    near_mult = jnp.zeros((DEC_NEAR, 1, 1), F32)
    for dist, c in enumerate(_near_multiplicity()):
        near_mult = jnp.where(near_i == DEC_NEAR - 1 - dist, float(c), near_mult)

    def far_part(s, win, d):
        n = N_BACK - (DEC_NEAR - 1) // d
        sl = pl.ds(w_buf + s - win, n, stride=d) if d > 1 else pl.ds(w_buf + s - win, n)
        i = lax.broadcasted_iota(jnp.int32, (n, 1, 1), 0)
        return sl, ((N_BACK - i) * d).astype(F32)

    def one_query(s, _):
        q = q_ref[0, s]
        parts = []
        for win, d in DILATED_CONFIGS:
            sl, dist = far_part(s, win, d)
            sc = jnp.sum(kc_ref[0, 0, sl] * q, axis=-1, keepdims=True) * scale - slope * dist
            parts.append((sc, None, lambda sl=sl: vc_ref[0, 0, sl]))
        nsl = pl.ds(s + 1, DEC_NEAR)
        sc = jnp.sum(ktail[nsl] * q, axis=-1, keepdims=True) * scale - slope * near_dist
        parts.append((sc, near_mult, lambda: vtail[nsl]))
        m = functools.reduce(jnp.maximum, [jnp.max(p[0], axis=0, keepdims=True) for p in parts])
        den = jnp.zeros((1, N_AH, 1), F32)
        o = jnp.zeros((1, N_AH, E_AH), F32)
        for sc, mu, load_v in parts:
            p = jnp.exp(sc - m)
            if mu is not None:
                p = p * mu
            den = den + jnp.sum(p, axis=0, keepdims=True)
            o = o + jnp.sum(p * load_v(), axis=0, keepdims=True)
        o = (o / den)[0]
        o = o * lax.rsqrt(jnp.mean(o * o, axis=-1, keepdims=True) + EPS)
        ha_ref[0, s] = o * gain_ref[...]
        return 0

    lax.fori_loop(0, s_new, one_query, 0)
    for cp in copies:
        cp.wait()


def _attn_decode(q3, kn3, vn3, cache_k, cache_v, gain, slopes, *, batch, s_new):
    w_buf = cache_k.shape[2]
    assert w_buf >= max(w for w, _ in DILATED_CONFIGS) and s_new <= DEC_NEAR
    assert all(w // d == N_BACK and d & (d - 1) == 0 for w, d in DILATED_CONFIGS)
    body = functools.partial(_decode_body, w_buf=w_buf, s_new=s_new)
    new_spec = pl.BlockSpec((1, s_new, N_AH, E_AH), lambda b: (b, 0, 0, 0))
    cache_spec = pl.BlockSpec((1, 1, w_buf, N_AH, E_AH), lambda b: (0, b, 0, 0, 0))
    tile_spec = pl.BlockSpec((N_AH, E_AH), lambda b: (0, 0))
    return pl.pallas_call(
        body,
        grid=(batch,),
        in_specs=[new_spec, new_spec, new_spec, cache_spec, cache_spec, tile_spec, tile_spec],
        out_specs=[new_spec, pl.BlockSpec(memory_space=pl.ANY), pl.BlockSpec(memory_space=pl.ANY)],
        out_shape=[
            jax.ShapeDtypeStruct((batch, s_new, N_AH, E_AH), F32),
            jax.ShapeDtypeStruct(cache_k.shape, F32),
            jax.ShapeDtypeStruct(cache_v.shape, F32),
        ],
        scratch_shapes=[pltpu.VMEM((DEC_NEAR + s_new, N_AH, E_AH), F32)] * 2 + [pltpu.SemaphoreType.DMA((4,))],
        compiler_params=_cparams(("arbitrary",)),
    )(q3, kn3, vn3, cache_k, cache_v, slopes, gain)


OP_TM = 512
ROUTE_W = LANE
PACK_ROWS = D_MODEL // LANE


def _layer_norm(z, g, b):
    mu = jnp.mean(z, axis=1, keepdims=True)
    zc = z - mu
    var = jnp.mean(zc * zc, axis=1, keepdims=True)
    return zc * lax.rsqrt(var + EPS) * g + b


def _outproj_body(hm_ref, ha_ref, x_ref, wm_ref, wa_ref, g_ref, b_ref, wrh_ref, wrl_ref, br_ref,
                  x1_ref, x1p_ref, route_ref):
    y = (jnp.dot(hm_ref[...].astype(BF16), wm_ref[...], preferred_element_type=F32)
         + jnp.dot(ha_ref[...].astype(BF16), wa_ref[...], preferred_element_type=F32))
    x1 = _layer_norm(ALPHA * x_ref[...] + y, g_ref[...], b_ref[...])
    x1_ref[...] = x1
    tm = x1.shape[0]
    for j in range(PACK_ROWS):
        x1p_ref[pl.ds(j, tm, stride=PACK_ROWS), :] = x1[:, j * LANE:(j + 1) * LANE]
    x1_hi = x1.astype(BF16)
    x1_lo = (x1 - x1_hi.astype(F32)).astype(BF16)
    logits = (jnp.dot(x1_hi, wrh_ref[...], preferred_element_type=F32)
              + jnp.dot(x1_lo, wrh_ref[...], preferred_element_type=F32)
              + jnp.dot(x1_hi, wrl_ref[...], preferred_element_type=F32)) + br_ref[...]
    lane = lax.broadcasted_iota(jnp.int32, (tm, ROUTE_W), 1)
    lane_f = lane.astype(F32)
    big = float(ROUTE_W)
    gl = jnp.where(lane < N_GROUPS, logits, NEG_INF)
    gmax = jnp.max(gl, axis=1, keepdims=True)
    g_w = 1.0 / jnp.sum(jnp.exp(gl - gmax), axis=1, keepdims=True)
    g_idx = jnp.min(jnp.where(gl == gmax, lane_f, big), axis=1, keepdims=True)
    lo = N_GROUPS + E_PER_GROUP * g_idx
    el = jnp.where(jnp.logical_and(lane_f >= lo, lane_f < lo + E_PER_GROUP), logits, NEG_INF)
    v1 = jnp.max(el, axis=1, keepdims=True)
    i1 = jnp.min(jnp.where(el == v1, lane_f, big), axis=1, keepdims=True)
    el2 = jnp.where(lane_f == i1, NEG_INF, el)
    v2 = jnp.max(el2, axis=1, keepdims=True)
    i2 = jnp.min(jnp.where(el2 == v2, lane_f, big), axis=1, keepdims=True)
    e2 = jnp.exp(v2 - v1)
    w1 = g_w / (1.0 + e2)
    w2 = g_w * e2 / (1.0 + e2)
    route = jnp.where(lane == 0, i1 - N_GROUPS,
                      jnp.where(lane == 1, i2 - N_GROUPS,
                                jnp.where(lane == 2, w1, jnp.where(lane == 3, w2, 0.0))))
    route_ref[...] = route


def _outproj(hm, ha, x, wo, g, b, wr_hi, wr_lo, br):
    n = x.shape[0]
    tm = min(OP_TM, n)
    assert n % tm == 0
    row = lambda i: (i, 0)
    fixed = lambda i: (0, 0)
    once = pl.Buffered(1)
    return pl.pallas_call(
        _outproj_body,
        grid=(n // tm,),
        in_specs=[
            pl.BlockSpec((tm, D_MLSTM), row),
            pl.BlockSpec((tm, N_AH * E_AH), row),
            pl.BlockSpec((tm, D_MODEL), row),
            pl.BlockSpec((D_MLSTM, D_MODEL), fixed, pipeline_mode=once),
            pl.BlockSpec((N_AH * E_AH, D_MODEL), lambda i: (1, 0), pipeline_mode=once),
            pl.BlockSpec((1, D_MODEL), fixed),
            pl.BlockSpec((1, D_MODEL), fixed),
            pl.BlockSpec((D_MODEL, ROUTE_W), fixed),
            pl.BlockSpec((D_MODEL, ROUTE_W), fixed),
            pl.BlockSpec((1, ROUTE_W), fixed),
        ],
        out_specs=[pl.BlockSpec((tm, D_MODEL), row), pl.BlockSpec((tm * PACK_ROWS, LANE), row),
                   pl.BlockSpec((tm, ROUTE_W), row)],
        out_shape=[jax.ShapeDtypeStruct((n, D_MODEL), F32), jax.ShapeDtypeStruct((n * PACK_ROWS, LANE), F32),
                   jax.ShapeDtypeStruct((n, ROUTE_W), F32)],
        compiler_params=_cparams(("arbitrary",)),
    )(hm, ha, x, wo, wo, g, b, wr_hi, wr_lo, br)


MOE_R = 128
MOE_MAXB = 6
MOE_FC = 256
MOE_NC = D_FF // MOE_FC
MOE_ISSUE_UNROLL = 8


def _moe_body(ex_ref, row0_ref, nb_ref, src_ref, x1p_hbm, wg_ref, wu_ref, wd_ref, ys_hbm,
              ubuf, xbuf, acc, sem_in, sem_out):
    t = pl.program_id(0)
    c = pl.program_id(1)
    n_items = pl.num_programs(0)
    slot = t % 2
    nb = nb_ref[t]

    def rows(base, r):
        return pl.ds(pl.multiple_of(base + r * MOE_R, MOE_R), MOE_R)

    def tile_rows(i):
        return pl.ds(pl.multiple_of(i * PACK_ROWS, PACK_ROWS), PACK_ROWS)

    def out_copy(tt, sl, r):
        return pltpu.make_async_copy(acc.at[sl, rows(0, r), :], ys_hbm.at[rows(row0_ref[tt], r), :], sem_out.at[sl])

    def for_blocks(n, fn):
        def body(r, _):
            fn(r)
            return 0

        lax.fori_loop(0, n, body, 0)

    def n_in(tt):
        return jnp.maximum(nb_ref[tt], 0)

    def n_out(tt):
        return jnp.abs(nb_ref[tt])

    def start_gather(tt, sl):
        base = row0_ref[tt]

        def group(gi):
            for k in range(MOE_ISSUE_UNROLL):
                i = gi * MOE_ISSUE_UNROLL + k
                tok = src_ref[base + i]
                pltpu.make_async_copy(x1p_hbm.at[tile_rows(tok), :], ubuf.at[sl, tile_rows(i), :], sem_in.at[sl]).start()

        for_blocks(n_in(tt) * (MOE_R // MOE_ISSUE_UNROLL), group)

    def wait_gather(tt, sl):
        def block(r):
            span = pl.ds(pl.multiple_of(r * (MOE_R * PACK_ROWS), MOE_R * PACK_ROWS), MOE_R * PACK_ROWS)
            pltpu.make_async_copy(x1p_hbm.at[pl.ds(0, MOE_R * PACK_ROWS), :], ubuf.at[sl, span, :], sem_in.at[sl]).wait()

        for_blocks(n_in(tt), block)

    def unpack_block(r):
        for j in range(PACK_ROWS):
            start = r * (MOE_R * PACK_ROWS) + j
            u = ubuf[slot, pl.ds(start, MOE_R, stride=PACK_ROWS), :]
            xbuf[rows(0, r), j * LANE:(j + 1) * LANE] = u.astype(BF16)

    @pl.when(c == 0)
    def _begin():
        @pl.when(t == 0)
        def _():
            start_gather(0, 0)

        @pl.when(t + 1 < n_items)
        def _():
            start_gather(t + 1, 1 - slot)

        wait_gather(t, slot)
        for_blocks(n_in(t), unpack_block)

        @pl.when(t >= 2)
        def _():
            for_blocks(n_out(t - 2), lambda r: out_copy(t - 2, slot, r).wait())

        @pl.when(nb < 0)
        def _():
            acc[slot] = jnp.zeros(acc.shape[1:], F32)

    wg = wg_ref[...].astype(BF16)
    wu = wu_ref[...].astype(BF16)
    wd = wd_ref[...].astype(BF16)

    def piece(off, size):
        rs = pl.ds(pl.multiple_of(off, MOE_R), size)
        x = xbuf[rs, :]
        gt = jnp.dot(x, wg, preferred_element_type=F32)
        up = jnp.dot(x, wu, preferred_element_type=F32)
        hid = (gt * jax.nn.sigmoid(gt) * up).astype(BF16)
        y = jnp.dot(hid, wd, preferred_element_type=F32)

        @pl.when(c == 0)
        def _():
            acc[slot, rs, :] = y

        @pl.when(c > 0)
        def _():
            acc[slot, rs, :] = acc[slot, rs, :] + y

    nbp = jnp.maximum(nb, 0)
    n4 = nbp // 4
    rem = nbp - 4 * n4
    for_blocks(n4, lambda i: piece(i * (4 * MOE_R), 4 * MOE_R))

    @pl.when(rem >= 2)
    def _():
        piece(n4 * (4 * MOE_R), 2 * MOE_R)

    @pl.when(rem % 2 == 1)
    def _():
        piece(n4 * (4 * MOE_R) + (rem // 2) * (2 * MOE_R), MOE_R)

    @pl.when(c == MOE_NC - 1)
    def _end():
        for_blocks(n_out(t), lambda r: out_copy(t, slot, r).start())

        @pl.when(t == n_items - 1)
        def _():
            for_blocks(n_out(t), lambda r: out_copy(t, slot, r).wait())

            @pl.when(t >= 1)
            def _():
                for_blocks(n_out(t - 1), lambda r: out_copy(t - 1, 1 - slot, r).wait())


def _moe(item_e, item_row0, item_nb, src, x1p, w_gate, w_up, w_down, n_items, n_rows):
    def chunk(t, c, n):
        return jnp.where(n[t] > 0, c, MOE_NC - 1)

    grid_spec = pltpu.PrefetchScalarGridSpec(
        num_scalar_prefetch=4,
        grid=(n_items, MOE_NC),
        in_specs=[
            pl.BlockSpec(memory_space=pl.ANY),
            pl.BlockSpec((None, D_MODEL, MOE_FC), lambda t, c, e, r, n, s: (e[t], 0, chunk(t, c, n))),
            pl.BlockSpec((None, D_MODEL, MOE_FC), lambda t, c, e, r, n, s: (e[t], 0, chunk(t, c, n))),
            pl.BlockSpec((None, MOE_FC, D_MODEL), lambda t, c, e, r, n, s: (e[t], chunk(t, c, n), 0)),
        ],
        out_specs=pl.BlockSpec(memory_space=pl.ANY),
        scratch_shapes=[
            pltpu.VMEM((2, MOE_MAXB * MOE_R * PACK_ROWS, LANE), F32),
            pltpu.VMEM((MOE_MAXB * MOE_R, D_MODEL), BF16),
            pltpu.VMEM((2, MOE_MAXB * MOE_R, D_MODEL), F32),
            pltpu.SemaphoreType.DMA((2,)),
            pltpu.SemaphoreType.DMA((2,)),
        ],
    )
    return pl.pallas_call(
        _moe_body,
        grid_spec=grid_spec,
        out_shape=jax.ShapeDtypeStruct((n_rows, D_MODEL), F32),
        compiler_params=_cparams(("arbitrary", "arbitrary")),
    )(item_e, item_row0, item_nb, src, x1p, w_gate, w_up, w_down)


COMB_TM = 128
COMB_ISSUE_UNROLL = 4


def _comb_issue(pos_ref, ys_hbm, buf, sem, tile, slot, tm):
    def body(g, _):
        for u in range(COMB_ISSUE_UNROLL):
            r = g * COMB_ISSUE_UNROLL + u
            for k in range(2):
                p = pos_ref[2 * (tile * tm + r) + k]
                pltpu.make_async_copy(ys_hbm.at[pl.ds(p, 1), :], buf.at[slot, k, pl.ds(r, 1), :], sem.at[slot]).start()
        return 0

    lax.fori_loop(0, tm // COMB_ISSUE_UNROLL, body, 0)


def _combine_body(pos_ref, ys_hbm, x1_ref, route_ref, g_ref, b_ref, o_ref, buf, sem, *, tm):
    i = pl.program_id(0)
    n = pl.num_programs(0)
    slot = i % 2

    @pl.when(i == 0)
    def _():
        _comb_issue(pos_ref, ys_hbm, buf, sem, 0, 0, tm)

    @pl.when(i + 1 < n)
    def _():
        _comb_issue(pos_ref, ys_hbm, buf, sem, i + 1, 1 - slot, tm)

    for k in range(2):
        pltpu.make_async_copy(ys_hbm.at[pl.ds(0, tm), :], buf.at[slot, k], sem.at[slot]).wait()
    f = buf[slot, 0] * route_ref[:, 2:3] + buf[slot, 1] * route_ref[:, 3:4]
    o_ref[...] = _layer_norm(ALPHA * x1_ref[...] + f, g_ref[...], b_ref[...])


def _combine(pos, ys, x1, route, g, b):
    n, d = x1.shape
    tm = COMB_TM
    grid_spec = pltpu.PrefetchScalarGridSpec(
        num_scalar_prefetch=1,
        grid=(n // tm,),
        in_specs=[
            pl.BlockSpec(memory_space=pl.ANY),
            pl.BlockSpec((tm, d), lambda i, p: (i, 0)),
            pl.BlockSpec((tm, ROUTE_W), lambda i, p: (i, 0)),
            pl.BlockSpec((1, d), lambda i, p: (0, 0)),
            pl.BlockSpec((1, d), lambda i, p: (0, 0)),
        ],
        out_specs=pl.BlockSpec((tm, d), lambda i, p: (i, 0)),
        scratch_shapes=[pltpu.VMEM((2, 2, tm, d), F32), pltpu.SemaphoreType.DMA((2,))],
    )
    return pl.pallas_call(
        functools.partial(_combine_body, tm=tm),
        grid_spec=grid_spec,
        out_shape=jax.ShapeDtypeStruct((n, d), F32),
        compiler_params=_cparams(("arbitrary",)),
    )(pos, ys, x1, route, g, b)


def _dispatch_plan(eid, n_items, n_rows):
    p_total = eid.shape[0]
    blk = 128
    assert p_total % blk == 0
    onehot = (eid[:, None] == jnp.arange(N_EXPERTS, dtype=jnp.int32)[None, :]).astype(F32)
    counts = jnp.sum(onehot, axis=0).astype(jnp.int32)
    ohb = onehot.reshape(p_total // blk, blk, N_EXPERTS)
    earlier = (jnp.arange(blk)[:, None] > jnp.arange(blk)[None, :]).astype(F32)
    within = jnp.einsum("ij,bjk->bik", earlier, ohb, precision=lax.Precision.HIGHEST)
    blk_tot = jnp.sum(ohb, axis=1)
    blk_off = jnp.cumsum(blk_tot, axis=0) - blk_tot
    rank = jnp.sum((within + blk_off[:, None, :]) * ohb, axis=2).reshape(p_total).astype(jnp.int32)
    nblk = (counts + MOE_R - 1) // MOE_R
    seg_start = (jnp.cumsum(nblk) - nblk) * MOE_R
    pos = seg_start[eid] + rank
    src = jnp.zeros((n_rows,), jnp.int32).at[pos].set(jnp.arange(p_total, dtype=jnp.int32) // 2)
    items_per_e = (nblk + MOE_MAXB - 1) // MOE_MAXB
    item_end = jnp.cumsum(items_per_e)
    item_start = item_end - items_per_e
    t = jnp.arange(n_items, dtype=jnp.int32)
    e_t = jnp.minimum(jnp.sum((item_end[None, :] <= t[:, None]).astype(jnp.int32), axis=1), N_EXPERTS - 1)
    live = t < item_end[-1]
    local = t - item_start[e_t]
    used = jnp.sum(nblk)
    idle0 = used + (t - item_end[-1]) * MOE_MAXB
    nz_t = jnp.clip(n_rows // MOE_R - idle0, 0, MOE_MAXB)
    nb_t = jnp.where(live, jnp.clip(nblk[e_t] - local * MOE_MAXB, 0, MOE_MAXB), -nz_t)
    row0_t = jnp.where(live, seg_start[e_t] + local * (MOE_MAXB * MOE_R),
                       jnp.minimum(idle0, n_rows // MOE_R - 1) * MOE_R)
    last_e = e_t[jnp.maximum(item_end[-1] - 1, 0)]
    e_t = jnp.where(live, e_t, last_e)
    return pos.astype(jnp.int32), src, e_t.astype(jnp.int32), row0_t.astype(jnp.int32), nb_t.astype(jnp.int32)


def _alibi_slopes():
    return jnp.asarray([2.0 ** (-8.0 * (h + 1) / N_AH) for h in range(N_AH)], dtype=F32)


def kernel(x_prompt, x_sample, state_conv, state_mlstm_C, state_mlstm_n, state_mlstm_m, cache_win_k, cache_win_v, w_in, b_gate, conv_w, conv_b, mh_gain, att_gain, w_out, ln1_g, ln1_b, w_group, b_group, w_router, b_router, w_gate, w_up, w_down, ln2_g, ln2_b):
    bp, tp, d = x_prompt.shape
    bs, ts, _ = x_sample.shape
    assert d == D_MODEL and w_in.shape[0] == 1 and tp % ATT_BLK == 0 and ts >= CONV_W - 1
    w_buf = cache_win_k.shape[2]
    n_p, n_s = bp * tp, bs * ts
    slopes = _alibi_slopes()

    wi = w_in[0]
    g0 = 4 * D_MLSTM
    g1 = g0 + 2 * N_MH
    w_pack = jnp.concatenate(
        [wi[:, :g0], wi[:, g1:], wi[:, g0:g1], jnp.zeros((d, LANE - 2 * N_MH), F32)], axis=1).astype(BF16)

    xp2 = x_prompt.reshape(n_p, d)
    xs2 = x_sample.reshape(n_s, d)
    proj_p = _in_proj(xp2, w_pack, 512)
    proj_s = _in_proj(xs2, w_pack, n_s)

    cw = conv_w[0]
    cb = conv_b[0][None, :]
    mh_g = mh_gain[0][None, :]
    att_g = att_gain[0][None, :]
    bg = b_gate[0]

    def gates_time_major(proj, batch, seq, pad_to):
        gt = proj[:, COL_G:COL_G + 2 * N_MH].reshape(batch, seq, 2, N_MH).transpose(0, 3, 2, 1)
        if pad_to > seq:
            gt = jnp.pad(gt, ((0, 0), (0, 0), (0, 0), (0, pad_to - seq)))
        return gt

    lc_p = 256
    hm_p, c_p, n_pp, m_p = _mlstm(
        proj_p, gates_time_major(proj_p, bp, tp, tp), jnp.zeros((bp, CONV_W - 1, 2 * D_MLSTM), F32), cw, cb,
        jnp.zeros((bp, N_MH, E_MH, E_MH), F32), jnp.zeros((bp, N_MH, 1, E_MH), F32),
        jnp.zeros((bp, N_MH, 1, 1), F32), mh_g, bg, batch=bp, seq=tp, lb=lc_p, lc=lc_p, hb=2, out_dtype=BF16)
    ha_p = _attn_prompt(proj_p, att_g, slopes, batch=bp, seq=tp)

    lc_s = 16
    hm_s, c_s, n_ss, m_s = _mlstm(
        proj_s, gates_time_major(proj_s, bs, ts, lc_s), state_conv[0], cw, cb,
        state_mlstm_C[0], state_mlstm_n[0][:, :, None, :], state_mlstm_m[0][:, :, None, None],
        mh_g, bg, batch=bs, seq=ts, lb=ts, lc=lc_s, hb=N_MH, out_dtype=F32)
    new_rows = lambda col: proj_s[:, col:col + N_AH * E_AH].reshape(bs, ts, N_AH, E_AH)
    ha_s, wk_s, wv_s = _attn_decode(
        new_rows(COL_QA), new_rows(COL_KA), new_rows(COL_VA), cache_win_k, cache_win_v,
        att_gain[0].reshape(N_AH, E_AH), jnp.broadcast_to(slopes[:, None], (N_AH, E_AH)), batch=bs, s_new=ts)
    ha_s = ha_s.reshape(n_s, N_AH * E_AH)

    n_all = n_p + n_s
    wo = w_out[0].astype(BF16)
    w_r = jnp.concatenate(
        [w_group[0], w_router[0].transpose(1, 0, 2).reshape(d, N_EXPERTS),
         jnp.zeros((d, ROUTE_W - N_GROUPS - N_EXPERTS), F32)], axis=1)
    b_r = jnp.concatenate(
        [b_group[0], b_router[0].reshape(N_EXPERTS), jnp.zeros((ROUTE_W - N_GROUPS - N_EXPERTS,), F32)])[None, :]
    wr_hi = w_r.astype(BF16)
    wr_lo = (w_r - wr_hi.astype(F32)).astype(BF16)
    ln1 = (ln1_g[0][None, :], ln1_b[0][None, :])
    x1_p, x1p_p, route_p = _outproj(hm_p, ha_p, xp2, wo, *ln1, wr_hi, wr_lo, b_r)
    x1_s, x1p_s, route_s = _outproj(hm_s, ha_s, xs2, wo, *ln1, wr_hi, wr_lo, b_r)

    p_total = 2 * n_all
    n_rows = ((p_total + N_EXPERTS * (MOE_R - 1)) // MOE_R + 1) * MOE_R
    n_items = N_EXPERTS + n_rows // (MOE_R * MOE_MAXB)
    eid = jnp.concatenate([route_p[:, 0:2], route_s[:, 0:2]], axis=0).astype(jnp.int32).reshape(p_total)
    pos, src, item_e, item_row0, item_nb = _dispatch_plan(eid, n_items, n_rows)
    x1p = jnp.concatenate([x1p_p, x1p_s], axis=0)
    ys = _moe(item_e, item_row0, item_nb, src, x1p, w_gate[0], w_up[0], w_down[0], n_items, n_rows)
    ln2 = (ln2_g[0][None, :], ln2_b[0][None, :])
    y_p = _combine(pos[:2 * n_p], ys, x1_p, route_p, *ln2).reshape(bp, tp, d)
    y_s = _combine(pos[2 * n_p:], ys, x1_s, route_s, *ln2).reshape(bs, ts, d)

    def tail_rows(proj, batch, seq, col, width, rows):
        return proj.reshape(batch, seq, N_PROJ)[:, seq - rows:, col:col + width]

    win = min(w_buf, tp)
    p_conv = tail_rows(proj_p, bp, tp, COL_QM, 2 * D_MLSTM, CONV_W - 1)[None]
    p_wk = tail_rows(proj_p, bp, tp, COL_KA, N_AH * E_AH, win).reshape(1, bp, win, N_AH, E_AH)
    p_wv = tail_rows(proj_p, bp, tp, COL_VA, N_AH * E_AH, win).reshape(1, bp, win, N_AH, E_AH)
    s_conv = tail_rows(proj_s, bs, ts, COL_QM, 2 * D_MLSTM, CONV_W - 1)[None]
    return (y_p, y_s,
            p_conv, c_p[None], n_pp[:, :, 0, :][None], m_p[:, :, 0, 0][None], p_wk, p_wv,
            s_conv, c_s[None], n_ss[:, :, 0, :][None], m_s[:, :, 0, 0][None], wk_s, wv_s)
```

```python
import functools
import math

import jax
import jax.numpy as jnp
from jax import lax
from jax.experimental import pallas as pl
from jax.experimental.pallas import tpu as pltpu

F32 = jnp.float32
BF16 = jnp.bfloat16
NEG_INF = float("-inf")

D_MODEL = 2048
D_MLSTM = 1024
N_MH = 4
E_MH = 256
N_AH = 8
E_AH = 128
DILATED_CONFIGS = ((128, 1), (512, 4), (2048, 16))
N_BACK = 128
BAND = 128
CONV_W = 4
N_GROUPS = 4
E_PER_GROUP = 8
N_EXPERTS = 32
D_FF = 1024
EPS = 1e-5
ALPHA = 2.0 ** 0.25

LANE = 128
SUBLANE = 8

COL_QM, COL_KM, COL_VM, COL_OM = 0, 1024, 2048, 3072
COL_QA, COL_KA, COL_VA, COL_G = 4096, 5120, 6144, 7168
N_PROJ = 7296
PROJ_TN = 2432

VMEM_LIMIT = 56 * 1024 * 1024


def _cparams(sem):
    return pltpu.CompilerParams(dimension_semantics=sem, vmem_limit_bytes=VMEM_LIMIT)


def _proj_body(x_ref, w_ref, o_ref):
    o_ref[...] = jnp.dot(x_ref[...].astype(BF16), w_ref[...], preferred_element_type=F32)


def _in_proj(x, w, tm):
    n = x.shape[0]
    return pl.pallas_call(
        _proj_body,
        grid=(N_PROJ // PROJ_TN, n // tm),
        in_specs=[pl.BlockSpec((tm, D_MODEL), lambda j, i: (i, 0)),
                  pl.BlockSpec((D_MODEL, PROJ_TN), lambda j, i: (0, j))],
        out_specs=pl.BlockSpec((tm, PROJ_TN), lambda j, i: (i, j)),
        out_shape=jax.ShapeDtypeStruct((n, N_PROJ), F32),
        compiler_params=_cparams(("arbitrary", "arbitrary")),
    )(x, w)


def _mlstm_body(bg_ref, xq_ref, xk_ref, v_ref, om_ref, g_ref, hq_ref, hk_ref, cwq_ref, cwk_ref,
                cbq_ref, cbk_ref, c0_ref, n0_ref, m0_ref, gain_ref,
                hm_ref, cout_ref, nout_ref, mout_ref,
                caug, m_s, uq, uk, vbuf, *, lb, lc, t_valid, hb):
    hg = pl.program_id(1)
    c = pl.program_id(2)
    nc = pl.num_programs(2)
    e = E_MH
    heads = range(hb)
    cols = lambda i: slice(i * e, (i + 1) * e)

    row_e = lax.broadcasted_iota(jnp.int32, (e, e), 0)
    col_e = lax.broadcasted_iota(jnp.int32, (e, e), 1)
    eye_e = row_e == col_e

    @pl.when(c == 0)
    def _init():
        for i in heads:
            caug[i, :, 0:e] = c0_ref[i]
            ncol = jnp.sum(jnp.where(eye_e, n0_ref[i], 0.0), axis=1, keepdims=True)
            caug[i, :, e:e + LANE] = jnp.broadcast_to(ncol, (e, LANE))
            m_s[i] = m0_ref[i]
        uq[...] = jnp.zeros(uq.shape, F32)
        uk[...] = jnp.zeros(uk.shape, F32)
        uq[pl.ds(SUBLANE - (CONV_W - 1), CONV_W - 1), :] = hq_ref[...]
        uk[pl.ds(SUBLANE - (CONV_W - 1), CONV_W - 1), :] = hk_ref[...]
        if lb != lc:
            vbuf[...] = jnp.zeros(vbuf.shape, F32)

    uq[pl.ds(SUBLANE, lb), :] = xq_ref[...]
    uk[pl.ds(SUBLANE, lb), :] = xk_ref[...]

    def conv_silu(u, cw_ref, cb_ref):
        acc = cb_ref[...]
        for j in range(CONV_W):
            acc = acc + u[pl.ds(SUBLANE - (CONV_W - 1) + j, lc), :] * cw_ref[j:j + 1, :]
        return acc * jax.nn.sigmoid(acc)

    q_all = conv_silu(uq, cwq_ref, cbq_ref)
    k_all = conv_silu(uk, cwk_ref, cbk_ref) * (e ** -0.5)
    if lb != lc:
        vbuf[pl.ds(0, lb), :] = v_ref[...]
        v_all = vbuf[...]
    else:
        v_all = v_ref[...]

    tq = uq[pl.ds(lc + SUBLANE - (CONV_W - 1), CONV_W - 1), :]
    tk = uk[pl.ds(lc + SUBLANE - (CONV_W - 1), CONV_W - 1), :]
    uq[pl.ds(SUBLANE - (CONV_W - 1), CONV_W - 1), :] = tq
    uk[pl.ds(SUBLANE - (CONV_W - 1), CONV_W - 1), :] = tk

    lane_t = lax.broadcasted_iota(jnp.int32, (1, lc), 1)
    valid = (lane_t + c * lc) < t_valid
    row_l = lax.broadcasted_iota(jnp.int32, (lc, lc), 0)
    col_l = lax.broadcasted_iota(jnp.int32, (lc, lc), 1)
    causal = col_l <= row_l
    eye_l = col_l == row_l
    nt = (((1,), (1,)), ((), ()))
    tn = (((0,), (0,)), ((), ()))

    st = []
    for i in heads:
        h = hg * hb + i
        gi = g_ref[i, 0:1, :] + bg_ref[h]
        gf = g_ref[i, 1:2, :] + bg_ref[N_MH + h]
        lf = -(jnp.maximum(-gf, 0.0) + jnp.log1p(jnp.exp(-jnp.abs(gf))))
        ig_row = jnp.where(valid, gi, NEG_INF)
        lf_row = jnp.where(valid, lf, 0.0)
        b_col = jnp.sum(jnp.where(causal, lf_row, 0.0), axis=1, keepdims=True)
        b_row = jnp.sum(jnp.where(eye_l, b_col, 0.0), axis=0, keepdims=True)
        m_prev = m_s[i]
        dlog = jnp.where(causal, b_col - b_row + ig_row, NEG_INF)
        a_col = b_col + m_prev
        m_t = jnp.maximum(a_col, jnp.max(dlog, axis=1, keepdims=True))
        st.append(dict(ig_row=ig_row, b_row=b_row, m_prev=m_prev, m_t=m_t,
                       w_intra=jnp.exp(dlog - m_t), w_inter=jnp.exp(a_col - m_t),
                       qb=q_all[:, cols(i)].astype(BF16), kb=k_all[:, cols(i)].astype(BF16),
                       vb=v_all[:, cols(i)].astype(BF16)))

    for i in heads:
        d = st[i]
        d["s"] = lax.dot_general(d["qb"], d["kb"], nt, preferred_element_type=F32) * d["w_intra"]
        d["qc"] = jnp.dot(d["qb"], caug[i].astype(BF16), preferred_element_type=F32)

    for i in heads:
        d = st[i]
        num = jnp.dot(d["s"].astype(BF16), d["vb"], preferred_element_type=F32) + d["w_inter"] * d["qc"][:, 0:e]
        den = jnp.sum(d["s"], axis=1, keepdims=True) + d["w_inter"] * d["qc"][:, e:e + 1]
        hh = num / jnp.maximum(jnp.abs(den), jnp.exp(-d["m_t"]))
        hh = hh - jnp.mean(hh, axis=1, keepdims=True)
        hh = hh * lax.rsqrt(jnp.mean(hh * hh, axis=1, keepdims=True) + EPS)
        if lb != lc:
            hh = hh[0:lb, :]
        hm = hh * gain_ref[:, cols(i)] * jax.nn.sigmoid(om_ref[:, cols(i)])
        hm_ref[:, cols(i)] = hm.astype(hm_ref.dtype)

    for i in heads:
        d = st[i]
        b_last = d["b_row"][:, lc - 1:lc]
        logw_row = b_last - d["b_row"] + d["ig_row"]
        m_new = jnp.maximum(b_last + d["m_prev"], jnp.max(logw_row, axis=1, keepdims=True))
        w_row = jnp.exp(logw_row - m_new)
        w_col = jnp.sum(jnp.where(eye_l, w_row, 0.0), axis=1, keepdims=True)
        decay = jnp.exp(b_last + d["m_prev"] - m_new)
        kw = (k_all[:, cols(i)] * w_col).astype(BF16)
        vaug = jnp.concatenate([d["vb"], jnp.ones((lc, LANE), BF16)], axis=1)
        upd = lax.dot_general(kw, vaug, tn, preferred_element_type=F32)
        caug[i] = decay * caug[i] + upd
        m_s[i] = m_new

    @pl.when(c == nc - 1)
    def _fin():
        for i in heads:
            cout_ref[i] = caug[i, :, 0:e]
            nout_ref[i] = jnp.sum(jnp.where(eye_e, caug[i, :, e:e + 1], 0.0), axis=0, keepdims=True)
            mout_ref[i] = jnp.broadcast_to(m_s[i], (1, LANE))


def _mlstm(xp, gates_t, hist, conv_w, conv_b, c0, n0, m0, gain, b_gate, *, batch, seq, lb, lc, hb, out_dtype):
    nc = seq // lb
    e = E_MH
    w = hb * e
    ng = N_MH // hb
    assert N_MH % hb == 0
    rb = lambda b, h, c, bg: b * nc + c
    qcol, kcol, vcol, ocol = COL_QM // w, COL_KM // w, COL_VM // w, COL_OM // w
    body = functools.partial(_mlstm_body, lb=lb, lc=lc, t_valid=seq if lb == lc else lb, hb=hb)
    grid_spec = pltpu.PrefetchScalarGridSpec(
        num_scalar_prefetch=1,
        grid=(batch, ng, nc),
        in_specs=[
            pl.BlockSpec((lb, w), lambda b, h, c, bg: (rb(b, h, c, bg), qcol + h)),
            pl.BlockSpec((lb, w), lambda b, h, c, bg: (rb(b, h, c, bg), kcol + h)),
            pl.BlockSpec((lb, w), lambda b, h, c, bg: (rb(b, h, c, bg), vcol + h)),
            pl.BlockSpec((lb, w), lambda b, h, c, bg: (rb(b, h, c, bg), ocol + h)),
            pl.BlockSpec((None, hb, 2, lc), lambda b, h, c, bg: (b, h, 0, c)),
            pl.BlockSpec((None, CONV_W - 1, w), lambda b, h, c, bg: (b, 0, h)),
            pl.BlockSpec((None, CONV_W - 1, w), lambda b, h, c, bg: (b, 0, ng + h)),
            pl.BlockSpec((CONV_W, w), lambda b, h, c, bg: (0, h)),
            pl.BlockSpec((CONV_W, w), lambda b, h, c, bg: (0, ng + h)),
            pl.BlockSpec((1, w), lambda b, h, c, bg: (0, h)),
            pl.BlockSpec((1, w), lambda b, h, c, bg: (0, ng + h)),
            pl.BlockSpec((None, hb, e, e), lambda b, h, c, bg: (b, h, 0, 0)),
            pl.BlockSpec((None, hb, 1, e), lambda b, h, c, bg: (b, h, 0, 0)),
            pl.BlockSpec((None, hb, 1, 1), lambda b, h, c, bg: (b, h, 0, 0)),
            pl.BlockSpec((1, w), lambda b, h, c, bg: (0, h)),
        ],
        out_specs=[
            pl.BlockSpec((lb, w), lambda b, h, c, bg: (rb(b, h, c, bg), h)),
            pl.BlockSpec((None, hb, e, e), lambda b, h, c, bg: (b, h, 0, 0)),
            pl.BlockSpec((None, hb, 1, e), lambda b, h, c, bg: (b, h, 0, 0)),
            pl.BlockSpec((None, hb, 1, LANE), lambda b, h, c, bg: (b, h, 0, 0)),
        ],
        scratch_shapes=[
            pltpu.VMEM((hb, e, e + LANE), F32),
            pltpu.VMEM((hb, 1, 1), F32),
            pltpu.VMEM((lc + 2 * SUBLANE, w), F32),
            pltpu.VMEM((lc + 2 * SUBLANE, w), F32),
            pltpu.VMEM((lc, w), F32),
        ],
    )
    return pl.pallas_call(
        body,
        grid_spec=grid_spec,
        out_shape=[
            jax.ShapeDtypeStruct((batch * seq, D_MLSTM), out_dtype),
            jax.ShapeDtypeStruct((batch, N_MH, e, e), F32),
            jax.ShapeDtypeStruct((batch, N_MH, 1, e), F32),
            jax.ShapeDtypeStruct((batch, N_MH, 1, LANE), F32),
        ],
        compiler_params=_cparams(("arbitrary", "arbitrary", "arbitrary")),
    )(b_gate, xp, xp, xp, xp, gates_t, hist, hist, conv_w, conv_w, conv_b, conv_b, c0, n0, m0, gain)


ATT_BLK = 2048
ATT_GROUP = 8


def _attn_body(sl_ref, q_ref, kc_ref, vc_ref, kp_ref, vp_ref, gain_ref, o_ref, o_s, m_s, l_s):
    blk = pl.program_id(1)
    h = pl.program_id(2)
    slope = sl_ref[h]
    scale = E_AH ** -0.5
    qi = lax.broadcasted_iota(jnp.int32, (BAND, BAND), 0)
    ci = lax.broadcasted_iota(jnp.int32, (BAND, BAND), 1)
    dist_prev = (BAND + qi - ci).astype(F32)
    dist_cur = (qi - ci).astype(F32)
    ok_prev = ci >= qi
    ok_cur = ci <= qi

    def run_units(cfg, d, specs):
        sd = slope * float(d)
        bm_cur = jnp.where(ok_cur, -sd * dist_cur, NEG_INF)
        nt = (((1,), (1,)), ((), ()))
        scores = []
        for prev_pen, q_sl, kp_src, kc_src, _, _ in specs:
            bm_prev = jnp.where(ok_prev, prev_pen - sd * dist_prev, NEG_INF)
            qv = q_ref[q_sl, :].astype(BF16)
            kp = kp_src[0][kp_src[1], :].astype(BF16)
            kc = kc_src[0][kc_src[1], :].astype(BF16)
            s_p = lax.dot_general(qv, kp, nt, preferred_element_type=F32) * scale + bm_prev
            s_c = lax.dot_general(qv, kc, nt, preferred_element_type=F32) * scale + bm_cur
            scores.append((s_p, s_c))
        probs = []
        for s_p, s_c in scores:
            m = jnp.max(jnp.maximum(s_p, s_c), axis=1, keepdims=True)
            p_p = jnp.exp(s_p - m)
            p_c = jnp.exp(s_c - m)
            l = jnp.sum(p_p + p_c, axis=1, keepdims=True)
            probs.append((m, l, p_p.astype(BF16), p_c.astype(BF16)))
        for (_, q_sl, _, _, vp_src, vc_src), (m, l, p_p, p_c) in zip(specs, probs):
            vp = vp_src[0][vp_src[1], :].astype(BF16)
            vc = vc_src[0][vc_src[1], :].astype(BF16)
            o = jnp.dot(p_p, vp, preferred_element_type=F32) + jnp.dot(p_c, vc, preferred_element_type=F32)
            o_s[cfg, q_sl, :] = o
            m_s[cfg, q_sl, :] = jnp.broadcast_to(m, (BAND, E_AH))
            l_s[cfg, q_sl, :] = jnp.broadcast_to(l, (BAND, E_AH))

    first_pen = jnp.where(blk == 0, NEG_INF, 0.0)

    for cfg, (win, d) in enumerate(DILATED_CONFIGS):
        span = BAND * d
        n_u = ATT_BLK // span

        def sl(r, u, _d=d, _span=span):
            if _d == 1:
                if isinstance(u, int):
                    return pl.ds(u * _span, BAND)
                return pl.ds(pl.multiple_of(u * _span, BAND), BAND)
            return pl.ds(r + u * _span, BAND, stride=_d)

        def head_unit(r, d=d, span=span, sl=sl):
            cur = sl(r, 0)
            prv = sl(r + ATT_BLK - span, 0) if d > 1 else pl.ds(ATT_BLK - span, BAND)
            return (first_pen, cur, (kp_ref, prv), (kc_ref, cur), (vp_ref, prv), (vc_ref, cur))

        def tail_unit(r, u, sl=sl):
            cur = sl(r, u)
            prv = sl(r, u - 1)
            return (0.0, cur, (kc_ref, prv), (kc_ref, cur), (vc_ref, prv), (vc_ref, cur))

        specs = [head_unit(r) if u == 0 else tail_unit(r, u) for u in range(n_u) for r in range(d)]
        for g in range(0, len(specs), ATT_GROUP):
            run_units(cfg, d, specs[g:g + ATT_GROUP])

    rows = 256

    def merge(i, _):
        rs = pl.ds(pl.multiple_of(i * rows, rows), rows)
        m0, m1, m2 = m_s[0, rs, :], m_s[1, rs, :], m_s[2, rs, :]
        m_all = jnp.maximum(jnp.maximum(m0, m1), m2)
        e0, e1, e2 = jnp.exp(m0 - m_all), jnp.exp(m1 - m_all), jnp.exp(m2 - m_all)
        num = e0 * o_s[0, rs, :] + e1 * o_s[1, rs, :] + e2 * o_s[2, rs, :]
        den = e0 * l_s[0, rs, :] + e1 * l_s[1, rs, :] + e2 * l_s[2, rs, :]
        ha = num / den
        ha = ha * lax.rsqrt(jnp.mean(ha * ha, axis=1, keepdims=True) + EPS)
        o_ref[rs, :] = (ha * gain_ref[...]).astype(o_ref.dtype)
        return 0

    lax.fori_loop(0, ATT_BLK // rows, merge, 0)


def _attn_prompt(xp, gain, slopes, *, batch, seq):
    nb = seq // ATT_BLK
    qc, kc, vc = COL_QA // E_AH, COL_KA // E_AH, COL_VA // E_AH
    cur = lambda b, i, h, s: b * nb + i
    prev = lambda b, i, h, s: b * nb + jnp.maximum(i - 1, 0)
    grid_spec = pltpu.PrefetchScalarGridSpec(
        num_scalar_prefetch=1,
        grid=(batch, nb, N_AH),
        in_specs=[
            pl.BlockSpec((ATT_BLK, E_AH), lambda b, i, h, s: (cur(b, i, h, s), qc + h)),
            pl.BlockSpec((ATT_BLK, E_AH), lambda b, i, h, s: (cur(b, i, h, s), kc + h)),
            pl.BlockSpec((ATT_BLK, E_AH), lambda b, i, h, s: (cur(b, i, h, s), vc + h)),
            pl.BlockSpec((ATT_BLK, E_AH), lambda b, i, h, s: (prev(b, i, h, s), kc + h)),
            pl.BlockSpec((ATT_BLK, E_AH), lambda b, i, h, s: (prev(b, i, h, s), vc + h)),
            pl.BlockSpec((1, E_AH), lambda b, i, h, s: (0, h)),
        ],
        out_specs=pl.BlockSpec((ATT_BLK, E_AH), lambda b, i, h, s: (cur(b, i, h, s), h)),
        scratch_shapes=[pltpu.VMEM((3, ATT_BLK, E_AH), F32)] * 3,
    )
    return pl.pallas_call(
        _attn_body,
        grid_spec=grid_spec,
        out_shape=jax.ShapeDtypeStruct((batch * seq, N_AH * E_AH), BF16),
        compiler_params=_cparams(("arbitrary", "arbitrary", "arbitrary")),
    )(slopes, xp, xp, xp, xp, xp, gain)


DEC_NEAR = 8


def _near_multiplicity():
    return [sum(1 for win, d in DILATED_CONFIGS if dist % d == 0 and dist <= win) for dist in range(DEC_NEAR)]


def _decode_body(q_ref, kn_ref, vn_ref, kc_ref, vc_ref, slope_ref, gain_ref,
                 ha_ref, ko_hbm, vo_hbm, ktail, vtail, sem, *, w_buf, s_new):
    b = pl.program_id(0)
    keep = w_buf - s_new
    copies = [
        pltpu.make_async_copy(kc_ref.at[0, 0, pl.ds(s_new, keep)], ko_hbm.at[0, b, pl.ds(0, keep)], sem.at[0]),
        pltpu.make_async_copy(vc_ref.at[0, 0, pl.ds(s_new, keep)], vo_hbm.at[0, b, pl.ds(0, keep)], sem.at[1]),
        pltpu.make_async_copy(kn_ref.at[0], ko_hbm.at[0, b, pl.ds(keep, s_new)], sem.at[2]),
        pltpu.make_async_copy(vn_ref.at[0], vo_hbm.at[0, b, pl.ds(keep, s_new)], sem.at[3]),
    ]
    for cp in copies:
        cp.start()

    ktail[pl.ds(0, DEC_NEAR)] = kc_ref[0, 0, pl.ds(w_buf - DEC_NEAR, DEC_NEAR)]
    vtail[pl.ds(0, DEC_NEAR)] = vc_ref[0, 0, pl.ds(w_buf - DEC_NEAR, DEC_NEAR)]
    ktail[pl.ds(DEC_NEAR, s_new)] = kn_ref[0]
    vtail[pl.ds(DEC_NEAR, s_new)] = vn_ref[0]

    scale = E_AH ** -0.5
    slope = slope_ref[:, 0:1]
    near_i = lax.broadcasted_iota(jnp.int32, (DEC_NEAR, 1, 1), 0)
    near_dist = (DEC_NEAR - 1 - near_i).astype(F32)
    near_mult = jnp.zeros((DEC_NEAR, 1, 1), F32)
    for dist, c in enumerate(_near_multiplicity()):
        near_mult = jnp.where(near_i == DEC_NEAR - 1 - dist, float(c), near_mult)

    def far_part(s, win, d):
        n = N_BACK - (DEC_NEAR - 1) // d
        sl = pl.ds(w_buf + s - win, n, stride=d) if d > 1 else pl.ds(w_buf + s - win, n)
        i = lax.broadcasted_iota(jnp.int32, (n, 1, 1), 0)
        return sl, ((N_BACK - i) * d).astype(F32)

    ones = jnp.ones((E_AH, E_AH), BF16)

    def row_dots(kk, q):
        n = kk.shape[0]
        prod = (kk * q).astype(BF16).reshape(n * N_AH, E_AH)
        return jnp.dot(prod, ones, preferred_element_type=F32).reshape(n, N_AH, E_AH)

    def one_query(s, _):
        q = q_ref[0, s]
        parts = []
        for win, d in DILATED_CONFIGS:
            sl, dist = far_part(s, win, d)
            sc = row_dots(kc_ref[0, 0, sl], q) * scale - slope * dist
            parts.append((sc, None, lambda sl=sl: vc_ref[0, 0, sl]))
        nsl = pl.ds(s + 1, DEC_NEAR)
        sc = row_dots(ktail[nsl], q) * scale - slope * near_dist
        parts.append((sc, near_mult, lambda: vtail[nsl]))
        m = functools.reduce(jnp.maximum, [jnp.max(p[0], axis=0, keepdims=True) for p in parts])
        den = jnp.zeros((1, N_AH, E_AH), F32)
        o = jnp.zeros((1, N_AH, E_AH), F32)
        for sc, mu, load_v in parts:
            p = jnp.exp(sc - m)
            if mu is not None:
                p = p * mu
            den = den + jnp.sum(p, axis=0, keepdims=True)
            o = o + jnp.sum(p * load_v(), axis=0, keepdims=True)
        o = (o / den)[0]
        o = o * lax.rsqrt(jnp.mean(o * o, axis=-1, keepdims=True) + EPS)
        ha_ref[0, s] = o * gain_ref[...]
        return 0

    lax.fori_loop(0, s_new, one_query, 0)
    for cp in copies:
        cp.wait()


def _attn_decode(q3, kn3, vn3, cache_k, cache_v, gain, slopes, *, batch, s_new):
    w_buf = cache_k.shape[2]
    assert w_buf >= max(w for w, _ in DILATED_CONFIGS) and s_new <= DEC_NEAR
    assert all(w // d == N_BACK and d & (d - 1) == 0 for w, d in DILATED_CONFIGS)
    body = functools.partial(_decode_body, w_buf=w_buf, s_new=s_new)
    new_spec = pl.BlockSpec((1, s_new, N_AH, E_AH), lambda b: (b, 0, 0, 0))
    cache_spec = pl.BlockSpec((1, 1, w_buf, N_AH, E_AH), lambda b: (0, b, 0, 0, 0))
    tile_spec = pl.BlockSpec((N_AH, E_AH), lambda b: (0, 0))
    return pl.pallas_call(
        body,
        grid=(batch,),
        in_specs=[new_spec, new_spec, new_spec, cache_spec, cache_spec, tile_spec, tile_spec],
        out_specs=[new_spec, pl.BlockSpec(memory_space=pl.ANY), pl.BlockSpec(memory_space=pl.ANY)],
        out_shape=[
            jax.ShapeDtypeStruct((batch, s_new, N_AH, E_AH), F32),
            jax.ShapeDtypeStruct(cache_k.shape, F32),
            jax.ShapeDtypeStruct(cache_v.shape, F32),
        ],
        scratch_shapes=[pltpu.VMEM((DEC_NEAR + s_new, N_AH, E_AH), F32)] * 2 + [pltpu.SemaphoreType.DMA((4,))],
        compiler_params=_cparams(("arbitrary",)),
    )(q3, kn3, vn3, cache_k, cache_v, slopes, gain)


OP_TM = 512
ROUTE_W = LANE
PACK_ROWS = D_MODEL // LANE


def _layer_norm(z, g, b):
    mu = jnp.mean(z, axis=1, keepdims=True)
    zc = z - mu
    var = jnp.mean(zc * zc, axis=1, keepdims=True)
    return zc * lax.rsqrt(var + EPS) * g + b


def _outproj_body(hm_ref, ha_ref, x_ref, wm_ref, wa_ref, g_ref, b_ref, wrh_ref, wrl_ref, br_ref,
                  x1_ref, x1p_ref, route_ref):
    y = (jnp.dot(hm_ref[...].astype(BF16), wm_ref[...], preferred_element_type=F32)
         + jnp.dot(ha_ref[...].astype(BF16), wa_ref[...], preferred_element_type=F32))
    x1 = _layer_norm(ALPHA * x_ref[...] + y, g_ref[...], b_ref[...])
    x1_ref[...] = x1
    tm = x1.shape[0]
    for j in range(PACK_ROWS):
        x1p_ref[pl.ds(j, tm, stride=PACK_ROWS), :] = x1[:, j * LANE:(j + 1) * LANE]
    x1_hi = x1.astype(BF16)
    x1_lo = (x1 - x1_hi.astype(F32)).astype(BF16)
    logits = (jnp.dot(x1_hi, wrh_ref[...], preferred_element_type=F32)
              + jnp.dot(x1_lo, wrh_ref[...], preferred_element_type=F32)
              + jnp.dot(x1_hi, wrl_ref[...], preferred_element_type=F32)) + br_ref[...]
    lane = lax.broadcasted_iota(jnp.int32, (tm, ROUTE_W), 1)
    lane_f = lane.astype(F32)
    big = float(ROUTE_W)
    gl = jnp.where(lane < N_GROUPS, logits, NEG_INF)
    gmax = jnp.max(gl, axis=1, keepdims=True)
    g_w = 1.0 / jnp.sum(jnp.exp(gl - gmax), axis=1, keepdims=True)
    g_idx = jnp.min(jnp.where(gl == gmax, lane_f, big), axis=1, keepdims=True)
    lo = N_GROUPS + E_PER_GROUP * g_idx
    el = jnp.where(jnp.logical_and(lane_f >= lo, lane_f < lo + E_PER_GROUP), logits, NEG_INF)
    v1 = jnp.max(el, axis=1, keepdims=True)
    i1 = jnp.min(jnp.where(el == v1, lane_f, big), axis=1, keepdims=True)
    el2 = jnp.where(lane_f == i1, NEG_INF, el)
    v2 = jnp.max(el2, axis=1, keepdims=True)
    i2 = jnp.min(jnp.where(el2 == v2, lane_f, big), axis=1, keepdims=True)
    e2 = jnp.exp(v2 - v1)
    w1 = g_w / (1.0 + e2)
    w2 = g_w * e2 / (1.0 + e2)
    route = jnp.where(lane == 0, i1 - N_GROUPS,
                      jnp.where(lane == 1, i2 - N_GROUPS,
                                jnp.where(lane == 2, w1, jnp.where(lane == 3, w2, 0.0))))
    route_ref[...] = route


def _outproj_two_groups(hmp_ref, hap_ref, xp_ref, hms_ref, has_ref, xs_ref, *rest, steps_p):
    i = pl.program_id(0)

    @pl.when(i < steps_p)
    def _():
        _outproj_body(hmp_ref, hap_ref, xp_ref, *rest)

    @pl.when(i >= steps_p)
    def _():
        _outproj_body(hms_ref, has_ref, xs_ref, *rest)


def _outproj(hm_p, ha_p, x_p, hm_s, ha_s, x_s, wo, g, b, wr_hi, wr_lo, br):
    tm = OP_TM
    n_p, n_s = x_p.shape[0], x_s.shape[0]
    assert n_p % tm == 0 and n_s <= tm
    steps_p = n_p // tm
    n = n_p + tm
    pad = lambda a: jnp.pad(a, ((0, tm - n_s), (0, 0)))
    hm_s, ha_s, x_s = pad(hm_s), pad(ha_s), pad(x_s)
    row = lambda i: (i, 0)
    prow = lambda i: (jnp.minimum(i, steps_p - 1), 0)
    fixed = lambda i: (0, 0)
    once = pl.Buffered(1)
    return pl.pallas_call(
        functools.partial(_outproj_two_groups, steps_p=steps_p),
        grid=(steps_p + 1,),
        in_specs=[
            pl.BlockSpec((tm, D_MLSTM), prow),
            pl.BlockSpec((tm, N_AH * E_AH), prow),
            pl.BlockSpec((tm, D_MODEL), prow),
            pl.BlockSpec((tm, D_MLSTM), fixed),
            pl.BlockSpec((tm, N_AH * E_AH), fixed),
            pl.BlockSpec((tm, D_MODEL), fixed),
            pl.BlockSpec((D_MLSTM, D_MODEL), fixed, pipeline_mode=once),
            pl.BlockSpec((N_AH * E_AH, D_MODEL), lambda i: (1, 0), pipeline_mode=once),
            pl.BlockSpec((1, D_MODEL), fixed),
            pl.BlockSpec((1, D_MODEL), fixed),
            pl.BlockSpec((D_MODEL, ROUTE_W), fixed),
            pl.BlockSpec((D_MODEL, ROUTE_W), fixed),
            pl.BlockSpec((1, ROUTE_W), fixed),
        ],
        out_specs=[pl.BlockSpec((tm, D_MODEL), row), pl.BlockSpec((tm * PACK_ROWS, LANE), row),
                   pl.BlockSpec((tm, ROUTE_W), row)],
        out_shape=[jax.ShapeDtypeStruct((n, D_MODEL), F32), jax.ShapeDtypeStruct((n * PACK_ROWS, LANE), F32),
                   jax.ShapeDtypeStruct((n, ROUTE_W), F32)],
        compiler_params=_cparams(("arbitrary",)),
    )(hm_p, ha_p, x_p, hm_s, ha_s, x_s, wo, wo, g, b, wr_hi, wr_lo, br)


MOE_R = 128
MOE_MAXB = 6
MOE_FC = 256
MOE_NC = D_FF // MOE_FC
MOE_ISSUE_UNROLL = 8


def _moe_body(ex_ref, row0_ref, nb_ref, src_ref, x1p_hbm, wg_ref, wu_ref, wd_ref, ys_hbm,
              ubuf, xbuf, acc, sem_in, sem_out):
    t = pl.program_id(0)
    c = pl.program_id(1)
    n_items = pl.num_programs(0)
    slot = t % 2
    nb = nb_ref[t]

    def rows(base, r):
        return pl.ds(pl.multiple_of(base + r * MOE_R, MOE_R), MOE_R)

    def tile_rows(i):
        return pl.ds(pl.multiple_of(i * PACK_ROWS, PACK_ROWS), PACK_ROWS)

    def out_copy(tt, sl, r):
        return pltpu.make_async_copy(acc.at[sl, rows(0, r), :], ys_hbm.at[rows(row0_ref[tt], r), :], sem_out.at[sl])

    def for_blocks(n, fn):
        def body(r, _):
            fn(r)
            return 0

        lax.fori_loop(0, n, body, 0)

    def n_in(tt):
        return jnp.maximum(nb_ref[tt], 0)

    def n_out(tt):
        return jnp.abs(nb_ref[tt])

    def start_gather(tt, sl):
        base = row0_ref[tt]

        def group(gi):
            for k in range(MOE_ISSUE_UNROLL):
                i = gi * MOE_ISSUE_UNROLL + k
                tok = src_ref[base + i]
                pltpu.make_async_copy(x1p_hbm.at[tile_rows(tok), :], ubuf.at[sl, tile_rows(i), :], sem_in.at[sl]).start()

        for_blocks(n_in(tt) * (MOE_R // MOE_ISSUE_UNROLL), group)

    def wait_gather(tt, sl):
        def block(r):
            span = pl.ds(pl.multiple_of(r * (MOE_R * PACK_ROWS), MOE_R * PACK_ROWS), MOE_R * PACK_ROWS)
            pltpu.make_async_copy(x1p_hbm.at[pl.ds(0, MOE_R * PACK_ROWS), :], ubuf.at[sl, span, :], sem_in.at[sl]).wait()

        for_blocks(n_in(tt), block)

    def unpack_block(r):
        for j in range(PACK_ROWS):
            start = r * (MOE_R * PACK_ROWS) + j
            u = ubuf[slot, pl.ds(start, MOE_R, stride=PACK_ROWS), :]
            xbuf[rows(0, r), j * LANE:(j + 1) * LANE] = u.astype(BF16)

    @pl.when(c == 0)
    def _begin():
        @pl.when(t == 0)
        def _():
            start_gather(0, 0)

        @pl.when(t + 1 < n_items)
        def _():
            start_gather(t + 1, 1 - slot)

        wait_gather(t, slot)
        for_blocks(n_in(t), unpack_block)

        @pl.when(t >= 2)
        def _():
            for_blocks(n_out(t - 2), lambda r: out_copy(t - 2, slot, r).wait())

        @pl.when(nb < 0)
        def _():
            acc[slot] = jnp.zeros(acc.shape[1:], F32)

    @pl.when(nb > 0)
    def _compute():
        wg = wg_ref[...].astype(BF16)
        wu = wu_ref[...].astype(BF16)
        wd = wd_ref[...].astype(BF16)

        def piece(off, size):
            rs = pl.ds(pl.multiple_of(off, MOE_R), size)
            x = xbuf[rs, :]
            gt = jnp.dot(x, wg, preferred_element_type=F32)
            up = jnp.dot(x, wu, preferred_element_type=F32)
            hid = (gt * jax.nn.sigmoid(gt) * up).astype(BF16)
            y = jnp.dot(hid, wd, preferred_element_type=F32)

            @pl.when(c == 0)
            def _():
                acc[slot, rs, :] = y

            @pl.when(c > 0)
            def _():
                acc[slot, rs, :] = acc[slot, rs, :] + y

        n4 = nb // 4
        rem = nb - 4 * n4
        for_blocks(n4, lambda i: piece(i * (4 * MOE_R), 4 * MOE_R))

        @pl.when(rem >= 2)
        def _():
            piece(n4 * (4 * MOE_R), 2 * MOE_R)

        @pl.when(rem % 2 == 1)
        def _():
            piece(n4 * (4 * MOE_R) + (rem // 2) * (2 * MOE_R), MOE_R)

    @pl.when(c == MOE_NC - 1)
    def _end():
        for_blocks(n_out(t), lambda r: out_copy(t, slot, r).start())

        @pl.when(t == n_items - 1)
        def _():
            for_blocks(n_out(t), lambda r: out_copy(t, slot, r).wait())

            @pl.when(t >= 1)
            def _():
                for_blocks(n_out(t - 1), lambda r: out_copy(t - 1, 1 - slot, r).wait())


def _moe(item_e, item_row0, item_nb, src, x1p, w_gate, w_up, w_down, n_items, n_rows):
    def chunk(t, c, n):
        return jnp.where(n[t] > 0, c, MOE_NC - 1)

    grid_spec = pltpu.PrefetchScalarGridSpec(
        num_scalar_prefetch=4,
        grid=(n_items, MOE_NC),
        in_specs=[
            pl.BlockSpec(memory_space=pl.ANY),
            pl.BlockSpec((None, D_MODEL, MOE_FC), lambda t, c, e, r, n, s: (e[t], 0, chunk(t, c, n))),
            pl.BlockSpec((None, D_MODEL, MOE_FC), lambda t, c, e, r, n, s: (e[t], 0, chunk(t, c, n))),
            pl.BlockSpec((None, MOE_FC, D_MODEL), lambda t, c, e, r, n, s: (e[t], chunk(t, c, n), 0)),
        ],
        out_specs=pl.BlockSpec(memory_space=pl.ANY),
        scratch_shapes=[
            pltpu.VMEM((2, MOE_MAXB * MOE_R * PACK_ROWS, LANE), F32),
            pltpu.VMEM((MOE_MAXB * MOE_R, D_MODEL), BF16),
            pltpu.VMEM((2, MOE_MAXB * MOE_R, D_MODEL), F32),
            pltpu.SemaphoreType.DMA((2,)),
            pltpu.SemaphoreType.DMA((2,)),
        ],
    )
    return pl.pallas_call(
        _moe_body,
        grid_spec=grid_spec,
        out_shape=jax.ShapeDtypeStruct((n_rows, D_MODEL), F32),
        compiler_params=_cparams(("arbitrary", "arbitrary")),
    )(item_e, item_row0, item_nb, src, x1p, w_gate, w_up, w_down)


COMB_TM = 128
COMB_ISSUE_UNROLL = 4


def _comb_issue(pos_ref, ys_hbm, buf, sem, tile, slot, tm):
    def body(g, _):
        for u in range(COMB_ISSUE_UNROLL):
            r = g * COMB_ISSUE_UNROLL + u
            for k in range(2):
                p = pos_ref[2 * (tile * tm + r) + k]
                pltpu.make_async_copy(ys_hbm.at[pl.ds(p, 1), :], buf.at[slot, k, pl.ds(r, 1), :], sem.at[slot]).start()
        return 0

    lax.fori_loop(0, tm // COMB_ISSUE_UNROLL, body, 0)


def _combine_body(pos_ref, ys_hbm, x1_ref, route_ref, g_ref, b_ref, o_ref, buf, sem, *, tm):
    i = pl.program_id(0)
    n = pl.num_programs(0)
    slot = i % 2

    @pl.when(i == 0)
    def _():
        _comb_issue(pos_ref, ys_hbm, buf, sem, 0, 0, tm)

    @pl.when(i + 1 < n)
    def _():
        _comb_issue(pos_ref, ys_hbm, buf, sem, i + 1, 1 - slot, tm)

    for k in range(2):
        pltpu.make_async_copy(ys_hbm.at[pl.ds(0, tm), :], buf.at[slot, k], sem.at[slot]).wait()
    f = buf[slot, 0] * route_ref[:, 2:3] + buf[slot, 1] * route_ref[:, 3:4]
    o_ref[...] = _layer_norm(ALPHA * x1_ref[...] + f, g_ref[...], b_ref[...])


def _combine(pos, ys, x1, route, g, b, *, row0, n):
    d = x1.shape[1]
    tm = COMB_TM
    assert row0 % tm == 0 and n % tm == 0
    off = row0 // tm
    grid_spec = pltpu.PrefetchScalarGridSpec(
        num_scalar_prefetch=1,
        grid=(n // tm,),
        in_specs=[
            pl.BlockSpec(memory_space=pl.ANY),
            pl.BlockSpec((tm, d), lambda i, p: (i + off, 0)),
            pl.BlockSpec((tm, ROUTE_W), lambda i, p: (i + off, 0)),
            pl.BlockSpec((1, d), lambda i, p: (0, 0)),
            pl.BlockSpec((1, d), lambda i, p: (0, 0)),
        ],
        out_specs=pl.BlockSpec((tm, d), lambda i, p: (i, 0)),
        scratch_shapes=[pltpu.VMEM((2, 2, tm, d), F32), pltpu.SemaphoreType.DMA((2,))],
    )
    return pl.pallas_call(
        functools.partial(_combine_body, tm=tm),
        grid_spec=grid_spec,
        out_shape=jax.ShapeDtypeStruct((n, d), F32),
        compiler_params=_cparams(("arbitrary",)),
    )(pos, ys, x1, route, g, b)


def _dispatch_plan(eid, n_items, n_rows):
    p_total = eid.shape[0]
    blk = 128
    assert p_total % blk == 0
    onehot = (eid[:, None] == jnp.arange(N_EXPERTS, dtype=jnp.int32)[None, :]).astype(F32)
    counts = jnp.sum(onehot, axis=0).astype(jnp.int32)
    ohb = onehot.reshape(p_total // blk, blk, N_EXPERTS)
    earlier = (jnp.arange(blk)[:, None] > jnp.arange(blk)[None, :]).astype(F32)
    within = jnp.einsum("ij,bjk->bik", earlier, ohb, precision=lax.Precision.HIGHEST)
    blk_tot = jnp.sum(ohb, axis=1)
    blk_off = jnp.cumsum(blk_tot, axis=0) - blk_tot
    rank = jnp.sum((within + blk_off[:, None, :]) * ohb, axis=2).reshape(p_total).astype(jnp.int32)
    nblk = (counts + MOE_R - 1) // MOE_R
    seg_start = (jnp.cumsum(nblk) - nblk) * MOE_R
    pos = seg_start[eid] + rank
    src = jnp.zeros((n_rows,), jnp.int32).at[pos].set(jnp.arange(p_total, dtype=jnp.int32) // 2)
    items_per_e = (nblk + MOE_MAXB - 1) // MOE_MAXB
    item_end = jnp.cumsum(items_per_e)
    item_start = item_end - items_per_e
    t = jnp.arange(n_items, dtype=jnp.int32)
    e_t = jnp.minimum(jnp.sum((item_end[None, :] <= t[:, None]).astype(jnp.int32), axis=1), N_EXPERTS - 1)
    live = t < item_end[-1]
    local = t - item_start[e_t]
    used = jnp.sum(nblk)
    idle0 = used + (t - item_end[-1]) * MOE_MAXB
    nz_t = jnp.clip(n_rows // MOE_R - idle0, 0, MOE_MAXB)
    nb_t = jnp.where(live, jnp.clip(nblk[e_t] - local * MOE_MAXB, 0, MOE_MAXB), -nz_t)
    row0_t = jnp.where(live, seg_start[e_t] + local * (MOE_MAXB * MOE_R),
                       jnp.minimum(idle0, n_rows // MOE_R - 1) * MOE_R)
    last_e = e_t[jnp.maximum(item_end[-1] - 1, 0)]
    e_t = jnp.where(live, e_t, last_e)
    return pos.astype(jnp.int32), src, e_t.astype(jnp.int32), row0_t.astype(jnp.int32), nb_t.astype(jnp.int32)


def _alibi_slopes():
    return jnp.asarray([2.0 ** (-8.0 * (h + 1) / N_AH) for h in range(N_AH)], dtype=F32)


def kernel(x_prompt, x_sample, state_conv, state_mlstm_C, state_mlstm_n, state_mlstm_m, cache_win_k, cache_win_v, w_in, b_gate, conv_w, conv_b, mh_gain, att_gain, w_out, ln1_g, ln1_b, w_group, b_group, w_router, b_router, w_gate, w_up, w_down, ln2_g, ln2_b):
    bp, tp, d = x_prompt.shape
    bs, ts, _ = x_sample.shape
    assert d == D_MODEL and w_in.shape[0] == 1 and tp % ATT_BLK == 0 and ts >= CONV_W - 1
    w_buf = cache_win_k.shape[2]
    n_p, n_s = bp * tp, bs * ts
    slopes = _alibi_slopes()

    wi = w_in[0]
    g0 = 4 * D_MLSTM
    g1 = g0 + 2 * N_MH
    w_pack = jnp.concatenate(
        [wi[:, :g0], wi[:, g1:], wi[:, g0:g1], jnp.zeros((d, LANE - 2 * N_MH), F32)], axis=1).astype(BF16)

    xp2 = x_prompt.reshape(n_p, d)
    xs2 = x_sample.reshape(n_s, d)
    proj_p = _in_proj(xp2, w_pack, 512)
    proj_s = _in_proj(xs2, w_pack, n_s)

    cw = conv_w[0]
    cb = conv_b[0][None, :]
    mh_g = mh_gain[0][None, :]
    att_g = att_gain[0][None, :]
    bg = b_gate[0]

    def gates_time_major(proj, batch, seq, pad_to):
        gt = proj[:, COL_G:COL_G + 2 * N_MH].reshape(batch, seq, 2, N_MH).transpose(0, 3, 2, 1)
        if pad_to > seq:
            gt = jnp.pad(gt, ((0, 0), (0, 0), (0, 0), (0, pad_to - seq)))
        return gt

    lc_p = 256
    hm_p, c_p, n_pp, m_p = _mlstm(
        proj_p, gates_time_major(proj_p, bp, tp, tp), jnp.zeros((bp, CONV_W - 1, 2 * D_MLSTM), F32), cw, cb,
        jnp.zeros((bp, N_MH, E_MH, E_MH), F32), jnp.zeros((bp, N_MH, 1, E_MH), F32),
        jnp.zeros((bp, N_MH, 1, 1), F32), mh_g, bg, batch=bp, seq=tp, lb=lc_p, lc=lc_p, hb=2, out_dtype=BF16)
    ha_p = _attn_prompt(proj_p, att_g, slopes, batch=bp, seq=tp)

    lc_s = 16
    hm_s, c_s, n_ss, m_s = _mlstm(
        proj_s, gates_time_major(proj_s, bs, ts, lc_s), state_conv[0], cw, cb,
        state_mlstm_C[0], state_mlstm_n[0][:, :, None, :], state_mlstm_m[0][:, :, None, None],
        mh_g, bg, batch=bs, seq=ts, lb=ts, lc=lc_s, hb=N_MH, out_dtype=F32)
    new_rows = lambda col: proj_s[:, col:col + N_AH * E_AH].reshape(bs, ts, N_AH, E_AH)
    ha_s, wk_s, wv_s = _attn_decode(
        new_rows(COL_QA), new_rows(COL_KA), new_rows(COL_VA), cache_win_k, cache_win_v,
        att_gain[0].reshape(N_AH, E_AH), jnp.broadcast_to(slopes[:, None], (N_AH, E_AH)), batch=bs, s_new=ts)
    ha_s = ha_s.reshape(n_s, N_AH * E_AH)

    n_all = n_p + n_s
    wo = w_out[0].astype(BF16)
    w_r = jnp.concatenate(
        [w_group[0], w_router[0].transpose(1, 0, 2).reshape(d, N_EXPERTS),
         jnp.zeros((d, ROUTE_W - N_GROUPS - N_EXPERTS), F32)], axis=1)
    b_r = jnp.concatenate(
        [b_group[0], b_router[0].reshape(N_EXPERTS), jnp.zeros((ROUTE_W - N_GROUPS - N_EXPERTS,), F32)])[None, :]
    wr_hi = w_r.astype(BF16)
    wr_lo = (w_r - wr_hi.astype(F32)).astype(BF16)
    ln1 = (ln1_g[0][None, :], ln1_b[0][None, :])
    x1, x1p, route = _outproj(hm_p, ha_p, xp2, hm_s, ha_s, xs2, wo, *ln1, wr_hi, wr_lo, b_r)

    p_total = 2 * n_all
    n_rows = ((p_total + N_EXPERTS * (MOE_R - 1)) // MOE_R + 1) * MOE_R
    n_items = N_EXPERTS + n_rows // (MOE_R * MOE_MAXB)
    eid = route[:n_all, 0:2].astype(jnp.int32).reshape(p_total)
    pos, src, item_e, item_row0, item_nb = _dispatch_plan(eid, n_items, n_rows)
    ys = _moe(item_e, item_row0, item_nb, src, x1p, w_gate[0], w_up[0], w_down[0], n_items, n_rows)
    ln2 = (ln2_g[0][None, :], ln2_b[0][None, :])
    y_p = _combine(pos[:2 * n_p], ys, x1, route, *ln2, row0=0, n=n_p).reshape(bp, tp, d)
    y_s = _combine(pos[2 * n_p:], ys, x1, route, *ln2, row0=n_p, n=n_s).reshape(bs, ts, d)

    def tail_rows(proj, batch, seq, col, width, rows):
        return proj.reshape(batch, seq, N_PROJ)[:, seq - rows:, col:col + width]

    win = min(w_buf, tp)
    p_conv = tail_rows(proj_p, bp, tp, COL_QM, 2 * D_MLSTM, CONV_W - 1)[None]
    p_wk = tail_rows(proj_p, bp, tp, COL_KA, N_AH * E_AH, win).reshape(1, bp, win, N_AH, E_AH)
    p_wv = tail_rows(proj_p, bp, tp, COL_VA, N_AH * E_AH, win).reshape(1, bp, win, N_AH, E_AH)
    s_conv = tail_rows(proj_s, bs, ts, COL_QM, 2 * D_MLSTM, CONV_W - 1)[None]
    return (y_p, y_s,
            p_conv, c_p[None], n_pp[:, :, 0, :][None], m_p[:, :, 0, 0][None], p_wk, p_wv,
            s_conv, c_s[None], n_ss[:, :, 0, :][None], m_s[:, :, 0, 0][None], wk_s, wv_s)
```

```python
import functools
import math

import jax
import jax.numpy as jnp
from jax import lax
from jax.experimental import pallas as pl
from jax.experimental.pallas import tpu as pltpu

F32 = jnp.float32
BF16 = jnp.bfloat16
NEG_INF = float("-inf")

D_MODEL = 2048
D_MLSTM = 1024
N_MH = 4
E_MH = 256
N_AH = 8
E_AH = 128
DILATED_CONFIGS = ((128, 1), (512, 4), (2048, 16))
N_BACK = 128
BAND = 128
CONV_W = 4
N_GROUPS = 4
E_PER_GROUP = 8
N_EXPERTS = 32
D_FF = 1024
EPS = 1e-5
ALPHA = 2.0 ** 0.25

LANE = 128
SUBLANE = 8

COL_QM, COL_KM, COL_VM, COL_OM = 0, 1024, 2048, 3072
COL_QA, COL_KA, COL_VA, COL_G = 4096, 5120, 6144, 7168
N_PROJ = 7296
PROJ_TN = 2432

VMEM_LIMIT = 56 * 1024 * 1024


def _cparams(sem):
    return pltpu.CompilerParams(dimension_semantics=sem, vmem_limit_bytes=VMEM_LIMIT)


def _proj_body(xp_ref, xs_ref, w_ref, o_ref, *, steps_p):
    i = pl.program_id(1)

    @pl.when(i < steps_p)
    def _():
        o_ref[...] = jnp.dot(xp_ref[...].astype(BF16), w_ref[...], preferred_element_type=F32)

    @pl.when(i >= steps_p)
    def _():
        o_ref[...] = jnp.dot(xs_ref[...].astype(BF16), w_ref[...], preferred_element_type=F32)


def _in_proj(x_p, x_s, w, tm):
    n_p, n_s = x_p.shape[0], x_s.shape[0]
    assert n_p % tm == 0 and n_s <= tm
    steps_p = n_p // tm
    x_s = jnp.pad(x_s, ((0, tm - n_s), (0, 0)))
    return pl.pallas_call(
        functools.partial(_proj_body, steps_p=steps_p),
        grid=(N_PROJ // PROJ_TN, steps_p + 1),
        in_specs=[pl.BlockSpec((tm, D_MODEL), lambda j, i: (jnp.minimum(i, steps_p - 1), 0)),
                  pl.BlockSpec((tm, D_MODEL), lambda j, i: (0, 0)),
                  pl.BlockSpec((D_MODEL, PROJ_TN), lambda j, i: (0, j))],
        out_specs=pl.BlockSpec((tm, PROJ_TN), lambda j, i: (i, j)),
        out_shape=jax.ShapeDtypeStruct((n_p + tm, N_PROJ), F32),
        compiler_params=_cparams(("arbitrary", "arbitrary")),
    )(x_p, x_s, w)


def _mlstm_body(bg_ref, xq_ref, xk_ref, v_ref, om_ref, g_ref, hq_ref, hk_ref, cwq_ref, cwk_ref,
                cbq_ref, cbk_ref, c0_ref, n0_ref, m0_ref, gain_ref,
                hm_ref, cout_ref, nout_ref, mout_ref,
                caug, m_s, uq, uk, vbuf, *, lb, lc, t_valid, hb):
    hg = pl.program_id(1)
    c = pl.program_id(2)
    nc = pl.num_programs(2)
    e = E_MH
    heads = range(hb)
    cols = lambda i: slice(i * e, (i + 1) * e)

    row_e = lax.broadcasted_iota(jnp.int32, (e, e), 0)
    col_e = lax.broadcasted_iota(jnp.int32, (e, e), 1)
    eye_e = row_e == col_e

    @pl.when(c == 0)
    def _init():
        for i in heads:
            caug[i, :, 0:e] = c0_ref[i]
            ncol = jnp.sum(jnp.where(eye_e, n0_ref[i], 0.0), axis=1, keepdims=True)
            caug[i, :, e:e + LANE] = jnp.broadcast_to(ncol, (e, LANE))
            m_s[i] = m0_ref[i]
        uq[...] = jnp.zeros(uq.shape, F32)
        uk[...] = jnp.zeros(uk.shape, F32)
        uq[pl.ds(SUBLANE - (CONV_W - 1), CONV_W - 1), :] = hq_ref[...]
        uk[pl.ds(SUBLANE - (CONV_W - 1), CONV_W - 1), :] = hk_ref[...]
        if lb != lc:
            vbuf[...] = jnp.zeros(vbuf.shape, F32)

    uq[pl.ds(SUBLANE, lb), :] = xq_ref[...]
    uk[pl.ds(SUBLANE, lb), :] = xk_ref[...]

    def conv_silu(u, cw_ref, cb_ref):
        acc = cb_ref[...]
        for j in range(CONV_W):
            acc = acc + u[pl.ds(SUBLANE - (CONV_W - 1) + j, lc), :] * cw_ref[j:j + 1, :]
        return acc * jax.nn.sigmoid(acc)

    q_all = conv_silu(uq, cwq_ref, cbq_ref)
    k_all = conv_silu(uk, cwk_ref, cbk_ref) * (e ** -0.5)
    if lb != lc:
        vbuf[pl.ds(0, lb), :] = v_ref[...]
        v_all = vbuf[...]
    else:
        v_all = v_ref[...]

    tq = uq[pl.ds(lc + SUBLANE - (CONV_W - 1), CONV_W - 1), :]
    tk = uk[pl.ds(lc + SUBLANE - (CONV_W - 1), CONV_W - 1), :]
    uq[pl.ds(SUBLANE - (CONV_W - 1), CONV_W - 1), :] = tq
    uk[pl.ds(SUBLANE - (CONV_W - 1), CONV_W - 1), :] = tk

    lane_t = lax.broadcasted_iota(jnp.int32, (1, lc), 1)
    valid = (lane_t + c * lc) < t_valid
    row_l = lax.broadcasted_iota(jnp.int32, (lc, lc), 0)
    col_l = lax.broadcasted_iota(jnp.int32, (lc, lc), 1)
    causal = col_l <= row_l
    eye_l = col_l == row_l
    nt = (((1,), (1,)), ((), ()))
    tn = (((0,), (0,)), ((), ()))

    st = []
    for i in heads:
        h = hg * hb + i
        gi = g_ref[i, 0:1, :] + bg_ref[h]
        gf = g_ref[i, 1:2, :] + bg_ref[N_MH + h]
        lf = -(jnp.maximum(-gf, 0.0) + jnp.log1p(jnp.exp(-jnp.abs(gf))))
        ig_row = jnp.where(valid, gi, NEG_INF)
        lf_row = jnp.where(valid, lf, 0.0)
        b_col = jnp.sum(jnp.where(causal, lf_row, 0.0), axis=1, keepdims=True)
        b_row = jnp.sum(jnp.where(eye_l, b_col, 0.0), axis=0, keepdims=True)
        m_prev = m_s[i]
        dlog = jnp.where(causal, b_col - b_row + ig_row, NEG_INF)
        a_col = b_col + m_prev
        m_t = jnp.maximum(a_col, jnp.max(dlog, axis=1, keepdims=True))
        st.append(dict(ig_row=ig_row, b_row=b_row, m_prev=m_prev, m_t=m_t,
                       w_intra=jnp.exp(dlog - m_t), w_inter=jnp.exp(a_col - m_t),
                       qb=q_all[:, cols(i)].astype(BF16), kb=k_all[:, cols(i)].astype(BF16),
                       vb=v_all[:, cols(i)].astype(BF16)))

    for i in heads:
        d = st[i]
        d["s"] = lax.dot_general(d["qb"], d["kb"], nt, preferred_element_type=F32) * d["w_intra"]
        d["qc"] = jnp.dot(d["qb"], caug[i].astype(BF16), preferred_element_type=F32)

    for i in heads:
        d = st[i]
        num = jnp.dot(d["s"].astype(BF16), d["vb"], preferred_element_type=F32) + d["w_inter"] * d["qc"][:, 0:e]
        den = jnp.sum(d["s"], axis=1, keepdims=True) + d["w_inter"] * d["qc"][:, e:e + 1]
        hh = num / jnp.maximum(jnp.abs(den), jnp.exp(-d["m_t"]))
        hh = hh - jnp.mean(hh, axis=1, keepdims=True)
        hh = hh * lax.rsqrt(jnp.mean(hh * hh, axis=1, keepdims=True) + EPS)
        if lb != lc:
            hh = hh[0:lb, :]
        hm = hh * gain_ref[:, cols(i)] * jax.nn.sigmoid(om_ref[:, cols(i)])
        hm_ref[:, cols(i)] = hm.astype(hm_ref.dtype)

    for i in heads:
        d = st[i]
        b_last = d["b_row"][:, lc - 1:lc]
        logw_row = b_last - d["b_row"] + d["ig_row"]
        m_new = jnp.maximum(b_last + d["m_prev"], jnp.max(logw_row, axis=1, keepdims=True))
        w_row = jnp.exp(logw_row - m_new)
        w_col = jnp.sum(jnp.where(eye_l, w_row, 0.0), axis=1, keepdims=True)
        decay = jnp.exp(b_last + d["m_prev"] - m_new)
        kw = (k_all[:, cols(i)] * w_col).astype(BF16)
        vaug = jnp.concatenate([d["vb"], jnp.ones((lc, LANE), BF16)], axis=1)
        upd = lax.dot_general(kw, vaug, tn, preferred_element_type=F32)
        caug[i] = decay * caug[i] + upd
        m_s[i] = m_new

    @pl.when(c == nc - 1)
    def _fin():
        for i in heads:
            cout_ref[i] = caug[i, :, 0:e]
            nout_ref[i] = jnp.sum(jnp.where(eye_e, caug[i, :, e:e + 1], 0.0), axis=0, keepdims=True)
            mout_ref[i] = jnp.broadcast_to(m_s[i], (1, LANE))


def _mlstm(xp, gates_t, hist, conv_w, conv_b, c0, n0, m0, gain, b_gate, *, batch, seq, lb, lc, hb, out_dtype):
    nc = seq // lb
    e = E_MH
    w = hb * e
    ng = N_MH // hb
    assert N_MH % hb == 0
    rb = lambda b, h, c, bg: b * nc + c
    qcol, kcol, vcol, ocol = COL_QM // w, COL_KM // w, COL_VM // w, COL_OM // w
    body = functools.partial(_mlstm_body, lb=lb, lc=lc, t_valid=seq if lb == lc else lb, hb=hb)
    grid_spec = pltpu.PrefetchScalarGridSpec(
        num_scalar_prefetch=1,
        grid=(batch, ng, nc),
        in_specs=[
            pl.BlockSpec((lb, w), lambda b, h, c, bg: (rb(b, h, c, bg), qcol + h)),
            pl.BlockSpec((lb, w), lambda b, h, c, bg: (rb(b, h, c, bg), kcol + h)),
            pl.BlockSpec((lb, w), lambda b, h, c, bg: (rb(b, h, c, bg), vcol + h)),
            pl.BlockSpec((lb, w), lambda b, h, c, bg: (rb(b, h, c, bg), ocol + h)),
            pl.BlockSpec((None, hb, 2, lc), lambda b, h, c, bg: (b, h, 0, c)),
            pl.BlockSpec((None, CONV_W - 1, w), lambda b, h, c, bg: (b, 0, h)),
            pl.BlockSpec((None, CONV_W - 1, w), lambda b, h, c, bg: (b, 0, ng + h)),
            pl.BlockSpec((CONV_W, w), lambda b, h, c, bg: (0, h)),
            pl.BlockSpec((CONV_W, w), lambda b, h, c, bg: (0, ng + h)),
            pl.BlockSpec((1, w), lambda b, h, c, bg: (0, h)),
            pl.BlockSpec((1, w), lambda b, h, c, bg: (0, ng + h)),
            pl.BlockSpec((None, hb, e, e), lambda b, h, c, bg: (b, h, 0, 0)),
            pl.BlockSpec((None, hb, 1, e), lambda b, h, c, bg: (b, h, 0, 0)),
            pl.BlockSpec((None, hb, 1, 1), lambda b, h, c, bg: (b, h, 0, 0)),
            pl.BlockSpec((1, w), lambda b, h, c, bg: (0, h)),
        ],
        out_specs=[
            pl.BlockSpec((lb, w), lambda b, h, c, bg: (rb(b, h, c, bg), h)),
            pl.BlockSpec((None, hb, e, e), lambda b, h, c, bg: (b, h, 0, 0)),
            pl.BlockSpec((None, hb, 1, e), lambda b, h, c, bg: (b, h, 0, 0)),
            pl.BlockSpec((None, hb, 1, LANE), lambda b, h, c, bg: (b, h, 0, 0)),
        ],
        scratch_shapes=[
            pltpu.VMEM((hb, e, e + LANE), F32),
            pltpu.VMEM((hb, 1, 1), F32),
            pltpu.VMEM((lc + 2 * SUBLANE, w), F32),
            pltpu.VMEM((lc + 2 * SUBLANE, w), F32),
            pltpu.VMEM((lc, w), F32),
        ],
    )
    return pl.pallas_call(
        body,
        grid_spec=grid_spec,
        out_shape=[
            jax.ShapeDtypeStruct((batch * seq, D_MLSTM), out_dtype),
            jax.ShapeDtypeStruct((batch, N_MH, e, e), F32),
            jax.ShapeDtypeStruct((batch, N_MH, 1, e), F32),
            jax.ShapeDtypeStruct((batch, N_MH, 1, LANE), F32),
        ],
        compiler_params=_cparams(("arbitrary", "arbitrary", "arbitrary")),
    )(b_gate, xp, xp, xp, xp, gates_t, hist, hist, conv_w, conv_w, conv_b, conv_b, c0, n0, m0, gain)


ATT_BLK = 2048
ATT_GROUP = 8


def _attn_body(sl_ref, q_ref, kc_ref, vc_ref, kp_ref, vp_ref, gain_ref, o_ref, o_s, m_s, l_s):
    blk = pl.program_id(1)
    h = pl.program_id(2)
    slope = sl_ref[h]
    scale = E_AH ** -0.5
    qi = lax.broadcasted_iota(jnp.int32, (BAND, BAND), 0)
    ci = lax.broadcasted_iota(jnp.int32, (BAND, BAND), 1)
    dist_prev = (BAND + qi - ci).astype(F32)
    dist_cur = (qi - ci).astype(F32)
    ok_prev = ci >= qi
    ok_cur = ci <= qi

    def run_units(cfg, d, specs):
        sd = slope * float(d)
        bm_cur = jnp.where(ok_cur, -sd * dist_cur, NEG_INF)
        nt = (((1,), (1,)), ((), ()))
        scores = []
        for prev_pen, q_sl, kp_src, kc_src, _, _ in specs:
            bm_prev = jnp.where(ok_prev, prev_pen - sd * dist_prev, NEG_INF)
            qv = q_ref[q_sl, :].astype(BF16)
            kp = kp_src[0][kp_src[1], :].astype(BF16)
            kc = kc_src[0][kc_src[1], :].astype(BF16)
            s_p = lax.dot_general(qv, kp, nt, preferred_element_type=F32) * scale + bm_prev
            s_c = lax.dot_general(qv, kc, nt, preferred_element_type=F32) * scale + bm_cur
            scores.append((s_p, s_c))
        probs = []
        for s_p, s_c in scores:
            m = jnp.max(jnp.maximum(s_p, s_c), axis=1, keepdims=True)
            p_p = jnp.exp(s_p - m)
            p_c = jnp.exp(s_c - m)
            l = jnp.sum(p_p + p_c, axis=1, keepdims=True)
            probs.append((m, l, p_p.astype(BF16), p_c.astype(BF16)))
        for (_, q_sl, _, _, vp_src, vc_src), (m, l, p_p, p_c) in zip(specs, probs):
            vp = vp_src[0][vp_src[1], :].astype(BF16)
            vc = vc_src[0][vc_src[1], :].astype(BF16)
            o = jnp.dot(p_p, vp, preferred_element_type=F32) + jnp.dot(p_c, vc, preferred_element_type=F32)
            o_s[cfg, q_sl, :] = o
            m_s[cfg, q_sl, :] = jnp.broadcast_to(m, (BAND, E_AH))
            l_s[cfg, q_sl, :] = jnp.broadcast_to(l, (BAND, E_AH))

    first_pen = jnp.where(blk == 0, NEG_INF, 0.0)

    for cfg, (win, d) in enumerate(DILATED_CONFIGS):
        span = BAND * d
        n_u = ATT_BLK // span

        def sl(r, u, _d=d, _span=span):
            if _d == 1:
                if isinstance(u, int):
                    return pl.ds(u * _span, BAND)
                return pl.ds(pl.multiple_of(u * _span, BAND), BAND)
            return pl.ds(r + u * _span, BAND, stride=_d)

        def head_unit(r, d=d, span=span, sl=sl):
            cur = sl(r, 0)
            prv = sl(r + ATT_BLK - span, 0) if d > 1 else pl.ds(ATT_BLK - span, BAND)
            return (first_pen, cur, (kp_ref, prv), (kc_ref, cur), (vp_ref, prv), (vc_ref, cur))

        def tail_unit(r, u, sl=sl):
            cur = sl(r, u)
            prv = sl(r, u - 1)
            return (0.0, cur, (kc_ref, prv), (kc_ref, cur), (vc_ref, prv), (vc_ref, cur))

        specs = [head_unit(r) if u == 0 else tail_unit(r, u) for u in range(n_u) for r in range(d)]
        for g in range(0, len(specs), ATT_GROUP):
            run_units(cfg, d, specs[g:g + ATT_GROUP])

    rows = 256

    def merge(i, _):
        rs = pl.ds(pl.multiple_of(i * rows, rows), rows)
        m0, m1, m2 = m_s[0, rs, :], m_s[1, rs, :], m_s[2, rs, :]
        m_all = jnp.maximum(jnp.maximum(m0, m1), m2)
        e0, e1, e2 = jnp.exp(m0 - m_all), jnp.exp(m1 - m_all), jnp.exp(m2 - m_all)
        num = e0 * o_s[0, rs, :] + e1 * o_s[1, rs, :] + e2 * o_s[2, rs, :]
        den = e0 * l_s[0, rs, :] + e1 * l_s[1, rs, :] + e2 * l_s[2, rs, :]
        ha = num / den
        ha = ha * lax.rsqrt(jnp.mean(ha * ha, axis=1, keepdims=True) + EPS)
        o_ref[rs, :] = (ha * gain_ref[...]).astype(o_ref.dtype)
        return 0

    lax.fori_loop(0, ATT_BLK // rows, merge, 0)


def _attn_prompt(xp, gain, slopes, *, batch, seq):
    nb = seq // ATT_BLK
    qc, kc, vc = COL_QA // E_AH, COL_KA // E_AH, COL_VA // E_AH
    cur = lambda b, i, h, s: b * nb + i
    prev = lambda b, i, h, s: b * nb + jnp.maximum(i - 1, 0)
    grid_spec = pltpu.PrefetchScalarGridSpec(
        num_scalar_prefetch=1,
        grid=(batch, nb, N_AH),
        in_specs=[
            pl.BlockSpec((ATT_BLK, E_AH), lambda b, i, h, s: (cur(b, i, h, s), qc + h)),
            pl.BlockSpec((ATT_BLK, E_AH), lambda b, i, h, s: (cur(b, i, h, s), kc + h)),
            pl.BlockSpec((ATT_BLK, E_AH), lambda b, i, h, s: (cur(b, i, h, s), vc + h)),
            pl.BlockSpec((ATT_BLK, E_AH), lambda b, i, h, s: (prev(b, i, h, s), kc + h)),
            pl.BlockSpec((ATT_BLK, E_AH), lambda b, i, h, s: (prev(b, i, h, s), vc + h)),
            pl.BlockSpec((1, E_AH), lambda b, i, h, s: (0, h)),
        ],
        out_specs=pl.BlockSpec((ATT_BLK, E_AH), lambda b, i, h, s: (cur(b, i, h, s), h)),
        scratch_shapes=[pltpu.VMEM((3, ATT_BLK, E_AH), F32)] * 3,
    )
    return pl.pallas_call(
        _attn_body,
        grid_spec=grid_spec,
        out_shape=jax.ShapeDtypeStruct((batch * seq, N_AH * E_AH), BF16),
        compiler_params=_cparams(("arbitrary", "arbitrary", "arbitrary")),
    )(slopes, xp, xp, xp, xp, xp, gain)


DEC_NEAR = 8


def _near_multiplicity():
    return [sum(1 for win, d in DILATED_CONFIGS if dist % d == 0 and dist <= win) for dist in range(DEC_NEAR)]


def _decode_body(q_ref, kn_ref, vn_ref, kc_ref, vc_ref, slope_ref, gain_ref,
                 ha_ref, ko_hbm, vo_hbm, ktail, vtail, sem, *, w_buf, s_new):
    b = pl.program_id(0)
    keep = w_buf - s_new
    copies = [
        pltpu.make_async_copy(kc_ref.at[0, 0, pl.ds(s_new, keep)], ko_hbm.at[0, b, pl.ds(0, keep)], sem.at[0]),
        pltpu.make_async_copy(vc_ref.at[0, 0, pl.ds(s_new, keep)], vo_hbm.at[0, b, pl.ds(0, keep)], sem.at[1]),
        pltpu.make_async_copy(kn_ref.at[0], ko_hbm.at[0, b, pl.ds(keep, s_new)], sem.at[2]),
        pltpu.make_async_copy(vn_ref.at[0], vo_hbm.at[0, b, pl.ds(keep, s_new)], sem.at[3]),
    ]
    for cp in copies:
        cp.start()

    ktail[pl.ds(0, DEC_NEAR)] = kc_ref[0, 0, pl.ds(w_buf - DEC_NEAR, DEC_NEAR)]
    vtail[pl.ds(0, DEC_NEAR)] = vc_ref[0, 0, pl.ds(w_buf - DEC_NEAR, DEC_NEAR)]
    ktail[pl.ds(DEC_NEAR, s_new)] = kn_ref[0]
    vtail[pl.ds(DEC_NEAR, s_new)] = vn_ref[0]

    scale = E_AH ** -0.5
    slope = slope_ref[:, 0:1]
    near_i = lax.broadcasted_iota(jnp.int32, (DEC_NEAR, 1, 1), 0)
    near_dist = (DEC_NEAR - 1 - near_i).astype(F32)
    near_mult = jnp.zeros((DEC_NEAR, 1, 1), F32)
    for dist, c in enumerate(_near_multiplicity()):
        near_mult = jnp.where(near_i == DEC_NEAR - 1 - dist, float(c), near_mult)

    def far_part(s, win, d):
        n = N_BACK - (DEC_NEAR - 1) // d
        sl = pl.ds(w_buf + s - win, n, stride=d) if d > 1 else pl.ds(w_buf + s - win, n)
        i = lax.broadcasted_iota(jnp.int32, (n, 1, 1), 0)
        return sl, ((N_BACK - i) * d).astype(F32)

    ones = jnp.ones((E_AH, E_AH), BF16)

    def row_dots(kk, q):
        n = kk.shape[0]
        prod = (kk * q).astype(BF16).reshape(n * N_AH, E_AH)
        return jnp.dot(prod, ones, preferred_element_type=F32).reshape(n, N_AH, E_AH)

    def one_query(s, _):
        q = q_ref[0, s]
        parts = []
        for win, d in DILATED_CONFIGS:
            sl, dist = far_part(s, win, d)
            sc = row_dots(kc_ref[0, 0, sl], q) * scale - slope * dist
            parts.append((sc, None, lambda sl=sl: vc_ref[0, 0, sl]))
        nsl = pl.ds(s + 1, DEC_NEAR)
        sc = row_dots(ktail[nsl], q) * scale - slope * near_dist
        parts.append((sc, near_mult, lambda: vtail[nsl]))
        m = functools.reduce(jnp.maximum, [jnp.max(p[0], axis=0, keepdims=True) for p in parts])
        den = jnp.zeros((1, N_AH, E_AH), F32)
        o = jnp.zeros((1, N_AH, E_AH), F32)
        for sc, mu, load_v in parts:
            p = jnp.exp(sc - m)
            if mu is not None:
                p = p * mu
            den = den + jnp.sum(p, axis=0, keepdims=True)
            o = o + jnp.sum(p * load_v(), axis=0, keepdims=True)
        o = (o / den)[0]
        o = o * lax.rsqrt(jnp.mean(o * o, axis=-1, keepdims=True) + EPS)
        ha_ref[0, s] = o * gain_ref[...]
        return 0

    lax.fori_loop(0, s_new, one_query, 0)
    for cp in copies:
        cp.wait()


def _attn_decode(q3, kn3, vn3, cache_k, cache_v, gain, slopes, *, batch, s_new):
    w_buf = cache_k.shape[2]
    assert w_buf >= max(w for w, _ in DILATED_CONFIGS) and s_new <= DEC_NEAR
    assert all(w // d == N_BACK and d & (d - 1) == 0 for w, d in DILATED_CONFIGS)
    body = functools.partial(_decode_body, w_buf=w_buf, s_new=s_new)
    new_spec = pl.BlockSpec((1, s_new, N_AH, E_AH), lambda b: (b, 0, 0, 0))
    cache_spec = pl.BlockSpec((1, 1, w_buf, N_AH, E_AH), lambda b: (0, b, 0, 0, 0))
    tile_spec = pl.BlockSpec((N_AH, E_AH), lambda b: (0, 0))
    return pl.pallas_call(
        body,
        grid=(batch,),
        in_specs=[new_spec, new_spec, new_spec, cache_spec, cache_spec, tile_spec, tile_spec],
        out_specs=[new_spec, pl.BlockSpec(memory_space=pl.ANY), pl.BlockSpec(memory_space=pl.ANY)],
        out_shape=[
            jax.ShapeDtypeStruct((batch, s_new, N_AH, E_AH), F32),
            jax.ShapeDtypeStruct(cache_k.shape, F32),
            jax.ShapeDtypeStruct(cache_v.shape, F32),
        ],
        scratch_shapes=[pltpu.VMEM((DEC_NEAR + s_new, N_AH, E_AH), F32)] * 2 + [pltpu.SemaphoreType.DMA((4,))],
        compiler_params=_cparams(("arbitrary",)),
    )(q3, kn3, vn3, cache_k, cache_v, slopes, gain)


OP_TM = 512
ROUTE_W = LANE
PACK_ROWS = D_MODEL // LANE


def _layer_norm(z, g, b):
    mu = jnp.mean(z, axis=1, keepdims=True)
    zc = z - mu
    var = jnp.mean(zc * zc, axis=1, keepdims=True)
    return zc * lax.rsqrt(var + EPS) * g + b


def _outproj_body(hm_ref, ha_ref, x_ref, wm_ref, wa_ref, g_ref, b_ref, wrh_ref, wrl_ref, br_ref,
                  x1_ref, x1p_ref, route_ref):
    y = (jnp.dot(hm_ref[...].astype(BF16), wm_ref[...], preferred_element_type=F32)
         + jnp.dot(ha_ref[...].astype(BF16), wa_ref[...], preferred_element_type=F32))
    x1 = _layer_norm(ALPHA * x_ref[...] + y, g_ref[...], b_ref[...])
    x1_ref[...] = x1
    tm = x1.shape[0]
    for j in range(PACK_ROWS):
        x1p_ref[pl.ds(j, tm, stride=PACK_ROWS), :] = x1[:, j * LANE:(j + 1) * LANE]
    x1_hi = x1.astype(BF16)
    x1_lo = (x1 - x1_hi.astype(F32)).astype(BF16)
    logits = (jnp.dot(x1_hi, wrh_ref[...], preferred_element_type=F32)
              + jnp.dot(x1_lo, wrh_ref[...], preferred_element_type=F32)
              + jnp.dot(x1_hi, wrl_ref[...], preferred_element_type=F32)) + br_ref[...]
    lane = lax.broadcasted_iota(jnp.int32, (tm, ROUTE_W), 1)
    lane_f = lane.astype(F32)
    big = float(ROUTE_W)
    gl = jnp.where(lane < N_GROUPS, logits, NEG_INF)
    gmax = jnp.max(gl, axis=1, keepdims=True)
    g_w = 1.0 / jnp.sum(jnp.exp(gl - gmax), axis=1, keepdims=True)
    g_idx = jnp.min(jnp.where(gl == gmax, lane_f, big), axis=1, keepdims=True)
    lo = N_GROUPS + E_PER_GROUP * g_idx
    el = jnp.where(jnp.logical_and(lane_f >= lo, lane_f < lo + E_PER_GROUP), logits, NEG_INF)
    v1 = jnp.max(el, axis=1, keepdims=True)
    i1 = jnp.min(jnp.where(el == v1, lane_f, big), axis=1, keepdims=True)
    el2 = jnp.where(lane_f == i1, NEG_INF, el)
    v2 = jnp.max(el2, axis=1, keepdims=True)
    i2 = jnp.min(jnp.where(el2 == v2, lane_f, big), axis=1, keepdims=True)
    e2 = jnp.exp(v2 - v1)
    w1 = g_w / (1.0 + e2)
    w2 = g_w * e2 / (1.0 + e2)
    route = jnp.where(lane == 0, i1 - N_GROUPS,
                      jnp.where(lane == 1, i2 - N_GROUPS,
                                jnp.where(lane == 2, w1, jnp.where(lane == 3, w2, 0.0))))
    route_ref[...] = route


def _outproj_two_groups(hmp_ref, hap_ref, xp_ref, hms_ref, has_ref, xs_ref, *rest, steps_p):
    i = pl.program_id(0)

    @pl.when(i < steps_p)
    def _():
        _outproj_body(hmp_ref, hap_ref, xp_ref, *rest)

    @pl.when(i >= steps_p)
    def _():
        _outproj_body(hms_ref, has_ref, xs_ref, *rest)


def _outproj(hm_p, ha_p, x_p, hm_s, ha_s, x_s, wo, g, b, wr_hi, wr_lo, br):
    tm = OP_TM
    n_p, n_s = x_p.shape[0], x_s.shape[0]
    assert n_p % tm == 0 and n_s <= tm
    steps_p = n_p // tm
    n = n_p + tm
    pad = lambda a: jnp.pad(a, ((0, tm - n_s), (0, 0)))
    hm_s, ha_s, x_s = pad(hm_s), pad(ha_s), pad(x_s)
    row = lambda i: (i, 0)
    prow = lambda i: (jnp.minimum(i, steps_p - 1), 0)
    fixed = lambda i: (0, 0)
    once = pl.Buffered(1)
    return pl.pallas_call(
        functools.partial(_outproj_two_groups, steps_p=steps_p),
        grid=(steps_p + 1,),
        in_specs=[
            pl.BlockSpec((tm, D_MLSTM), prow),
            pl.BlockSpec((tm, N_AH * E_AH), prow),
            pl.BlockSpec((tm, D_MODEL), prow),
            pl.BlockSpec((tm, D_MLSTM), fixed),
            pl.BlockSpec((tm, N_AH * E_AH), fixed),
            pl.BlockSpec((tm, D_MODEL), fixed),
            pl.BlockSpec((D_MLSTM, D_MODEL), fixed, pipeline_mode=once),
            pl.BlockSpec((N_AH * E_AH, D_MODEL), lambda i: (1, 0), pipeline_mode=once),
            pl.BlockSpec((1, D_MODEL), fixed),
            pl.BlockSpec((1, D_MODEL), fixed),
            pl.BlockSpec((D_MODEL, ROUTE_W), fixed),
            pl.BlockSpec((D_MODEL, ROUTE_W), fixed),
            pl.BlockSpec((1, ROUTE_W), fixed),
        ],
        out_specs=[pl.BlockSpec((tm, D_MODEL), row), pl.BlockSpec((tm * PACK_ROWS, LANE), row),
                   pl.BlockSpec((tm, ROUTE_W), row)],
        out_shape=[jax.ShapeDtypeStruct((n, D_MODEL), F32), jax.ShapeDtypeStruct((n * PACK_ROWS, LANE), F32),
                   jax.ShapeDtypeStruct((n, ROUTE_W), F32)],
        compiler_params=_cparams(("arbitrary",)),
    )(hm_p, ha_p, x_p, hm_s, ha_s, x_s, wo, wo, g, b, wr_hi, wr_lo, br)


MOE_R = 128
MOE_MAXB = 6
MOE_FC = 256
MOE_NC = D_FF // MOE_FC
MOE_ISSUE_UNROLL = 32


def _moe_body(ex_ref, row0_ref, nb_ref, src_ref, x1p_hbm, wg_ref, wu_ref, wd_ref, ys_hbm,
              ubuf, xbuf, acc, sem_in, sem_out):
    t = pl.program_id(0)
    c = pl.program_id(1)
    n_items = pl.num_programs(0)
    slot = t % 2
    nb = nb_ref[t]

    def rows(base, r):
        return pl.ds(pl.multiple_of(base + r * MOE_R, MOE_R), MOE_R)

    def tile_rows(i):
        return pl.ds(pl.multiple_of(i * PACK_ROWS, PACK_ROWS), PACK_ROWS)

    def out_copy(tt, sl, r):
        return pltpu.make_async_copy(acc.at[sl, rows(0, r), :], ys_hbm.at[rows(row0_ref[tt], r), :], sem_out.at[sl])

    def for_blocks(n, fn):
        def body(r, _):
            fn(r)
            return 0

        lax.fori_loop(0, n, body, 0)

    def n_in(tt):
        return jnp.maximum(nb_ref[tt], 0)

    def n_out(tt):
        return jnp.abs(nb_ref[tt])

    def start_gather(tt, sl):
        base = row0_ref[tt]

        def group(gi):
            for k in range(MOE_ISSUE_UNROLL):
                i = gi * MOE_ISSUE_UNROLL + k
                tok = src_ref[base + i]
                pltpu.make_async_copy(x1p_hbm.at[tile_rows(tok), :], ubuf.at[sl, tile_rows(i), :], sem_in.at[sl]).start()

        for_blocks(n_in(tt) * (MOE_R // MOE_ISSUE_UNROLL), group)

    def wait_gather(tt, sl):
        def block(r):
            span = pl.ds(pl.multiple_of(r * (MOE_R * PACK_ROWS), MOE_R * PACK_ROWS), MOE_R * PACK_ROWS)
            pltpu.make_async_copy(x1p_hbm.at[pl.ds(0, MOE_R * PACK_ROWS), :], ubuf.at[sl, span, :], sem_in.at[sl]).wait()

        for_blocks(n_in(tt), block)

    def unpack_block(r):
        for j in range(PACK_ROWS):
            start = r * (MOE_R * PACK_ROWS) + j
            u = ubuf[slot, pl.ds(start, MOE_R, stride=PACK_ROWS), :]
            xbuf[rows(0, r), j * LANE:(j + 1) * LANE] = u.astype(BF16)

    @pl.when(c == 0)
    def _begin():
        @pl.when(t == 0)
        def _():
            start_gather(0, 0)

        @pl.when(t + 1 < n_items)
        def _():
            start_gather(t + 1, 1 - slot)

        wait_gather(t, slot)
        for_blocks(n_in(t), unpack_block)

        @pl.when(t >= 2)
        def _():
            for_blocks(n_out(t - 2), lambda r: out_copy(t - 2, slot, r).wait())

        @pl.when(nb < 0)
        def _():
            acc[slot] = jnp.zeros(acc.shape[1:], F32)

    @pl.when(nb > 0)
    def _compute():
        wg = wg_ref[...].astype(BF16)
        wu = wu_ref[...].astype(BF16)
        wd = wd_ref[...].astype(BF16)

        def piece(off, size):
            rs = pl.ds(pl.multiple_of(off, MOE_R), size)
            x = xbuf[rs, :]
            gt = jnp.dot(x, wg, preferred_element_type=F32)
            up = jnp.dot(x, wu, preferred_element_type=F32)
            hid = (gt * jax.nn.sigmoid(gt) * up).astype(BF16)
            y = jnp.dot(hid, wd, preferred_element_type=F32)

            @pl.when(c == 0)
            def _():
                acc[slot, rs, :] = y

            @pl.when(c > 0)
            def _():
                acc[slot, rs, :] = acc[slot, rs, :] + y

        n4 = nb // 4
        rem = nb - 4 * n4
        for_blocks(n4, lambda i: piece(i * (4 * MOE_R), 4 * MOE_R))

        @pl.when(rem >= 2)
        def _():
            piece(n4 * (4 * MOE_R), 2 * MOE_R)

        @pl.when(rem % 2 == 1)
        def _():
            piece(n4 * (4 * MOE_R) + (rem // 2) * (2 * MOE_R), MOE_R)

    @pl.when(c == MOE_NC - 1)
    def _end():
        for_blocks(n_out(t), lambda r: out_copy(t, slot, r).start())

        @pl.when(t == n_items - 1)
        def _():
            for_blocks(n_out(t), lambda r: out_copy(t, slot, r).wait())

            @pl.when(t >= 1)
            def _():
                for_blocks(n_out(t - 1), lambda r: out_copy(t - 1, 1 - slot, r).wait())


def _moe(item_e, item_row0, item_nb, src, x1p, w_gate, w_up, w_down, n_items, n_rows):
    def chunk(t, c, n):
        return jnp.where(n[t] > 0, c, MOE_NC - 1)

    grid_spec = pltpu.PrefetchScalarGridSpec(
        num_scalar_prefetch=4,
        grid=(n_items, MOE_NC),
        in_specs=[
            pl.BlockSpec(memory_space=pl.ANY),
            pl.BlockSpec((None, D_MODEL, MOE_FC), lambda t, c, e, r, n, s: (e[t], 0, chunk(t, c, n))),
            pl.BlockSpec((None, D_MODEL, MOE_FC), lambda t, c, e, r, n, s: (e[t], 0, chunk(t, c, n))),
            pl.BlockSpec((None, MOE_FC, D_MODEL), lambda t, c, e, r, n, s: (e[t], chunk(t, c, n), 0)),
        ],
        out_specs=pl.BlockSpec(memory_space=pl.ANY),
        scratch_shapes=[
            pltpu.VMEM((2, MOE_MAXB * MOE_R * PACK_ROWS, LANE), F32),
            pltpu.VMEM((MOE_MAXB * MOE_R, D_MODEL), BF16),
            pltpu.VMEM((2, MOE_MAXB * MOE_R, D_MODEL), F32),
            pltpu.SemaphoreType.DMA((2,)),
            pltpu.SemaphoreType.DMA((2,)),
        ],
    )
    return pl.pallas_call(
        _moe_body,
        grid_spec=grid_spec,
        out_shape=jax.ShapeDtypeStruct((n_rows, D_MODEL), F32),
        compiler_params=_cparams(("arbitrary", "arbitrary")),
    )(item_e, item_row0, item_nb, src, x1p, w_gate, w_up, w_down)


COMB_TM = 128
COMB_ISSUE_UNROLL = 32


def _comb_issue(pos_ref, ys_hbm, buf, sem, tile, slot, tm):
    def body(g, _):
        for u in range(COMB_ISSUE_UNROLL):
            r = g * COMB_ISSUE_UNROLL + u
            for k in range(2):
                p = pos_ref[2 * (tile * tm + r) + k]
                pltpu.make_async_copy(ys_hbm.at[pl.ds(p, 1), :], buf.at[slot, k, pl.ds(r, 1), :], sem.at[slot]).start()
        return 0

    lax.fori_loop(0, tm // COMB_ISSUE_UNROLL, body, 0)


def _combine_body(pos_ref, ys_hbm, x1_ref, route_ref, g_ref, b_ref, o_ref, buf, sem, *, tm):
    i = pl.program_id(0)
    n = pl.num_programs(0)
    slot = i % 2

    @pl.when(i == 0)
    def _():
        _comb_issue(pos_ref, ys_hbm, buf, sem, 0, 0, tm)

    @pl.when(i + 1 < n)
    def _():
        _comb_issue(pos_ref, ys_hbm, buf, sem, i + 1, 1 - slot, tm)

    for k in range(2):
        pltpu.make_async_copy(ys_hbm.at[pl.ds(0, tm), :], buf.at[slot, k], sem.at[slot]).wait()
    f = buf[slot, 0] * route_ref[:, 2:3] + buf[slot, 1] * route_ref[:, 3:4]
    o_ref[...] = _layer_norm(ALPHA * x1_ref[...] + f, g_ref[...], b_ref[...])


def _combine(pos, ys, x1, route, g, b, *, row0, n):
    d = x1.shape[1]
    tm = COMB_TM
    assert row0 % tm == 0 and n % tm == 0
    off = row0 // tm
    grid_spec = pltpu.PrefetchScalarGridSpec(
        num_scalar_prefetch=1,
        grid=(n // tm,),
        in_specs=[
            pl.BlockSpec(memory_space=pl.ANY),
            pl.BlockSpec((tm, d), lambda i, p: (i + off, 0)),
            pl.BlockSpec((tm, ROUTE_W), lambda i, p: (i + off, 0)),
            pl.BlockSpec((1, d), lambda i, p: (0, 0)),
            pl.BlockSpec((1, d), lambda i, p: (0, 0)),
        ],
        out_specs=pl.BlockSpec((tm, d), lambda i, p: (i, 0)),
        scratch_shapes=[pltpu.VMEM((2, 2, tm, d), F32), pltpu.SemaphoreType.DMA((2,))],
    )
    return pl.pallas_call(
        functools.partial(_combine_body, tm=tm),
        grid_spec=grid_spec,
        out_shape=jax.ShapeDtypeStruct((n, d), F32),
        compiler_params=_cparams(("arbitrary",)),
    )(pos, ys, x1, route, g, b)


def _dispatch_plan(eid, n_items, n_rows):
    p_total = eid.shape[0]
    blk = 128
    assert p_total % blk == 0
    onehot = (eid[:, None] == jnp.arange(N_EXPERTS, dtype=jnp.int32)[None, :]).astype(F32)
    counts = jnp.sum(onehot, axis=0).astype(jnp.int32)
    ohb = onehot.reshape(p_total // blk, blk, N_EXPERTS)
    earlier = (jnp.arange(blk)[:, None] > jnp.arange(blk)[None, :]).astype(F32)
    within = jnp.einsum("ij,bjk->bik", earlier, ohb, precision=lax.Precision.HIGHEST)
    blk_tot = jnp.sum(ohb, axis=1)
    blk_off = jnp.cumsum(blk_tot, axis=0) - blk_tot
    rank = jnp.sum((within + blk_off[:, None, :]) * ohb, axis=2).reshape(p_total).astype(jnp.int32)
    nblk = (counts + MOE_R - 1) // MOE_R
    seg_start = (jnp.cumsum(nblk) - nblk) * MOE_R
    pos = seg_start[eid] + rank
    src = jnp.zeros((n_rows,), jnp.int32).at[pos].set(jnp.arange(p_total, dtype=jnp.int32) // 2)
    items_per_e = (nblk + MOE_MAXB - 1) // MOE_MAXB
    item_end = jnp.cumsum(items_per_e)
    item_start = item_end - items_per_e
    t = jnp.arange(n_items, dtype=jnp.int32)
    e_t = jnp.minimum(jnp.sum((item_end[None, :] <= t[:, None]).astype(jnp.int32), axis=1), N_EXPERTS - 1)
    live = t < item_end[-1]
    local = t - item_start[e_t]
    used = jnp.sum(nblk)
    idle0 = used + (t - item_end[-1]) * MOE_MAXB
    nz_t = jnp.clip(n_rows // MOE_R - idle0, 0, MOE_MAXB)
    nb_t = jnp.where(live, jnp.clip(nblk[e_t] - local * MOE_MAXB, 0, MOE_MAXB), -nz_t)
    row0_t = jnp.where(live, seg_start[e_t] + local * (MOE_MAXB * MOE_R),
                       jnp.minimum(idle0, n_rows // MOE_R - 1) * MOE_R)
    last_e = e_t[jnp.maximum(item_end[-1] - 1, 0)]
    e_t = jnp.where(live, e_t, last_e)
    return pos.astype(jnp.int32), src, e_t.astype(jnp.int32), row0_t.astype(jnp.int32), nb_t.astype(jnp.int32)


def _alibi_slopes():
    return jnp.asarray([2.0 ** (-8.0 * (h + 1) / N_AH) for h in range(N_AH)], dtype=F32)


def kernel(x_prompt, x_sample, state_conv, state_mlstm_C, state_mlstm_n, state_mlstm_m, cache_win_k, cache_win_v, w_in, b_gate, conv_w, conv_b, mh_gain, att_gain, w_out, ln1_g, ln1_b, w_group, b_group, w_router, b_router, w_gate, w_up, w_down, ln2_g, ln2_b):
    bp, tp, d = x_prompt.shape
    bs, ts, _ = x_sample.shape
    assert d == D_MODEL and w_in.shape[0] == 1 and tp % ATT_BLK == 0 and ts >= CONV_W - 1
    w_buf = cache_win_k.shape[2]
    n_p, n_s = bp * tp, bs * ts
    slopes = _alibi_slopes()

    wi = w_in[0]
    g0 = 4 * D_MLSTM
    g1 = g0 + 2 * N_MH
    w_pack = jnp.concatenate(
        [wi[:, :g0], wi[:, g1:], wi[:, g0:g1], jnp.zeros((d, LANE - 2 * N_MH), F32)], axis=1).astype(BF16)

    xp2 = x_prompt.reshape(n_p, d)
    xs2 = x_sample.reshape(n_s, d)
    proj_p = _in_proj(xp2, xs2, w_pack, 512)
    proj_s = proj_p[n_p:n_p + n_s]

    cw = conv_w[0]
    cb = conv_b[0][None, :]
    mh_g = mh_gain[0][None, :]
    att_g = att_gain[0][None, :]
    bg = b_gate[0]

    def gates_time_major(proj, batch, seq, pad_to):
        gt = proj[:batch * seq, COL_G:COL_G + 2 * N_MH].reshape(batch, seq, 2, N_MH).transpose(0, 3, 2, 1)
        if pad_to > seq:
            gt = jnp.pad(gt, ((0, 0), (0, 0), (0, 0), (0, pad_to - seq)))
        return gt

    lc_p = 256
    hm_p, c_p, n_pp, m_p = _mlstm(
        proj_p, gates_time_major(proj_p, bp, tp, tp), jnp.zeros((bp, CONV_W - 1, 2 * D_MLSTM), F32), cw, cb,
        jnp.zeros((bp, N_MH, E_MH, E_MH), F32), jnp.zeros((bp, N_MH, 1, E_MH), F32),
        jnp.zeros((bp, N_MH, 1, 1), F32), mh_g, bg, batch=bp, seq=tp, lb=lc_p, lc=lc_p, hb=2, out_dtype=BF16)
    ha_p = _attn_prompt(proj_p, att_g, slopes, batch=bp, seq=tp)

    lc_s = 16
    hm_s, c_s, n_ss, m_s = _mlstm(
        proj_s, gates_time_major(proj_s, bs, ts, lc_s), state_conv[0], cw, cb,
        state_mlstm_C[0], state_mlstm_n[0][:, :, None, :], state_mlstm_m[0][:, :, None, None],
        mh_g, bg, batch=bs, seq=ts, lb=ts, lc=lc_s, hb=N_MH, out_dtype=F32)
    new_rows = lambda col: proj_s[:, col:col + N_AH * E_AH].reshape(bs, ts, N_AH, E_AH)
    ha_s, wk_s, wv_s = _attn_decode(
        new_rows(COL_QA), new_rows(COL_KA), new_rows(COL_VA), cache_win_k, cache_win_v,
        att_gain[0].reshape(N_AH, E_AH), jnp.broadcast_to(slopes[:, None], (N_AH, E_AH)), batch=bs, s_new=ts)
    ha_s = ha_s.reshape(n_s, N_AH * E_AH)

    n_all = n_p + n_s
    wo = w_out[0].astype(BF16)
    w_r = jnp.concatenate(
        [w_group[0], w_router[0].transpose(1, 0, 2).reshape(d, N_EXPERTS),
         jnp.zeros((d, ROUTE_W - N_GROUPS - N_EXPERTS), F32)], axis=1)
    b_r = jnp.concatenate(
        [b_group[0], b_router[0].reshape(N_EXPERTS), jnp.zeros((ROUTE_W - N_GROUPS - N_EXPERTS,), F32)])[None, :]
    wr_hi = w_r.astype(BF16)
    wr_lo = (w_r - wr_hi.astype(F32)).astype(BF16)
    ln1 = (ln1_g[0][None, :], ln1_b[0][None, :])
    x1, x1p, route = _outproj(hm_p, ha_p, xp2, hm_s, ha_s, xs2, wo, *ln1, wr_hi, wr_lo, b_r)

    p_total = 2 * n_all
    n_rows = ((p_total + N_EXPERTS * (MOE_R - 1)) // MOE_R + 1) * MOE_R
    n_items = N_EXPERTS + n_rows // (MOE_R * MOE_MAXB)
    eid = route[:n_all, 0:2].astype(jnp.int32).reshape(p_total)
    pos, src, item_e, item_row0, item_nb = _dispatch_plan(eid, n_items, n_rows)
    ys = _moe(item_e, item_row0, item_nb, src, x1p, w_gate[0], w_up[0], w_down[0], n_items, n_rows)
    ln2 = (ln2_g[0][None, :], ln2_b[0][None, :])
    y_p = _combine(pos[:2 * n_p], ys, x1, route, *ln2, row0=0, n=n_p).reshape(bp, tp, d)
    y_s = _combine(pos[2 * n_p:], ys, x1, route, *ln2, row0=n_p, n=n_s).reshape(bs, ts, d)

    def tail_rows(proj, batch, seq, col, width, rows):
        return jnp.stack([proj[(b + 1) * seq - rows:(b + 1) * seq, col:col + width] for b in range(batch)])

    win = min(w_buf, tp)
    p_conv = tail_rows(proj_p, bp, tp, COL_QM, 2 * D_MLSTM, CONV_W - 1)[None]
    p_wk = tail_rows(proj_p, bp, tp, COL_KA, N_AH * E_AH, win).reshape(1, bp, win, N_AH, E_AH)
    p_wv = tail_rows(proj_p, bp, tp, COL_VA, N_AH * E_AH, win).reshape(1, bp, win, N_AH, E_AH)
    s_conv = tail_rows(proj_s, bs, ts, COL_QM, 2 * D_MLSTM, CONV_W - 1)[None]
    return (y_p, y_s,
            p_conv, c_p[None], n_pp[:, :, 0, :][None], m_p[:, :, 0, 0][None], p_wk, p_wv,
            s_conv, c_s[None], n_ss[:, :, 0, :][None], m_s[:, :, 0, 0][None], wk_s, wv_s)
```

```python
import functools
import math

import jax
import jax.numpy as jnp
from jax import lax
from jax.experimental import pallas as pl
from jax.experimental.pallas import tpu as pltpu

F32 = jnp.float32
BF16 = jnp.bfloat16
NEG_INF = float("-inf")

D_MODEL = 2048
D_MLSTM = 1024
N_MH = 4
E_MH = 256
N_AH = 8
E_AH = 128
DILATED_CONFIGS = ((128, 1), (512, 4), (2048, 16))
N_BACK = 128
BAND = 128
CONV_W = 4
N_GROUPS = 4
E_PER_GROUP = 8
N_EXPERTS = 32
D_FF = 1024
EPS = 1e-5
ALPHA = 2.0 ** 0.25

LANE = 128
SUBLANE = 8

COL_QM, COL_KM, COL_VM, COL_OM = 0, 1024, 2048, 3072
COL_QA, COL_KA, COL_VA, COL_G = 4096, 5120, 6144, 7168
N_PROJ = 7296
PROJ_TN = 2432

VMEM_LIMIT = 56 * 1024 * 1024


def _cparams(sem):
    return pltpu.CompilerParams(dimension_semantics=sem, vmem_limit_bytes=VMEM_LIMIT)


def _proj_body(x_ref, w_ref, o_ref):
    o_ref[...] = jnp.dot(x_ref[...].astype(BF16), w_ref[...], preferred_element_type=F32)


def _in_proj(x, w, tm):
    n = x.shape[0]
    return pl.pallas_call(
        _proj_body,
        grid=(N_PROJ // PROJ_TN, n // tm),
        in_specs=[pl.BlockSpec((tm, D_MODEL), lambda j, i: (i, 0)),
                  pl.BlockSpec((D_MODEL, PROJ_TN), lambda j, i: (0, j))],
        out_specs=pl.BlockSpec((tm, PROJ_TN), lambda j, i: (i, j)),
        out_shape=jax.ShapeDtypeStruct((n, N_PROJ), F32),
        compiler_params=_cparams(("arbitrary", "arbitrary")),
    )(x, w)


def _mlstm_body(bg_ref, xq_ref, xk_ref, v_ref, om_ref, g_ref, hq_ref, hk_ref, cwq_ref, cwk_ref,
                cbq_ref, cbk_ref, c0_ref, n0_ref, m0_ref, gain_ref,
                hm_ref, cout_ref, nout_ref, mout_ref,
                caug, m_s, uq, uk, vbuf, *, lb, lc, t_valid, hb):
    hg = pl.program_id(1)
    c = pl.program_id(2)
    nc = pl.num_programs(2)
    e = E_MH
    heads = range(hb)
    cols = lambda i: slice(i * e, (i + 1) * e)

    row_e = lax.broadcasted_iota(jnp.int32, (e, e), 0)
    col_e = lax.broadcasted_iota(jnp.int32, (e, e), 1)
    eye_e = row_e == col_e

    @pl.when(c == 0)
    def _init():
        for i in heads:
            caug[i, :, 0:e] = c0_ref[i]
            ncol = jnp.sum(jnp.where(eye_e, n0_ref[i], 0.0), axis=1, keepdims=True)
            caug[i, :, e:e + LANE] = jnp.broadcast_to(ncol, (e, LANE))
            m_s[i] = m0_ref[i]
        uq[...] = jnp.zeros(uq.shape, F32)
        uk[...] = jnp.zeros(uk.shape, F32)
        uq[pl.ds(SUBLANE - (CONV_W - 1), CONV_W - 1), :] = hq_ref[...]
        uk[pl.ds(SUBLANE - (CONV_W - 1), CONV_W - 1), :] = hk_ref[...]
        if lb != lc:
            vbuf[...] = jnp.zeros(vbuf.shape, F32)

    uq[pl.ds(SUBLANE, lb), :] = xq_ref[...]
    uk[pl.ds(SUBLANE, lb), :] = xk_ref[...]

    def conv_silu(u, cw_ref, cb_ref):
        acc = cb_ref[...]
        for j in range(CONV_W):
            acc = acc + u[pl.ds(SUBLANE - (CONV_W - 1) + j, lc), :] * cw_ref[j:j + 1, :]
        return acc * jax.nn.sigmoid(acc)

    q_all = conv_silu(uq, cwq_ref, cbq_ref)
    k_all = conv_silu(uk, cwk_ref, cbk_ref) * (e ** -0.5)
    if lb != lc:
        vbuf[pl.ds(0, lb), :] = v_ref[...]
        v_all = vbuf[...]
    else:
        v_all = v_ref[...]

    tq = uq[pl.ds(lc + SUBLANE - (CONV_W - 1), CONV_W - 1), :]
    tk = uk[pl.ds(lc + SUBLANE - (CONV_W - 1), CONV_W - 1), :]
    uq[pl.ds(SUBLANE - (CONV_W - 1), CONV_W - 1), :] = tq
    uk[pl.ds(SUBLANE - (CONV_W - 1), CONV_W - 1), :] = tk

    lane_t = lax.broadcasted_iota(jnp.int32, (1, lc), 1)
    valid = (lane_t + c * lc) < t_valid
    row_l = lax.broadcasted_iota(jnp.int32, (lc, lc), 0)
    col_l = lax.broadcasted_iota(jnp.int32, (lc, lc), 1)
    causal = col_l <= row_l
    eye_l = col_l == row_l
    nt = (((1,), (1,)), ((), ()))
    tn = (((0,), (0,)), ((), ()))

    st = []
    for i in heads:
        h = hg * hb + i
        gi = g_ref[i, 0:1, :] + bg_ref[h]
        gf = g_ref[i, 1:2, :] + bg_ref[N_MH + h]
        lf = -(jnp.maximum(-gf, 0.0) + jnp.log1p(jnp.exp(-jnp.abs(gf))))
        ig_row = jnp.where(valid, gi, NEG_INF)
        lf_row = jnp.where(valid, lf, 0.0)
        b_col = jnp.sum(jnp.where(causal, lf_row, 0.0), axis=1, keepdims=True)
        b_row = jnp.sum(jnp.where(eye_l, b_col, 0.0), axis=0, keepdims=True)
        m_prev = m_s[i]
        dlog = jnp.where(causal, b_col - b_row + ig_row, NEG_INF)
        a_col = b_col + m_prev
        m_t = jnp.maximum(a_col, jnp.max(dlog, axis=1, keepdims=True))
        st.append(dict(ig_row=ig_row, b_row=b_row, m_prev=m_prev, m_t=m_t,
                       w_intra=jnp.exp(dlog - m_t), w_inter=jnp.exp(a_col - m_t),
                       qb=q_all[:, cols(i)].astype(BF16), kb=k_all[:, cols(i)].astype(BF16),
                       vb=v_all[:, cols(i)].astype(BF16)))

    for i in heads:
        d = st[i]
        d["s"] = lax.dot_general(d["qb"], d["kb"], nt, preferred_element_type=F32) * d["w_intra"]
        d["qc"] = jnp.dot(d["qb"], caug[i].astype(BF16), preferred_element_type=F32)

    for i in heads:
        d = st[i]
        num = jnp.dot(d["s"].astype(BF16), d["vb"], preferred_element_type=F32) + d["w_inter"] * d["qc"][:, 0:e]
        den = jnp.sum(d["s"], axis=1, keepdims=True) + d["w_inter"] * d["qc"][:, e:e + 1]
        hh = num / jnp.maximum(jnp.abs(den), jnp.exp(-d["m_t"]))
        hh = hh - jnp.mean(hh, axis=1, keepdims=True)
        hh = hh * lax.rsqrt(jnp.mean(hh * hh, axis=1, keepdims=True) + EPS)
        if lb != lc:
            hh = hh[0:lb, :]
        hm = hh * gain_ref[:, cols(i)] * jax.nn.sigmoid(om_ref[:, cols(i)])
        hm_ref[:, cols(i)] = hm.astype(hm_ref.dtype)

    for i in heads:
        d = st[i]
        b_last = d["b_row"][:, lc - 1:lc]
        logw_row = b_last - d["b_row"] + d["ig_row"]
        m_new = jnp.maximum(b_last + d["m_prev"], jnp.max(logw_row, axis=1, keepdims=True))
        w_row = jnp.exp(logw_row - m_new)
        w_col = jnp.sum(jnp.where(eye_l, w_row, 0.0), axis=1, keepdims=True)
        decay = jnp.exp(b_last + d["m_prev"] - m_new)
        kw = (k_all[:, cols(i)] * w_col).astype(BF16)
        vaug = jnp.concatenate([d["vb"], jnp.ones((lc, LANE), BF16)], axis=1)
        upd = lax.dot_general(kw, vaug, tn, preferred_element_type=F32)
        caug[i] = decay * caug[i] + upd
        m_s[i] = m_new

    @pl.when(c == nc - 1)
    def _fin():
        for i in heads:
            cout_ref[i] = caug[i, :, 0:e]
            nout_ref[i] = jnp.sum(jnp.where(eye_e, caug[i, :, e:e + 1], 0.0), axis=0, keepdims=True)
            mout_ref[i] = jnp.broadcast_to(m_s[i], (1, LANE))


def _mlstm(xp, gates_t, hist, conv_w, conv_b, c0, n0, m0, gain, b_gate, *, batch, seq, lb, lc, hb, out_dtype):
    nc = seq // lb
    e = E_MH
    w = hb * e
    ng = N_MH // hb
    assert N_MH % hb == 0
    rb = lambda b, h, c, bg: b * nc + c
    qcol, kcol, vcol, ocol = COL_QM // w, COL_KM // w, COL_VM // w, COL_OM // w
    body = functools.partial(_mlstm_body, lb=lb, lc=lc, t_valid=seq if lb == lc else lb, hb=hb)
    grid_spec = pltpu.PrefetchScalarGridSpec(
        num_scalar_prefetch=1,
        grid=(batch, ng, nc),
        in_specs=[
            pl.BlockSpec((lb, w), lambda b, h, c, bg: (rb(b, h, c, bg), qcol + h)),
            pl.BlockSpec((lb, w), lambda b, h, c, bg: (rb(b, h, c, bg), kcol + h)),
            pl.BlockSpec((lb, w), lambda b, h, c, bg: (rb(b, h, c, bg), vcol + h)),
            pl.BlockSpec((lb, w), lambda b, h, c, bg: (rb(b, h, c, bg), ocol + h)),
            pl.BlockSpec((None, hb, 2, lc), lambda b, h, c, bg: (b, h, 0, c)),
            pl.BlockSpec((None, CONV_W - 1, w), lambda b, h, c, bg: (b, 0, h)),
            pl.BlockSpec((None, CONV_W - 1, w), lambda b, h, c, bg: (b, 0, ng + h)),
            pl.BlockSpec((CONV_W, w), lambda b, h, c, bg: (0, h)),
            pl.BlockSpec((CONV_W, w), lambda b, h, c, bg: (0, ng + h)),
            pl.BlockSpec((1, w), lambda b, h, c, bg: (0, h)),
            pl.BlockSpec((1, w), lambda b, h, c, bg: (0, ng + h)),
            pl.BlockSpec((None, hb, e, e), lambda b, h, c, bg: (b, h, 0, 0)),
            pl.BlockSpec((None, hb, 1, e), lambda b, h, c, bg: (b, h, 0, 0)),
            pl.BlockSpec((None, hb, 1, 1), lambda b, h, c, bg: (b, h, 0, 0)),
            pl.BlockSpec((1, w), lambda b, h, c, bg: (0, h)),
        ],
        out_specs=[
            pl.BlockSpec((lb, w), lambda b, h, c, bg: (rb(b, h, c, bg), h)),
            pl.BlockSpec((None, hb, e, e), lambda b, h, c, bg: (b, h, 0, 0)),
            pl.BlockSpec((None, hb, 1, e), lambda b, h, c, bg: (b, h, 0, 0)),
            pl.BlockSpec((None, hb, 1, LANE), lambda b, h, c, bg: (b, h, 0, 0)),
        ],
        scratch_shapes=[
            pltpu.VMEM((hb, e, e + LANE), F32),
            pltpu.VMEM((hb, 1, 1), F32),
            pltpu.VMEM((lc + 2 * SUBLANE, w), F32),
            pltpu.VMEM((lc + 2 * SUBLANE, w), F32),
            pltpu.VMEM((lc, w), F32),
        ],
    )
    return pl.pallas_call(
        body,
        grid_spec=grid_spec,
        out_shape=[
            jax.ShapeDtypeStruct((batch * seq, D_MLSTM), out_dtype),
            jax.ShapeDtypeStruct((batch, N_MH, e, e), F32),
            jax.ShapeDtypeStruct((batch, N_MH, 1, e), F32),
            jax.ShapeDtypeStruct((batch, N_MH, 1, LANE), F32),
        ],
        compiler_params=_cparams(("arbitrary", "arbitrary", "arbitrary")),
    )(b_gate, xp, xp, xp, xp, gates_t, hist, hist, conv_w, conv_w, conv_b, conv_b, c0, n0, m0, gain)


ATT_BLK = 2048
ATT_GROUP = 16


def _attn_body(sl_ref, q_ref, kc_ref, vc_ref, kp_ref, vp_ref, gain_ref, o_ref, o_s, m_s, l_s):
    blk = pl.program_id(1)
    h = pl.program_id(2)
    slope = sl_ref[h]
    scale = E_AH ** -0.5
    qi = lax.broadcasted_iota(jnp.int32, (BAND, BAND), 0)
    ci = lax.broadcasted_iota(jnp.int32, (BAND, BAND), 1)
    dist_prev = (BAND + qi - ci).astype(F32)
    dist_cur = (qi - ci).astype(F32)
    ok_prev = ci >= qi
    ok_cur = ci <= qi

    def run_units(cfg, d, specs):
        sd = slope * float(d)
        bm_cur = jnp.where(ok_cur, -sd * dist_cur, NEG_INF)
        nt = (((1,), (1,)), ((), ()))
        scores = []
        for prev_pen, q_sl, kp_src, kc_src, _, _ in specs:
            bm_prev = jnp.where(ok_prev, prev_pen - sd * dist_prev, NEG_INF)
            qv = q_ref[q_sl, :].astype(BF16)
            kp = kp_src[0][kp_src[1], :].astype(BF16)
            kc = kc_src[0][kc_src[1], :].astype(BF16)
            s_p = lax.dot_general(qv, kp, nt, preferred_element_type=F32) * scale + bm_prev
            s_c = lax.dot_general(qv, kc, nt, preferred_element_type=F32) * scale + bm_cur
            scores.append((s_p, s_c))
        probs = []
        for s_p, s_c in scores:
            m = jnp.max(jnp.maximum(s_p, s_c), axis=1, keepdims=True)
            p_p = jnp.exp(s_p - m)
            p_c = jnp.exp(s_c - m)
            l = jnp.sum(p_p + p_c, axis=1, keepdims=True)
            probs.append((m, l, p_p.astype(BF16), p_c.astype(BF16)))
        for (_, q_sl, _, _, vp_src, vc_src), (m, l, p_p, p_c) in zip(specs, probs):
            vp = vp_src[0][vp_src[1], :].astype(BF16)
            vc = vc_src[0][vc_src[1], :].astype(BF16)
            o = jnp.dot(p_p, vp, preferred_element_type=F32) + jnp.dot(p_c, vc, preferred_element_type=F32)
            o_s[cfg, q_sl, :] = o
            m_s[cfg, q_sl, :] = jnp.broadcast_to(m, (BAND, E_AH))
            l_s[cfg, q_sl, :] = jnp.broadcast_to(l, (BAND, E_AH))

    first_pen = jnp.where(blk == 0, NEG_INF, 0.0)

    for cfg, (win, d) in enumerate(DILATED_CONFIGS):
        span = BAND * d
        n_u = ATT_BLK // span

        def sl(r, u, _d=d, _span=span):
            if _d == 1:
                if isinstance(u, int):
                    return pl.ds(u * _span, BAND)
                return pl.ds(pl.multiple_of(u * _span, BAND), BAND)
            return pl.ds(r + u * _span, BAND, stride=_d)

        def head_unit(r, d=d, span=span, sl=sl):
            cur = sl(r, 0)
            prv = sl(r + ATT_BLK - span, 0) if d > 1 else pl.ds(ATT_BLK - span, BAND)
            return (first_pen, cur, (kp_ref, prv), (kc_ref, cur), (vp_ref, prv), (vc_ref, cur))

        def tail_unit(r, u, sl=sl):
            cur = sl(r, u)
            prv = sl(r, u - 1)
            return (0.0, cur, (kc_ref, prv), (kc_ref, cur), (vc_ref, prv), (vc_ref, cur))

        specs = [head_unit(r) if u == 0 else tail_unit(r, u) for u in range(n_u) for r in range(d)]
        for g in range(0, len(specs), ATT_GROUP):
            run_units(cfg, d, specs[g:g + ATT_GROUP])

    rows = 256

    def merge(i, _):
        rs = pl.ds(pl.multiple_of(i * rows, rows), rows)
        m0, m1, m2 = m_s[0, rs, :], m_s[1, rs, :], m_s[2, rs, :]
        m_all = jnp.maximum(jnp.maximum(m0, m1), m2)
        e0, e1, e2 = jnp.exp(m0 - m_all), jnp.exp(m1 - m_all), jnp.exp(m2 - m_all)
        num = e0 * o_s[0, rs, :] + e1 * o_s[1, rs, :] + e2 * o_s[2, rs, :]
        den = e0 * l_s[0, rs, :] + e1 * l_s[1, rs, :] + e2 * l_s[2, rs, :]
        ha = num / den
        ha = ha * lax.rsqrt(jnp.mean(ha * ha, axis=1, keepdims=True) + EPS)
        o_ref[rs, :] = (ha * gain_ref[...]).astype(o_ref.dtype)
        return 0

    lax.fori_loop(0, ATT_BLK // rows, merge, 0)


def _attn_prompt(xp, gain, slopes, *, batch, seq):
    nb = seq // ATT_BLK
    qc, kc, vc = COL_QA // E_AH, COL_KA // E_AH, COL_VA // E_AH
    cur = lambda b, i, h, s: b * nb + i
    prev = lambda b, i, h, s: b * nb + jnp.maximum(i - 1, 0)
    grid_spec = pltpu.PrefetchScalarGridSpec(
        num_scalar_prefetch=1,
        grid=(batch, nb, N_AH),
        in_specs=[
            pl.BlockSpec((ATT_BLK, E_AH), lambda b, i, h, s: (cur(b, i, h, s), qc + h)),
            pl.BlockSpec((ATT_BLK, E_AH), lambda b, i, h, s: (cur(b, i, h, s), kc + h)),
            pl.BlockSpec((ATT_BLK, E_AH), lambda b, i, h, s: (cur(b, i, h, s), vc + h)),
            pl.BlockSpec((ATT_BLK, E_AH), lambda b, i, h, s: (prev(b, i, h, s), kc + h)),
            pl.BlockSpec((ATT_BLK, E_AH), lambda b, i, h, s: (prev(b, i, h, s), vc + h)),
            pl.BlockSpec((1, E_AH), lambda b, i, h, s: (0, h)),
        ],
        out_specs=pl.BlockSpec((ATT_BLK, E_AH), lambda b, i, h, s: (cur(b, i, h, s), h)),
        scratch_shapes=[pltpu.VMEM((3, ATT_BLK, E_AH), F32)] * 3,
    )
    return pl.pallas_call(
        _attn_body,
        grid_spec=grid_spec,
        out_shape=jax.ShapeDtypeStruct((batch * seq, N_AH * E_AH), BF16),
        compiler_params=_cparams(("arbitrary", "arbitrary", "arbitrary")),
    )(slopes, xp, xp, xp, xp, xp, gain)


DEC_NEAR = 8


def _near_multiplicity():
    return [sum(1 for win, d in DILATED_CONFIGS if dist % d == 0 and dist <= win) for dist in range(DEC_NEAR)]


def _decode_body(q_ref, kn_ref, vn_ref, kc_ref, vc_ref, slope_ref, gain_ref,
                 ha_ref, ko_hbm, vo_hbm, ktail, vtail, sem, *, w_buf, s_new):
    b = pl.program_id(0)
    keep = w_buf - s_new
    copies = [
        pltpu.make_async_copy(kc_ref.at[0, 0, pl.ds(s_new, keep)], ko_hbm.at[0, b, pl.ds(0, keep)], sem.at[0]),
        pltpu.make_async_copy(vc_ref.at[0, 0, pl.ds(s_new, keep)], vo_hbm.at[0, b, pl.ds(0, keep)], sem.at[1]),
        pltpu.make_async_copy(kn_ref.at[0], ko_hbm.at[0, b, pl.ds(keep, s_new)], sem.at[2]),
        pltpu.make_async_copy(vn_ref.at[0], vo_hbm.at[0, b, pl.ds(keep, s_new)], sem.at[3]),
    ]
    for cp in copies:
        cp.start()

    ktail[pl.ds(0, DEC_NEAR)] = kc_ref[0, 0, pl.ds(w_buf - DEC_NEAR, DEC_NEAR)]
    vtail[pl.ds(0, DEC_NEAR)] = vc_ref[0, 0, pl.ds(w_buf - DEC_NEAR, DEC_NEAR)]
    ktail[pl.ds(DEC_NEAR, s_new)] = kn_ref[0]
    vtail[pl.ds(DEC_NEAR, s_new)] = vn_ref[0]

    scale = E_AH ** -0.5
    slope = slope_ref[:, 0:1]
    near_i = lax.broadcasted_iota(jnp.int32, (DEC_NEAR, 1, 1), 0)
    near_dist = (DEC_NEAR - 1 - near_i).astype(F32)
    near_mult = jnp.zeros((DEC_NEAR, 1, 1), F32)
    for dist, c in enumerate(_near_multiplicity()):
        near_mult = jnp.where(near_i == DEC_NEAR - 1 - dist, float(c), near_mult)

    def far_part(s, win, d):
        n = N_BACK - (DEC_NEAR - 1) // d
        sl = pl.ds(w_buf + s - win, n, stride=d) if d > 1 else pl.ds(w_buf + s - win, n)
        i = lax.broadcasted_iota(jnp.int32, (n, 1, 1), 0)
        return sl, ((N_BACK - i) * d).astype(F32)

    ones = jnp.ones((E_AH, E_AH), BF16)

    def row_dots(kk, q):
        n = kk.shape[0]
        prod = (kk * q).astype(BF16).reshape(n * N_AH, E_AH)
        return jnp.dot(prod, ones, preferred_element_type=F32).reshape(n, N_AH, E_AH)

    def one_query(s, _):
        q = q_ref[0, s]
        parts = []
        for win, d in DILATED_CONFIGS:
            sl, dist = far_part(s, win, d)
            sc = row_dots(kc_ref[0, 0, sl], q) * scale - slope * dist
            parts.append((sc, None, lambda sl=sl: vc_ref[0, 0, sl]))
        nsl = pl.ds(s + 1, DEC_NEAR)
        sc = row_dots(ktail[nsl], q) * scale - slope * near_dist
        parts.append((sc, near_mult, lambda: vtail[nsl]))
        m = functools.reduce(jnp.maximum, [jnp.max(p[0], axis=0, keepdims=True) for p in parts])
        den = jnp.zeros((1, N_AH, E_AH), F32)
        o = jnp.zeros((1, N_AH, E_AH), F32)
        for sc, mu, load_v in parts:
            p = jnp.exp(sc - m)
            if mu is not None:
                p = p * mu
            den = den + jnp.sum(p, axis=0, keepdims=True)
            o = o + jnp.sum(p * load_v(), axis=0, keepdims=True)
        o = (o / den)[0]
        o = o * lax.rsqrt(jnp.mean(o * o, axis=-1, keepdims=True) + EPS)
        ha_ref[0, s] = o * gain_ref[...]
        return 0

    lax.fori_loop(0, s_new, one_query, 0)
    for cp in copies:
        cp.wait()


def _attn_decode(q3, kn3, vn3, cache_k, cache_v, gain, slopes, *, batch, s_new):
    w_buf = cache_k.shape[2]
    assert w_buf >= max(w for w, _ in DILATED_CONFIGS) and s_new <= DEC_NEAR
    assert all(w // d == N_BACK and d & (d - 1) == 0 for w, d in DILATED_CONFIGS)
    body = functools.partial(_decode_body, w_buf=w_buf, s_new=s_new)
    new_spec = pl.BlockSpec((1, s_new, N_AH, E_AH), lambda b: (b, 0, 0, 0))
    cache_spec = pl.BlockSpec((1, 1, w_buf, N_AH, E_AH), lambda b: (0, b, 0, 0, 0))
    tile_spec = pl.BlockSpec((N_AH, E_AH), lambda b: (0, 0))
    return pl.pallas_call(
        body,
        grid=(batch,),
        in_specs=[new_spec, new_spec, new_spec, cache_spec, cache_spec, tile_spec, tile_spec],
        out_specs=[new_spec, pl.BlockSpec(memory_space=pl.ANY), pl.BlockSpec(memory_space=pl.ANY)],
        out_shape=[
            jax.ShapeDtypeStruct((batch, s_new, N_AH, E_AH), F32),
            jax.ShapeDtypeStruct(cache_k.shape, F32),
            jax.ShapeDtypeStruct(cache_v.shape, F32),
        ],
        scratch_shapes=[pltpu.VMEM((DEC_NEAR + s_new, N_AH, E_AH), F32)] * 2 + [pltpu.SemaphoreType.DMA((4,))],
        compiler_params=_cparams(("arbitrary",)),
    )(q3, kn3, vn3, cache_k, cache_v, slopes, gain)


OP_TM = 512
ROUTE_W = LANE
PACK_ROWS = D_MODEL // LANE


def _layer_norm(z, g, b):
    mu = jnp.mean(z, axis=1, keepdims=True)
    zc = z - mu
    var = jnp.mean(zc * zc, axis=1, keepdims=True)
    return zc * lax.rsqrt(var + EPS) * g + b


def _outproj_body(hm_ref, ha_ref, x_ref, wm_ref, wa_ref, g_ref, b_ref, wrh_ref, wrl_ref, br_ref,
                  x1_ref, x1p_ref, route_ref):
    y = (jnp.dot(hm_ref[...].astype(BF16), wm_ref[...], preferred_element_type=F32)
         + jnp.dot(ha_ref[...].astype(BF16), wa_ref[...], preferred_element_type=F32))
    x1 = _layer_norm(ALPHA * x_ref[...] + y, g_ref[...], b_ref[...])
    x1_ref[...] = x1
    tm = x1.shape[0]
    for j in range(PACK_ROWS):
        x1p_ref[pl.ds(j, tm, stride=PACK_ROWS), :] = x1[:, j * LANE:(j + 1) * LANE]
    x1_hi = x1.astype(BF16)
    x1_lo = (x1 - x1_hi.astype(F32)).astype(BF16)
    logits = (jnp.dot(x1_hi, wrh_ref[...], preferred_element_type=F32)
              + jnp.dot(x1_lo, wrh_ref[...], preferred_element_type=F32)
              + jnp.dot(x1_hi, wrl_ref[...], preferred_element_type=F32)) + br_ref[...]
    lane = lax.broadcasted_iota(jnp.int32, (tm, ROUTE_W), 1)
    lane_f = lane.astype(F32)
    big = float(ROUTE_W)
    gl = jnp.where(lane < N_GROUPS, logits, NEG_INF)
    gmax = jnp.max(gl, axis=1, keepdims=True)
    g_w = 1.0 / jnp.sum(jnp.exp(gl - gmax), axis=1, keepdims=True)
    g_idx = jnp.min(jnp.where(gl == gmax, lane_f, big), axis=1, keepdims=True)
    lo = N_GROUPS + E_PER_GROUP * g_idx
    el = jnp.where(jnp.logical_and(lane_f >= lo, lane_f < lo + E_PER_GROUP), logits, NEG_INF)
    v1 = jnp.max(el, axis=1, keepdims=True)
    i1 = jnp.min(jnp.where(el == v1, lane_f, big), axis=1, keepdims=True)
    el2 = jnp.where(lane_f == i1, NEG_INF, el)
    v2 = jnp.max(el2, axis=1, keepdims=True)
    i2 = jnp.min(jnp.where(el2 == v2, lane_f, big), axis=1, keepdims=True)
    e2 = jnp.exp(v2 - v1)
    w1 = g_w / (1.0 + e2)
    w2 = g_w * e2 / (1.0 + e2)
    route = jnp.where(lane == 0, i1 - N_GROUPS,
                      jnp.where(lane == 1, i2 - N_GROUPS,
                                jnp.where(lane == 2, w1, jnp.where(lane == 3, w2, 0.0))))
    route_ref[...] = route


def _outproj_two_groups(hmp_ref, hap_ref, xp_ref, hms_ref, has_ref, xs_ref, *rest, steps_p):
    i = pl.program_id(0)

    @pl.when(i < steps_p)
    def _():
        _outproj_body(hmp_ref, hap_ref, xp_ref, *rest)

    @pl.when(i >= steps_p)
    def _():
        _outproj_body(hms_ref, has_ref, xs_ref, *rest)


def _outproj(hm_p, ha_p, x_p, hm_s, ha_s, x_s, wo, g, b, wr_hi, wr_lo, br):
    tm = OP_TM
    n_p, n_s = x_p.shape[0], x_s.shape[0]
    assert n_p % tm == 0 and n_s <= tm
    steps_p = n_p // tm
    n = n_p + tm
    pad = lambda a: jnp.pad(a, ((0, tm - n_s), (0, 0)))
    hm_s, ha_s, x_s = pad(hm_s), pad(ha_s), pad(x_s)
    row = lambda i: (i, 0)
    prow = lambda i: (jnp.minimum(i, steps_p - 1), 0)
    fixed = lambda i: (0, 0)
    once = pl.Buffered(1)
    return pl.pallas_call(
        functools.partial(_outproj_two_groups, steps_p=steps_p),
        grid=(steps_p + 1,),
        in_specs=[
            pl.BlockSpec((tm, D_MLSTM), prow),
            pl.BlockSpec((tm, N_AH * E_AH), prow),
            pl.BlockSpec((tm, D_MODEL), prow),
            pl.BlockSpec((tm, D_MLSTM), fixed),
            pl.BlockSpec((tm, N_AH * E_AH), fixed),
            pl.BlockSpec((tm, D_MODEL), fixed),
            pl.BlockSpec((D_MLSTM, D_MODEL), fixed, pipeline_mode=once),
            pl.BlockSpec((N_AH * E_AH, D_MODEL), lambda i: (1, 0), pipeline_mode=once),
            pl.BlockSpec((1, D_MODEL), fixed),
            pl.BlockSpec((1, D_MODEL), fixed),
            pl.BlockSpec((D_MODEL, ROUTE_W), fixed),
            pl.BlockSpec((D_MODEL, ROUTE_W), fixed),
            pl.BlockSpec((1, ROUTE_W), fixed),
        ],
        out_specs=[pl.BlockSpec((tm, D_MODEL), row), pl.BlockSpec((tm * PACK_ROWS, LANE), row),
                   pl.BlockSpec((tm, ROUTE_W), row)],
        out_shape=[jax.ShapeDtypeStruct((n, D_MODEL), F32), jax.ShapeDtypeStruct((n * PACK_ROWS, LANE), F32),
                   jax.ShapeDtypeStruct((n, ROUTE_W), F32)],
        compiler_params=_cparams(("arbitrary",)),
    )(hm_p, ha_p, x_p, hm_s, ha_s, x_s, wo, wo, g, b, wr_hi, wr_lo, br)


MOE_R = 128
MOE_MAXB = 6
MOE_FC = 256
MOE_NC = D_FF // MOE_FC
MOE_ISSUE_UNROLL = 32


def _moe_body(ex_ref, row0_ref, nb_ref, src_ref, x1p_hbm, wg_ref, wu_ref, wd_ref, ys_hbm,
              ubuf, xbuf, acc, sem_in, sem_out):
    t = pl.program_id(0)
    c = pl.program_id(1)
    n_items = pl.num_programs(0)
    slot = t % 2
    nb = nb_ref[t]

    def rows(base, r):
        return pl.ds(pl.multiple_of(base + r * MOE_R, MOE_R), MOE_R)

    def tile_rows(i):
        return pl.ds(pl.multiple_of(i * PACK_ROWS, PACK_ROWS), PACK_ROWS)

    def out_copy(tt, sl, r):
        return pltpu.make_async_copy(acc.at[sl, rows(0, r), :], ys_hbm.at[rows(row0_ref[tt], r), :], sem_out.at[sl])

    def for_blocks(n, fn):
        def body(r, _):
            fn(r)
            return 0

        lax.fori_loop(0, n, body, 0)

    def n_in(tt):
        return jnp.maximum(nb_ref[tt], 0)

    def n_out(tt):
        return jnp.abs(nb_ref[tt])

    def start_gather(tt, sl):
        base = row0_ref[tt]

        def group(gi):
            for k in range(MOE_ISSUE_UNROLL):
                i = gi * MOE_ISSUE_UNROLL + k
                tok = src_ref[base + i]
                pltpu.make_async_copy(x1p_hbm.at[tile_rows(tok), :], ubuf.at[sl, tile_rows(i), :], sem_in.at[sl]).start()

        for_blocks(n_in(tt) * (MOE_R // MOE_ISSUE_UNROLL), group)

    def wait_gather(tt, sl):
        def block(r):
            span = pl.ds(pl.multiple_of(r * (MOE_R * PACK_ROWS), MOE_R * PACK_ROWS), MOE_R * PACK_ROWS)
            pltpu.make_async_copy(x1p_hbm.at[pl.ds(0, MOE_R * PACK_ROWS), :], ubuf.at[sl, span, :], sem_in.at[sl]).wait()

        for_blocks(n_in(tt), block)

    def unpack_block(r):
        for j in range(PACK_ROWS):
            start = r * (MOE_R * PACK_ROWS) + j
            u = ubuf[slot, pl.ds(start, MOE_R, stride=PACK_ROWS), :]
            xbuf[rows(0, r), j * LANE:(j + 1) * LANE] = u.astype(BF16)

    @pl.when(c == 0)
    def _begin():
        @pl.when(t == 0)
        def _():
            start_gather(0, 0)

        @pl.when(t + 1 < n_items)
        def _():
            start_gather(t + 1, 1 - slot)

        wait_gather(t, slot)
        for_blocks(n_in(t), unpack_block)

        @pl.when(t >= 2)
        def _():
            for_blocks(n_out(t - 2), lambda r: out_copy(t - 2, slot, r).wait())

        @pl.when(nb < 0)
        def _():
            acc[slot] = jnp.zeros(acc.shape[1:], F32)

    @pl.when(nb > 0)
    def _compute():
        wg = wg_ref[...].astype(BF16)
        wu = wu_ref[...].astype(BF16)
        wd = wd_ref[...].astype(BF16)

        def piece(off, size):
            rs = pl.ds(pl.multiple_of(off, MOE_R), size)
            x = xbuf[rs, :]
            gt = jnp.dot(x, wg, preferred_element_type=F32)
            up = jnp.dot(x, wu, preferred_element_type=F32)
            hid = (gt * jax.nn.sigmoid(gt) * up).astype(BF16)
            y = jnp.dot(hid, wd, preferred_element_type=F32)

            @pl.when(c == 0)
            def _():
                acc[slot, rs, :] = y

            @pl.when(c > 0)
            def _():
                acc[slot, rs, :] = acc[slot, rs, :] + y

        n4 = nb // 4
        rem = nb - 4 * n4
        for_blocks(n4, lambda i: piece(i * (4 * MOE_R), 4 * MOE_R))

        @pl.when(rem >= 2)
        def _():
            piece(n4 * (4 * MOE_R), 2 * MOE_R)

        @pl.when(rem % 2 == 1)
        def _():
            piece(n4 * (4 * MOE_R) + (rem // 2) * (2 * MOE_R), MOE_R)

    @pl.when(c == MOE_NC - 1)
    def _end():
        for_blocks(n_out(t), lambda r: out_copy(t, slot, r).start())

        @pl.when(t == n_items - 1)
        def _():
            for_blocks(n_out(t), lambda r: out_copy(t, slot, r).wait())

            @pl.when(t >= 1)
            def _():
                for_blocks(n_out(t - 1), lambda r: out_copy(t - 1, 1 - slot, r).wait())


def _moe(item_e, item_row0, item_nb, src, x1p, w_gate, w_up, w_down, n_items, n_rows):
    def chunk(t, c, n):
        return jnp.where(n[t] > 0, c, MOE_NC - 1)

    grid_spec = pltpu.PrefetchScalarGridSpec(
        num_scalar_prefetch=4,
        grid=(n_items, MOE_NC),
        in_specs=[
            pl.BlockSpec(memory_space=pl.ANY),
            pl.BlockSpec((None, D_MODEL, MOE_FC), lambda t, c, e, r, n, s: (e[t], 0, chunk(t, c, n))),
            pl.BlockSpec((None, D_MODEL, MOE_FC), lambda t, c, e, r, n, s: (e[t], 0, chunk(t, c, n))),
            pl.BlockSpec((None, MOE_FC, D_MODEL), lambda t, c, e, r, n, s: (e[t], chunk(t, c, n), 0)),
        ],
        out_specs=pl.BlockSpec(memory_space=pl.ANY),
        scratch_shapes=[
            pltpu.VMEM((2, MOE_MAXB * MOE_R * PACK_ROWS, LANE), F32),
            pltpu.VMEM((MOE_MAXB * MOE_R, D_MODEL), BF16),
            pltpu.VMEM((2, MOE_MAXB * MOE_R, D_MODEL), F32),
            pltpu.SemaphoreType.DMA((2,)),
            pltpu.SemaphoreType.DMA((2,)),
        ],
    )
    return pl.pallas_call(
        _moe_body,
        grid_spec=grid_spec,
        out_shape=jax.ShapeDtypeStruct((n_rows, D_MODEL), F32),
        compiler_params=_cparams(("arbitrary", "arbitrary")),
    )(item_e, item_row0, item_nb, src, x1p, w_gate, w_up, w_down)


COMB_TM = 128
COMB_ISSUE_UNROLL = 32


def _comb_issue(pos_ref, ys_hbm, buf, sem, tile, slot, tm):
    def body(g, _):
        for u in range(COMB_ISSUE_UNROLL):
            r = g * COMB_ISSUE_UNROLL + u
            for k in range(2):
                p = pos_ref[2 * (tile * tm + r) + k]
                pltpu.make_async_copy(ys_hbm.at[pl.ds(p, 1), :], buf.at[slot, k, pl.ds(r, 1), :], sem.at[slot]).start()
        return 0

    lax.fori_loop(0, tm // COMB_ISSUE_UNROLL, body, 0)


def _combine_body(pos_ref, ys_hbm, x1_ref, route_ref, g_ref, b_ref, o_ref, buf, sem, *, tm):
    i = pl.program_id(0)
    n = pl.num_programs(0)
    slot = i % 2

    @pl.when(i == 0)
    def _():
        _comb_issue(pos_ref, ys_hbm, buf, sem, 0, 0, tm)

    @pl.when(i + 1 < n)
    def _():
        _comb_issue(pos_ref, ys_hbm, buf, sem, i + 1, 1 - slot, tm)

    for k in range(2):
        pltpu.make_async_copy(ys_hbm.at[pl.ds(0, tm), :], buf.at[slot, k], sem.at[slot]).wait()
    f = buf[slot, 0] * route_ref[:, 2:3] + buf[slot, 1] * route_ref[:, 3:4]
    o_ref[...] = _layer_norm(ALPHA * x1_ref[...] + f, g_ref[...], b_ref[...])


def _combine(pos, ys, x1, route, g, b, *, row0, n):
    d = x1.shape[1]
    tm = COMB_TM
    assert row0 % tm == 0 and n % tm == 0
    off = row0 // tm
    grid_spec = pltpu.PrefetchScalarGridSpec(
        num_scalar_prefetch=1,
        grid=(n // tm,),
        in_specs=[
            pl.BlockSpec(memory_space=pl.ANY),
            pl.BlockSpec((tm, d), lambda i, p: (i + off, 0)),
            pl.BlockSpec((tm, ROUTE_W), lambda i, p: (i + off, 0)),
            pl.BlockSpec((1, d), lambda i, p: (0, 0)),
            pl.BlockSpec((1, d), lambda i, p: (0, 0)),
        ],
        out_specs=pl.BlockSpec((tm, d), lambda i, p: (i, 0)),
        scratch_shapes=[pltpu.VMEM((2, 2, tm, d), F32), pltpu.SemaphoreType.DMA((2,))],
    )
    return pl.pallas_call(
        functools.partial(_combine_body, tm=tm),
        grid_spec=grid_spec,
        out_shape=jax.ShapeDtypeStruct((n, d), F32),
        compiler_params=_cparams(("arbitrary",)),
    )(pos, ys, x1, route, g, b)


def _dispatch_plan(eid, n_items, n_rows):
    p_total = eid.shape[0]
    blk = 128
    assert p_total % blk == 0
    onehot = (eid[:, None] == jnp.arange(N_EXPERTS, dtype=jnp.int32)[None, :]).astype(F32)
    counts = jnp.sum(onehot, axis=0).astype(jnp.int32)
    ohb = onehot.reshape(p_total // blk, blk, N_EXPERTS)
    earlier = (jnp.arange(blk)[:, None] > jnp.arange(blk)[None, :]).astype(F32)
    within = jnp.einsum("ij,bjk->bik", earlier, ohb, precision=lax.Precision.HIGHEST)
    blk_tot = jnp.sum(ohb, axis=1)
    blk_off = jnp.cumsum(blk_tot, axis=0) - blk_tot
    rank = jnp.sum((within + blk_off[:, None, :]) * ohb, axis=2).reshape(p_total).astype(jnp.int32)
    nblk = (counts + MOE_R - 1) // MOE_R
    seg_start = (jnp.cumsum(nblk) - nblk) * MOE_R
    pos = seg_start[eid] + rank
    src = jnp.zeros((n_rows,), jnp.int32).at[pos].set(jnp.arange(p_total, dtype=jnp.int32) // 2)
    items_per_e = (nblk + MOE_MAXB - 1) // MOE_MAXB
    item_end = jnp.cumsum(items_per_e)
    item_start = item_end - items_per_e
    t = jnp.arange(n_items, dtype=jnp.int32)
    e_t = jnp.minimum(jnp.sum((item_end[None, :] <= t[:, None]).astype(jnp.int32), axis=1), N_EXPERTS - 1)
    live = t < item_end[-1]
    local = t - item_start[e_t]
    used = jnp.sum(nblk)
    idle0 = used + (t - item_end[-1]) * MOE_MAXB
    nz_t = jnp.clip(n_rows // MOE_R - idle0, 0, MOE_MAXB)
    nb_t = jnp.where(live, jnp.clip(nblk[e_t] - local * MOE_MAXB, 0, MOE_MAXB), -nz_t)
    row0_t = jnp.where(live, seg_start[e_t] + local * (MOE_MAXB * MOE_R),
                       jnp.minimum(idle0, n_rows // MOE_R - 1) * MOE_R)
    last_e = e_t[jnp.maximum(item_end[-1] - 1, 0)]
    e_t = jnp.where(live, e_t, last_e)
    return pos.astype(jnp.int32), src, e_t.astype(jnp.int32), row0_t.astype(jnp.int32), nb_t.astype(jnp.int32)


def _alibi_slopes():
    return jnp.asarray([2.0 ** (-8.0 * (h + 1) / N_AH) for h in range(N_AH)], dtype=F32)


def kernel(x_prompt, x_sample, state_conv, state_mlstm_C, state_mlstm_n, state_mlstm_m, cache_win_k, cache_win_v, w_in, b_gate, conv_w, conv_b, mh_gain, att_gain, w_out, ln1_g, ln1_b, w_group, b_group, w_router, b_router, w_gate, w_up, w_down, ln2_g, ln2_b):
    bp, tp, d = x_prompt.shape
    bs, ts, _ = x_sample.shape
    assert d == D_MODEL and w_in.shape[0] == 1 and tp % ATT_BLK == 0 and ts >= CONV_W - 1
    w_buf = cache_win_k.shape[2]
    n_p, n_s = bp * tp, bs * ts
    slopes = _alibi_slopes()

    wi = w_in[0]
    g0 = 4 * D_MLSTM
    g1 = g0 + 2 * N_MH
    w_pack = jnp.concatenate(
        [wi[:, :g0], wi[:, g1:], wi[:, g0:g1], jnp.zeros((d, LANE - 2 * N_MH), F32)], axis=1).astype(BF16)

    xp2 = x_prompt.reshape(n_p, d)
    xs2 = x_sample.reshape(n_s, d)
    proj_p = _in_proj(xp2, w_pack, 512)
    proj_s = _in_proj(xs2, w_pack, n_s)

    cw = conv_w[0]
    cb = conv_b[0][None, :]
    mh_g = mh_gain[0][None, :]
    att_g = att_gain[0][None, :]
    bg = b_gate[0]

    def gates_time_major(proj, batch, seq, pad_to):
        gt = proj[:, COL_G:COL_G + 2 * N_MH].reshape(batch, seq, 2, N_MH).transpose(0, 3, 2, 1)
        if pad_to > seq:
            gt = jnp.pad(gt, ((0, 0), (0, 0), (0, 0), (0, pad_to - seq)))
        return gt

    lc_p = 256
    hm_p, c_p, n_pp, m_p = _mlstm(
        proj_p, gates_time_major(proj_p, bp, tp, tp), jnp.zeros((bp, CONV_W - 1, 2 * D_MLSTM), F32), cw, cb,
        jnp.zeros((bp, N_MH, E_MH, E_MH), F32), jnp.zeros((bp, N_MH, 1, E_MH), F32),
        jnp.zeros((bp, N_MH, 1, 1), F32), mh_g, bg, batch=bp, seq=tp, lb=lc_p, lc=lc_p, hb=N_MH, out_dtype=BF16)
    ha_p = _attn_prompt(proj_p, att_g, slopes, batch=bp, seq=tp)

    lc_s = 16
    hm_s, c_s, n_ss, m_s = _mlstm(
        proj_s, gates_time_major(proj_s, bs, ts, lc_s), state_conv[0], cw, cb,
        state_mlstm_C[0], state_mlstm_n[0][:, :, None, :], state_mlstm_m[0][:, :, None, None],
        mh_g, bg, batch=bs, seq=ts, lb=ts, lc=lc_s, hb=N_MH, out_dtype=F32)
    new_rows = lambda col: proj_s[:, col:col + N_AH * E_AH].reshape(bs, ts, N_AH, E_AH)
    ha_s, wk_s, wv_s = _attn_decode(
        new_rows(COL_QA), new_rows(COL_KA), new_rows(COL_VA), cache_win_k, cache_win_v,
        att_gain[0].reshape(N_AH, E_AH), jnp.broadcast_to(slopes[:, None], (N_AH, E_AH)), batch=bs, s_new=ts)
    ha_s = ha_s.reshape(n_s, N_AH * E_AH)

    n_all = n_p + n_s
    wo = w_out[0].astype(BF16)
    w_r = jnp.concatenate(
        [w_group[0], w_router[0].transpose(1, 0, 2).reshape(d, N_EXPERTS),
         jnp.zeros((d, ROUTE_W - N_GROUPS - N_EXPERTS), F32)], axis=1)
    b_r = jnp.concatenate(
        [b_group[0], b_router[0].reshape(N_EXPERTS), jnp.zeros((ROUTE_W - N_GROUPS - N_EXPERTS,), F32)])[None, :]
    wr_hi = w_r.astype(BF16)
    wr_lo = (w_r - wr_hi.astype(F32)).astype(BF16)
    ln1 = (ln1_g[0][None, :], ln1_b[0][None, :])
    x1, x1p, route = _outproj(hm_p, ha_p, xp2, hm_s, ha_s, xs2, wo, *ln1, wr_hi, wr_lo, b_r)

    p_total = 2 * n_all
    n_rows = ((p_total + N_EXPERTS * (MOE_R - 1)) // MOE_R + 1) * MOE_R
    n_items = N_EXPERTS + n_rows // (MOE_R * MOE_MAXB)
    eid = route[:n_all, 0:2].astype(jnp.int32).reshape(p_total)
    pos, src, item_e, item_row0, item_nb = _dispatch_plan(eid, n_items, n_rows)
    ys = _moe(item_e, item_row0, item_nb, src, x1p, w_gate[0], w_up[0], w_down[0], n_items, n_rows)
    ln2 = (ln2_g[0][None, :], ln2_b[0][None, :])
    y_p = _combine(pos[:2 * n_p], ys, x1, route, *ln2, row0=0, n=n_p).reshape(bp, tp, d)
    y_s = _combine(pos[2 * n_p:], ys, x1, route, *ln2, row0=n_p, n=n_s).reshape(bs, ts, d)

    def tail_rows(proj, batch, seq, col, width, rows):
        return proj.reshape(batch, seq, N_PROJ)[:, seq - rows:, col:col + width]

    win = min(w_buf, tp)
    p_conv = tail_rows(proj_p, bp, tp, COL_QM, 2 * D_MLSTM, CONV_W - 1)[None]
    p_wk = tail_rows(proj_p, bp, tp, COL_KA, N_AH * E_AH, win).reshape(1, bp, win, N_AH, E_AH)
    p_wv = tail_rows(proj_p, bp, tp, COL_VA, N_AH * E_AH, win).reshape(1, bp, win, N_AH, E_AH)
    s_conv = tail_rows(proj_s, bs, ts, COL_QM, 2 * D_MLSTM, CONV_W - 1)[None]
    return (y_p, y_s,
            p_conv, c_p[None], n_pp[:, :, 0, :][None], m_p[:, :, 0, 0][None], p_wk, p_wv,
            s_conv, c_s[None], n_ss[:, :, 0, :][None], m_s[:, :, 0, 0][None], wk_s, wv_s)
```

```python
import functools
import math

import jax
import jax.numpy as jnp
from jax import lax
from jax.experimental import pallas as pl
from jax.experimental.pallas import tpu as pltpu

F32 = jnp.float32
BF16 = jnp.bfloat16
NEG_INF = float("-inf")

D_MODEL = 2048
D_MLSTM = 1024
N_MH = 4
E_MH = 256
N_AH = 8
E_AH = 128
DILATED_CONFIGS = ((128, 1), (512, 4), (2048, 16))
N_BACK = 128
BAND = 128
CONV_W = 4
N_GROUPS = 4
E_PER_GROUP = 8
N_EXPERTS = 32
D_FF = 1024
EPS = 1e-5
ALPHA = 2.0 ** 0.25

LANE = 128
SUBLANE = 8

COL_QM, COL_KM, COL_VM, COL_OM = 0, 1024, 2048, 3072
COL_QA, COL_KA, COL_VA, COL_G = 4096, 5120, 6144, 7168
N_PROJ = 7296
PROJ_TN = 2432

VMEM_LIMIT = 56 * 1024 * 1024


def _cparams(sem):
    return pltpu.CompilerParams(dimension_semantics=sem, vmem_limit_bytes=VMEM_LIMIT)


def _proj_body(x_ref, w_ref, o_ref):
    o_ref[...] = jnp.dot(x_ref[...].astype(BF16), w_ref[...], preferred_element_type=F32)


def _in_proj(x, w, tm):
    n = x.shape[0]
    return pl.pallas_call(
        _proj_body,
        grid=(N_PROJ // PROJ_TN, n // tm),
        in_specs=[pl.BlockSpec((tm, D_MODEL), lambda j, i: (i, 0)),
                  pl.BlockSpec((D_MODEL, PROJ_TN), lambda j, i: (0, j))],
        out_specs=pl.BlockSpec((tm, PROJ_TN), lambda j, i: (i, j)),
        out_shape=jax.ShapeDtypeStruct((n, N_PROJ), F32),
        compiler_params=_cparams(("arbitrary", "arbitrary")),
    )(x, w)


def _mlstm_body(bg_ref, xq_ref, xk_ref, v_ref, om_ref, g_ref, hq_ref, hk_ref, cwq_ref, cwk_ref,
                cbq_ref, cbk_ref, c0_ref, n0_ref, m0_ref, gain_ref,
                hm_ref, cout_ref, nout_ref, mout_ref,
                caug, m_s, uq, uk, vbuf, *, lb, lc, t_valid, hb):
    hg = pl.program_id(1)
    c = pl.program_id(2)
    nc = pl.num_programs(2)
    e = E_MH
    heads = range(hb)
    cols = lambda i: slice(i * e, (i + 1) * e)

    row_e = lax.broadcasted_iota(jnp.int32, (e, e), 0)
    col_e = lax.broadcasted_iota(jnp.int32, (e, e), 1)
    eye_e = row_e == col_e

    @pl.when(c == 0)
    def _init():
        for i in heads:
            caug[i, :, 0:e] = c0_ref[i]
            ncol = jnp.sum(jnp.where(eye_e, n0_ref[i], 0.0), axis=1, keepdims=True)
            caug[i, :, e:e + LANE] = jnp.broadcast_to(ncol, (e, LANE))
            m_s[i] = m0_ref[i]
        uq[...] = jnp.zeros(uq.shape, F32)
        uk[...] = jnp.zeros(uk.shape, F32)
        uq[pl.ds(SUBLANE - (CONV_W - 1), CONV_W - 1), :] = hq_ref[...]
        uk[pl.ds(SUBLANE - (CONV_W - 1), CONV_W - 1), :] = hk_ref[...]
        if lb != lc:
            vbuf[...] = jnp.zeros(vbuf.shape, F32)

    uq[pl.ds(SUBLANE, lb), :] = xq_ref[...]
    uk[pl.ds(SUBLANE, lb), :] = xk_ref[...]

    def conv_silu(u, cw_ref, cb_ref):
        acc = cb_ref[...]
        for j in range(CONV_W):
            acc = acc + u[pl.ds(SUBLANE - (CONV_W - 1) + j, lc), :] * cw_ref[j:j + 1, :]
        return acc * jax.nn.sigmoid(acc)

    q_all = conv_silu(uq, cwq_ref, cbq_ref)
    k_all = conv_silu(uk, cwk_ref, cbk_ref) * (e ** -0.5)
    if lb != lc:
        vbuf[pl.ds(0, lb), :] = v_ref[...]
        v_all = vbuf[...]
    else:
        v_all = v_ref[...]

    tq = uq[pl.ds(lc + SUBLANE - (CONV_W - 1), CONV_W - 1), :]
    tk = uk[pl.ds(lc + SUBLANE - (CONV_W - 1), CONV_W - 1), :]
    uq[pl.ds(SUBLANE - (CONV_W - 1), CONV_W - 1), :] = tq
    uk[pl.ds(SUBLANE - (CONV_W - 1), CONV_W - 1), :] = tk

    lane_t = lax.broadcasted_iota(jnp.int32, (1, lc), 1)
    valid = (lane_t + c * lc) < t_valid
    row_l = lax.broadcasted_iota(jnp.int32, (lc, lc), 0)
    col_l = lax.broadcasted_iota(jnp.int32, (lc, lc), 1)
    causal = col_l <= row_l
    eye_l = col_l == row_l
    nt = (((1,), (1,)), ((), ()))
    tn = (((0,), (0,)), ((), ()))

    st = []
    for i in heads:
        h = hg * hb + i
        gi = g_ref[i, 0:1, :] + bg_ref[h]
        gf = g_ref[i, 1:2, :] + bg_ref[N_MH + h]
        lf = -(jnp.maximum(-gf, 0.0) + jnp.log1p(jnp.exp(-jnp.abs(gf))))
        ig_row = jnp.where(valid, gi, NEG_INF)
        lf_row = jnp.where(valid, lf, 0.0)
        b_col = jnp.sum(jnp.where(causal, lf_row, 0.0), axis=1, keepdims=True)
        b_row = jnp.sum(jnp.where(eye_l, b_col, 0.0), axis=0, keepdims=True)
        m_prev = m_s[i]
        dlog = jnp.where(causal, b_col - b_row + ig_row, NEG_INF)
        a_col = b_col + m_prev
        m_t = jnp.maximum(a_col, jnp.max(dlog, axis=1, keepdims=True))
        st.append(dict(ig_row=ig_row, b_row=b_row, m_prev=m_prev, m_t=m_t,
                       w_intra=jnp.exp(dlog - m_t), w_inter=jnp.exp(a_col - m_t),
                       qb=q_all[:, cols(i)].astype(BF16), kb=k_all[:, cols(i)].astype(BF16),
                       vb=v_all[:, cols(i)].astype(BF16)))

    for i in heads:
        d = st[i]
        d["s"] = lax.dot_general(d["qb"], d["kb"], nt, preferred_element_type=F32) * d["w_intra"]
        d["qc"] = jnp.dot(d["qb"], caug[i].astype(BF16), preferred_element_type=F32)

    for i in heads:
        d = st[i]
        num = jnp.dot(d["s"].astype(BF16), d["vb"], preferred_element_type=F32) + d["w_inter"] * d["qc"][:, 0:e]
        den = jnp.sum(d["s"], axis=1, keepdims=True) + d["w_inter"] * d["qc"][:, e:e + 1]
        hh = num / jnp.maximum(jnp.abs(den), jnp.exp(-d["m_t"]))
        hh = hh - jnp.mean(hh, axis=1, keepdims=True)
        hh = hh * lax.rsqrt(jnp.mean(hh * hh, axis=1, keepdims=True) + EPS)
        if lb != lc:
            hh = hh[0:lb, :]
        hm = hh * gain_ref[:, cols(i)] * jax.nn.sigmoid(om_ref[:, cols(i)])
        hm_ref[:, cols(i)] = hm.astype(hm_ref.dtype)

    for i in heads:
        d = st[i]
        b_last = d["b_row"][:, lc - 1:lc]
        logw_row = b_last - d["b_row"] + d["ig_row"]
        m_new = jnp.maximum(b_last + d["m_prev"], jnp.max(logw_row, axis=1, keepdims=True))
        w_row = jnp.exp(logw_row - m_new)
        w_col = jnp.sum(jnp.where(eye_l, w_row, 0.0), axis=1, keepdims=True)
        decay = jnp.exp(b_last + d["m_prev"] - m_new)
        kw = (k_all[:, cols(i)] * w_col).astype(BF16)
        vaug = jnp.concatenate([d["vb"], jnp.ones((lc, LANE), BF16)], axis=1)
        upd = lax.dot_general(kw, vaug, tn, preferred_element_type=F32)
        caug[i] = decay * caug[i] + upd
        m_s[i] = m_new

    @pl.when(c == nc - 1)
    def _fin():
        for i in heads:
            cout_ref[i] = caug[i, :, 0:e]
            nout_ref[i] = jnp.sum(jnp.where(eye_e, caug[i, :, e:e + 1], 0.0), axis=0, keepdims=True)
            mout_ref[i] = jnp.broadcast_to(m_s[i], (1, LANE))


def _mlstm(xp, gates_t, hist, conv_w, conv_b, c0, n0, m0, gain, b_gate, *, batch, seq, lb, lc, hb, out_dtype):
    nc = seq // lb
    e = E_MH
    w = hb * e
    ng = N_MH // hb
    assert N_MH % hb == 0
    rb = lambda b, h, c, bg: b * nc + c
    qcol, kcol, vcol, ocol = COL_QM // w, COL_KM // w, COL_VM // w, COL_OM // w
    body = functools.partial(_mlstm_body, lb=lb, lc=lc, t_valid=seq if lb == lc else lb, hb=hb)
    grid_spec = pltpu.PrefetchScalarGridSpec(
        num_scalar_prefetch=1,
        grid=(batch, ng, nc),
        in_specs=[
            pl.BlockSpec((lb, w), lambda b, h, c, bg: (rb(b, h, c, bg), qcol + h)),
            pl.BlockSpec((lb, w), lambda b, h, c, bg: (rb(b, h, c, bg), kcol + h)),
            pl.BlockSpec((lb, w), lambda b, h, c, bg: (rb(b, h, c, bg), vcol + h)),
            pl.BlockSpec((lb, w), lambda b, h, c, bg: (rb(b, h, c, bg), ocol + h)),
            pl.BlockSpec((None, hb, 2, lc), lambda b, h, c, bg: (b, h, 0, c)),
            pl.BlockSpec((None, CONV_W - 1, w), lambda b, h, c, bg: (b, 0, h)),
            pl.BlockSpec((None, CONV_W - 1, w), lambda b, h, c, bg: (b, 0, ng + h)),
            pl.BlockSpec((CONV_W, w), lambda b, h, c, bg: (0, h)),
            pl.BlockSpec((CONV_W, w), lambda b, h, c, bg: (0, ng + h)),
            pl.BlockSpec((1, w), lambda b, h, c, bg: (0, h)),
            pl.BlockSpec((1, w), lambda b, h, c, bg: (0, ng + h)),
            pl.BlockSpec((None, hb, e, e), lambda b, h, c, bg: (b, h, 0, 0)),
            pl.BlockSpec((None, hb, 1, e), lambda b, h, c, bg: (b, h, 0, 0)),
            pl.BlockSpec((None, hb, 1, 1), lambda b, h, c, bg: (b, h, 0, 0)),
            pl.BlockSpec((1, w), lambda b, h, c, bg: (0, h)),
        ],
        out_specs=[
            pl.BlockSpec((lb, w), lambda b, h, c, bg: (rb(b, h, c, bg), h)),
            pl.BlockSpec((None, hb, e, e), lambda b, h, c, bg: (b, h, 0, 0)),
            pl.BlockSpec((None, hb, 1, e), lambda b, h, c, bg: (b, h, 0, 0)),
            pl.BlockSpec((None, hb, 1, LANE), lambda b, h, c, bg: (b, h, 0, 0)),
        ],
        scratch_shapes=[
            pltpu.VMEM((hb, e, e + LANE), F32),
            pltpu.VMEM((hb, 1, 1), F32),
            pltpu.VMEM((lc + 2 * SUBLANE, w), F32),
            pltpu.VMEM((lc + 2 * SUBLANE, w), F32),
            pltpu.VMEM((lc, w), F32),
        ],
    )
    return pl.pallas_call(
        body,
        grid_spec=grid_spec,
        out_shape=[
            jax.ShapeDtypeStruct((batch * seq, D_MLSTM), out_dtype),
            jax.ShapeDtypeStruct((batch, N_MH, e, e), F32),
            jax.ShapeDtypeStruct((batch, N_MH, 1, e), F32),
            jax.ShapeDtypeStruct((batch, N_MH, 1, LANE), F32),
        ],
        compiler_params=_cparams(("arbitrary", "arbitrary", "arbitrary")),
    )(b_gate, xp, xp, xp, xp, gates_t, hist, hist, conv_w, conv_w, conv_b, conv_b, c0, n0, m0, gain)


ATT_BLK = 2048
ATT_GROUP = 16


def _attn_body(sl_ref, q_ref, kc_ref, vc_ref, kp_ref, vp_ref, gain_ref, o_ref, o_s, m_s, l_s):
    blk = pl.program_id(1)
    h = pl.program_id(2)
    slope = sl_ref[h]
    scale = E_AH ** -0.5
    qi = lax.broadcasted_iota(jnp.int32, (BAND, BAND), 0)
    ci = lax.broadcasted_iota(jnp.int32, (BAND, BAND), 1)
    dist_prev = (BAND + qi - ci).astype(F32)
    dist_cur = (qi - ci).astype(F32)
    ok_prev = ci >= qi
    ok_cur = ci <= qi

    def run_units(cfg, d, specs):
        sd = slope * float(d)
        bm_cur = jnp.where(ok_cur, -sd * dist_cur, NEG_INF)
        nt = (((1,), (1,)), ((), ()))
        scores = []
        for prev_pen, q_sl, kp_src, kc_src, _, _ in specs:
            bm_prev = jnp.where(ok_prev, prev_pen - sd * dist_prev, NEG_INF)
            qv = q_ref[q_sl, :].astype(BF16)
            kp = kp_src[0][kp_src[1], :].astype(BF16)
            kc = kc_src[0][kc_src[1], :].astype(BF16)
            s_p = lax.dot_general(qv, kp, nt, preferred_element_type=F32) * scale + bm_prev
            s_c = lax.dot_general(qv, kc, nt, preferred_element_type=F32) * scale + bm_cur
            scores.append((s_p, s_c))
        probs = []
        for s_p, s_c in scores:
            m = jnp.max(jnp.maximum(s_p, s_c), axis=1, keepdims=True)
            p_p = jnp.exp(s_p - m)
            p_c = jnp.exp(s_c - m)
            l = jnp.sum(p_p + p_c, axis=1, keepdims=True)
            probs.append((m, l, p_p.astype(BF16), p_c.astype(BF16)))
        for (_, q_sl, _, _, vp_src, vc_src), (m, l, p_p, p_c) in zip(specs, probs):
            vp = vp_src[0][vp_src[1], :].astype(BF16)
            vc = vc_src[0][vc_src[1], :].astype(BF16)
            o = jnp.dot(p_p, vp, preferred_element_type=F32) + jnp.dot(p_c, vc, preferred_element_type=F32)
            o_s[cfg, q_sl, :] = o
            m_s[cfg, q_sl, :] = jnp.broadcast_to(m, (BAND, E_AH))
            l_s[cfg, q_sl, :] = jnp.broadcast_to(l, (BAND, E_AH))

    first_pen = jnp.where(blk == 0, NEG_INF, 0.0)

    for cfg, (win, d) in enumerate(DILATED_CONFIGS):
        span = BAND * d
        n_u = ATT_BLK // span

        def sl(r, u, _d=d, _span=span):
            if _d == 1:
                if isinstance(u, int):
                    return pl.ds(u * _span, BAND)
                return pl.ds(pl.multiple_of(u * _span, BAND), BAND)
            return pl.ds(r + u * _span, BAND, stride=_d)

        def head_unit(r, d=d, span=span, sl=sl):
            cur = sl(r, 0)
            prv = sl(r + ATT_BLK - span, 0) if d > 1 else pl.ds(ATT_BLK - span, BAND)
            return (first_pen, cur, (kp_ref, prv), (kc_ref, cur), (vp_ref, prv), (vc_ref, cur))

        def tail_unit(r, u, sl=sl):
            cur = sl(r, u)
            prv = sl(r, u - 1)
            return (0.0, cur, (kc_ref, prv), (kc_ref, cur), (vc_ref, prv), (vc_ref, cur))

        specs = [head_unit(r) if u == 0 else tail_unit(r, u) for u in range(n_u) for r in range(d)]
        for g in range(0, len(specs), ATT_GROUP):
            run_units(cfg, d, specs[g:g + ATT_GROUP])

    rows = 256

    def merge(i, _):
        rs = pl.ds(pl.multiple_of(i * rows, rows), rows)
        m0, m1, m2 = m_s[0, rs, :], m_s[1, rs, :], m_s[2, rs, :]
        m_all = jnp.maximum(jnp.maximum(m0, m1), m2)
        e0, e1, e2 = jnp.exp(m0 - m_all), jnp.exp(m1 - m_all), jnp.exp(m2 - m_all)
        num = e0 * o_s[0, rs, :] + e1 * o_s[1, rs, :] + e2 * o_s[2, rs, :]
        den = e0 * l_s[0, rs, :] + e1 * l_s[1, rs, :] + e2 * l_s[2, rs, :]
        ha = num / den
        ha = ha * lax.rsqrt(jnp.mean(ha * ha, axis=1, keepdims=True) + EPS)
        o_ref[rs, :] = (ha * gain_ref[...]).astype(o_ref.dtype)
        return 0

    lax.fori_loop(0, ATT_BLK // rows, merge, 0)


def _attn_prompt(xp, gain, slopes, *, batch, seq):
    nb = seq // ATT_BLK
    qc, kc, vc = COL_QA // E_AH, COL_KA // E_AH, COL_VA // E_AH
    cur = lambda b, i, h, s: b * nb + i
    prev = lambda b, i, h, s: b * nb + jnp.maximum(i - 1, 0)
    grid_spec = pltpu.PrefetchScalarGridSpec(
        num_scalar_prefetch=1,
        grid=(batch, nb, N_AH),
        in_specs=[
            pl.BlockSpec((ATT_BLK, E_AH), lambda b, i, h, s: (cur(b, i, h, s), qc + h)),
            pl.BlockSpec((ATT_BLK, E_AH), lambda b, i, h, s: (cur(b, i, h, s), kc + h)),
            pl.BlockSpec((ATT_BLK, E_AH), lambda b, i, h, s: (cur(b, i, h, s), vc + h)),
            pl.BlockSpec((ATT_BLK, E_AH), lambda b, i, h, s: (prev(b, i, h, s), kc + h)),
            pl.BlockSpec((ATT_BLK, E_AH), lambda b, i, h, s: (prev(b, i, h, s), vc + h)),
            pl.BlockSpec((1, E_AH), lambda b, i, h, s: (0, h)),
        ],
        out_specs=pl.BlockSpec((ATT_BLK, E_AH), lambda b, i, h, s: (cur(b, i, h, s), h)),
        scratch_shapes=[pltpu.VMEM((3, ATT_BLK, E_AH), F32)] * 3,
    )
    return pl.pallas_call(
        _attn_body,
        grid_spec=grid_spec,
        out_shape=jax.ShapeDtypeStruct((batch * seq, N_AH * E_AH), BF16),
        compiler_params=_cparams(("arbitrary", "arbitrary", "arbitrary")),
    )(slopes, xp, xp, xp, xp, xp, gain)


DEC_NEAR = 8


def _near_multiplicity():
    return [sum(1 for win, d in DILATED_CONFIGS if dist % d == 0 and dist <= win) for dist in range(DEC_NEAR)]


def _decode_body(q_ref, kn_ref, vn_ref, kc_ref, vc_ref, slope_ref, gain_ref,
                 ha_ref, ko_hbm, vo_hbm, ktail, vtail, sem, *, w_buf, s_new):
    b = pl.program_id(0)
    keep = w_buf - s_new
    copies = [
        pltpu.make_async_copy(kc_ref.at[0, 0, pl.ds(s_new, keep)], ko_hbm.at[0, b, pl.ds(0, keep)], sem.at[0]),
        pltpu.make_async_copy(vc_ref.at[0, 0, pl.ds(s_new, keep)], vo_hbm.at[0, b, pl.ds(0, keep)], sem.at[1]),
        pltpu.make_async_copy(kn_ref.at[0], ko_hbm.at[0, b, pl.ds(keep, s_new)], sem.at[2]),
        pltpu.make_async_copy(vn_ref.at[0], vo_hbm.at[0, b, pl.ds(keep, s_new)], sem.at[3]),
    ]
    for cp in copies:
        cp.start()

    ktail[pl.ds(0, DEC_NEAR)] = kc_ref[0, 0, pl.ds(w_buf - DEC_NEAR, DEC_NEAR)]
    vtail[pl.ds(0, DEC_NEAR)] = vc_ref[0, 0, pl.ds(w_buf - DEC_NEAR, DEC_NEAR)]
    ktail[pl.ds(DEC_NEAR, s_new)] = kn_ref[0]
    vtail[pl.ds(DEC_NEAR, s_new)] = vn_ref[0]

    scale = E_AH ** -0.5
    slope = slope_ref[:, 0:1]
    near_i = lax.broadcasted_iota(jnp.int32, (DEC_NEAR, 1, 1), 0)
    near_dist = (DEC_NEAR - 1 - near_i).astype(F32)
    near_mult = jnp.zeros((DEC_NEAR, 1, 1), F32)
    for dist, c in enumerate(_near_multiplicity()):
        near_mult = jnp.where(near_i == DEC_NEAR - 1 - dist, float(c), near_mult)

    def far_part(s, win, d):
        n = N_BACK - (DEC_NEAR - 1) // d
        sl = pl.ds(w_buf + s - win, n, stride=d) if d > 1 else pl.ds(w_buf + s - win, n)
        i = lax.broadcasted_iota(jnp.int32, (n, 1, 1), 0)
        return sl, ((N_BACK - i) * d).astype(F32)

    ones = jnp.ones((E_AH, E_AH), BF16)

    def row_dots(kk, q):
        n = kk.shape[0]
        prod = (kk * q).astype(BF16).reshape(n * N_AH, E_AH)
        return jnp.dot(prod, ones, preferred_element_type=F32).reshape(n, N_AH, E_AH)

    def one_query(s, _):
        q = q_ref[0, s]
        parts = []
        for win, d in DILATED_CONFIGS:
            sl, dist = far_part(s, win, d)
            sc = row_dots(kc_ref[0, 0, sl], q) * scale - slope * dist
            parts.append((sc, None, lambda sl=sl: vc_ref[0, 0, sl]))
        nsl = pl.ds(s + 1, DEC_NEAR)
        sc = row_dots(ktail[nsl], q) * scale - slope * near_dist
        parts.append((sc, near_mult, lambda: vtail[nsl]))
        m = functools.reduce(jnp.maximum, [jnp.max(p[0], axis=0, keepdims=True) for p in parts])
        den = jnp.zeros((1, N_AH, E_AH), F32)
        o = jnp.zeros((1, N_AH, E_AH), F32)
        for sc, mu, load_v in parts:
            p = jnp.exp(sc - m)
            if mu is not None:
                p = p * mu
            den = den + jnp.sum(p, axis=0, keepdims=True)
            o = o + jnp.sum(p * load_v(), axis=0, keepdims=True)
        o = (o / den)[0]
        o = o * lax.rsqrt(jnp.mean(o * o, axis=-1, keepdims=True) + EPS)
        ha_ref[0, s] = o * gain_ref[...]
        return 0

    lax.fori_loop(0, s_new, one_query, 0)
    for cp in copies:
        cp.wait()


def _attn_decode(q3, kn3, vn3, cache_k, cache_v, gain, slopes, *, batch, s_new):
    w_buf = cache_k.shape[2]
    assert w_buf >= max(w for w, _ in DILATED_CONFIGS) and s_new <= DEC_NEAR
    assert all(w // d == N_BACK and d & (d - 1) == 0 for w, d in DILATED_CONFIGS)
    body = functools.partial(_decode_body, w_buf=w_buf, s_new=s_new)
    new_spec = pl.BlockSpec((1, s_new, N_AH, E_AH), lambda b: (b, 0, 0, 0))
    cache_spec = pl.BlockSpec((1, 1, w_buf, N_AH, E_AH), lambda b: (0, b, 0, 0, 0))
    tile_spec = pl.BlockSpec((N_AH, E_AH), lambda b: (0, 0))
    return pl.pallas_call(
        body,
        grid=(batch,),
        in_specs=[new_spec, new_spec, new_spec, cache_spec, cache_spec, tile_spec, tile_spec],
        out_specs=[new_spec, pl.BlockSpec(memory_space=pl.ANY), pl.BlockSpec(memory_space=pl.ANY)],
        out_shape=[
            jax.ShapeDtypeStruct((batch, s_new, N_AH, E_AH), F32),
            jax.ShapeDtypeStruct(cache_k.shape, F32),
            jax.ShapeDtypeStruct(cache_v.shape, F32),
        ],
        scratch_shapes=[pltpu.VMEM((DEC_NEAR + s_new, N_AH, E_AH), F32)] * 2 + [pltpu.SemaphoreType.DMA((4,))],
        compiler_params=_cparams(("arbitrary",)),
    )(q3, kn3, vn3, cache_k, cache_v, slopes, gain)


OP_TM = 512
ROUTE_W = LANE
PACK_ROWS = D_MODEL // LANE


def _layer_norm(z, g, b):
    mu = jnp.mean(z, axis=1, keepdims=True)
    zc = z - mu
    var = jnp.mean(zc * zc, axis=1, keepdims=True)
    return zc * lax.rsqrt(var + EPS) * g + b


def _outproj_body(hm_ref, ha_ref, x_ref, wm_ref, wa_ref, g_ref, b_ref, wrh_ref, wrl_ref, br_ref,
                  x1_ref, x1p_ref, route_ref):
    y = (jnp.dot(hm_ref[...].astype(BF16), wm_ref[...], preferred_element_type=F32)
         + jnp.dot(ha_ref[...].astype(BF16), wa_ref[...], preferred_element_type=F32))
    x1 = _layer_norm(ALPHA * x_ref[...] + y, g_ref[...], b_ref[...])
    x1_ref[...] = x1
    tm = x1.shape[0]
    for j in range(PACK_ROWS):
        x1p_ref[pl.ds(j, tm, stride=PACK_ROWS), :] = x1[:, j * LANE:(j + 1) * LANE]
    x1_hi = x1.astype(BF16)
    x1_lo = (x1 - x1_hi.astype(F32)).astype(BF16)
    logits = (jnp.dot(x1_hi, wrh_ref[...], preferred_element_type=F32)
              + jnp.dot(x1_lo, wrh_ref[...], preferred_element_type=F32)
              + jnp.dot(x1_hi, wrl_ref[...], preferred_element_type=F32)) + br_ref[...]
    lane = lax.broadcasted_iota(jnp.int32, (tm, ROUTE_W), 1)
    lane_f = lane.astype(F32)
    big = float(ROUTE_W)
    gl = jnp.where(lane < N_GROUPS, logits, NEG_INF)
    gmax = jnp.max(gl, axis=1, keepdims=True)
    g_w = 1.0 / jnp.sum(jnp.exp(gl - gmax), axis=1, keepdims=True)
    g_idx = jnp.min(jnp.where(gl == gmax, lane_f, big), axis=1, keepdims=True)
    lo = N_GROUPS + E_PER_GROUP * g_idx
    el = jnp.where(jnp.logical_and(lane_f >= lo, lane_f < lo + E_PER_GROUP), logits, NEG_INF)
    v1 = jnp.max(el, axis=1, keepdims=True)
    i1 = jnp.min(jnp.where(el == v1, lane_f, big), axis=1, keepdims=True)
    el2 = jnp.where(lane_f == i1, NEG_INF, el)
    v2 = jnp.max(el2, axis=1, keepdims=True)
    i2 = jnp.min(jnp.where(el2 == v2, lane_f, big), axis=1, keepdims=True)
    e2 = jnp.exp(v2 - v1)
    w1 = g_w / (1.0 + e2)
    w2 = g_w * e2 / (1.0 + e2)
    route = jnp.where(lane == 0, i1 - N_GROUPS,
                      jnp.where(lane == 1, i2 - N_GROUPS,
                                jnp.where(lane == 2, w1, jnp.where(lane == 3, w2, 0.0))))
    route_ref[...] = route


def _outproj_two_groups(hmp_ref, hap_ref, xp_ref, hms_ref, has_ref, xs_ref, *rest, steps_p):
    i = pl.program_id(0)

    @pl.when(i < steps_p)
    def _():
        _outproj_body(hmp_ref, hap_ref, xp_ref, *rest)

    @pl.when(i >= steps_p)
    def _():
        _outproj_body(hms_ref, has_ref, xs_ref, *rest)


def _outproj(hm_p, ha_p, x_p, hm_s, ha_s, x_s, wo, g, b, wr_hi, wr_lo, br):
    tm = OP_TM
    n_p, n_s = x_p.shape[0], x_s.shape[0]
    assert n_p % tm == 0 and n_s <= tm
    steps_p = n_p // tm
    n = n_p + tm
    pad = lambda a: jnp.pad(a, ((0, tm - n_s), (0, 0)))
    hm_s, ha_s, x_s = pad(hm_s), pad(ha_s), pad(x_s)
    row = lambda i: (i, 0)
    prow = lambda i: (jnp.minimum(i, steps_p - 1), 0)
    fixed = lambda i: (0, 0)
    once = pl.Buffered(1)
    return pl.pallas_call(
        functools.partial(_outproj_two_groups, steps_p=steps_p),
        grid=(steps_p + 1,),
        in_specs=[
            pl.BlockSpec((tm, D_MLSTM), prow),
            pl.BlockSpec((tm, N_AH * E_AH), prow),
            pl.BlockSpec((tm, D_MODEL), prow),
            pl.BlockSpec((tm, D_MLSTM), fixed),
            pl.BlockSpec((tm, N_AH * E_AH), fixed),
            pl.BlockSpec((tm, D_MODEL), fixed),
            pl.BlockSpec((D_MLSTM, D_MODEL), fixed, pipeline_mode=once),
            pl.BlockSpec((N_AH * E_AH, D_MODEL), lambda i: (1, 0), pipeline_mode=once),
            pl.BlockSpec((1, D_MODEL), fixed),
            pl.BlockSpec((1, D_MODEL), fixed),
            pl.BlockSpec((D_MODEL, ROUTE_W), fixed),
            pl.BlockSpec((D_MODEL, ROUTE_W), fixed),
            pl.BlockSpec((1, ROUTE_W), fixed),
        ],
        out_specs=[pl.BlockSpec((tm, D_MODEL), row), pl.BlockSpec((tm * PACK_ROWS, LANE), row),
                   pl.BlockSpec((tm, ROUTE_W), row)],
        out_shape=[jax.ShapeDtypeStruct((n, D_MODEL), F32), jax.ShapeDtypeStruct((n * PACK_ROWS, LANE), F32),
                   jax.ShapeDtypeStruct((n, ROUTE_W), F32)],
        compiler_params=_cparams(("arbitrary",)),
    )(hm_p, ha_p, x_p, hm_s, ha_s, x_s, wo, wo, g, b, wr_hi, wr_lo, br)


MOE_R = 128
MOE_MAXB = 6
MOE_FC = 256
MOE_NC = D_FF // MOE_FC
MOE_ISSUE_UNROLL = 32


def _moe_body(ex_ref, row0_ref, nb_ref, src_ref, x1p_hbm, wg_ref, wu_ref, wd_ref, ys_hbm,
              ubuf, xbuf, acc, sem_in, sem_out):
    t = pl.program_id(0)
    c = pl.program_id(1)
    n_items = pl.num_programs(0)
    slot = t % 2
    nb = nb_ref[t]

    def rows(base, r):
        return pl.ds(pl.multiple_of(base + r * MOE_R, MOE_R), MOE_R)

    def tile_rows(i):
        return pl.ds(pl.multiple_of(i * PACK_ROWS, PACK_ROWS), PACK_ROWS)

    def out_copy(tt, sl, r):
        return pltpu.make_async_copy(acc.at[sl, rows(0, r), :], ys_hbm.at[rows(row0_ref[tt], r), :], sem_out.at[sl])

    def for_blocks(n, fn):
        def body(r, _):
            fn(r)
            return 0

        lax.fori_loop(0, n, body, 0)

    def n_in(tt):
        return jnp.maximum(nb_ref[tt], 0)

    def n_out(tt):
        return jnp.abs(nb_ref[tt])

    def start_gather(tt, sl):
        base = row0_ref[tt]

        def group(gi):
            for k in range(MOE_ISSUE_UNROLL):
                i = gi * MOE_ISSUE_UNROLL + k
                tok = src_ref[base + i]
                pltpu.make_async_copy(x1p_hbm.at[tile_rows(tok), :], ubuf.at[sl, tile_rows(i), :], sem_in.at[sl]).start()

        for_blocks(n_in(tt) * (MOE_R // MOE_ISSUE_UNROLL), group)

    def wait_gather(tt, sl):
        def block(r):
            span = pl.ds(pl.multiple_of(r * (MOE_R * PACK_ROWS), MOE_R * PACK_ROWS), MOE_R * PACK_ROWS)
            pltpu.make_async_copy(x1p_hbm.at[pl.ds(0, MOE_R * PACK_ROWS), :], ubuf.at[sl, span, :], sem_in.at[sl]).wait()

        for_blocks(n_in(tt), block)

    def unpack_block(r):
        for j in range(PACK_ROWS):
            start = r * (MOE_R * PACK_ROWS) + j
            u = ubuf[slot, pl.ds(start, MOE_R, stride=PACK_ROWS), :]
            xbuf[rows(0, r), j * LANE:(j + 1) * LANE] = u.astype(BF16)

    @pl.when(c == 0)
    def _begin():
        @pl.when(t == 0)
        def _():
            start_gather(0, 0)

        @pl.when(t + 1 < n_items)
        def _():
            start_gather(t + 1, 1 - slot)

        wait_gather(t, slot)
        for_blocks(n_in(t), unpack_block)

        @pl.when(t >= 2)
        def _():
            for_blocks(n_out(t - 2), lambda r: out_copy(t - 2, slot, r).wait())

        @pl.when(nb < 0)
        def _():
            acc[slot] = jnp.zeros(acc.shape[1:], F32)

    @pl.when(nb > 0)
    def _compute():
        wg = wg_ref[...].astype(BF16)
        wu = wu_ref[...].astype(BF16)
        wd = wd_ref[...].astype(BF16)

        def piece(size):
            rs = pl.ds(0, size)
            x = xbuf[rs, :]
            gt = jnp.dot(x, wg, preferred_element_type=F32)
            up = jnp.dot(x, wu, preferred_element_type=F32)
            hid = (gt * jax.nn.sigmoid(gt) * up).astype(BF16)
            y = jnp.dot(hid, wd, preferred_element_type=F32)

            @pl.when(c == 0)
            def _():
                acc[slot, rs, :] = y

            @pl.when(c > 0)
            def _():
                acc[slot, rs, :] = acc[slot, rs, :] + y

        for blocks in range(1, MOE_MAXB + 1):
            @pl.when(nb == blocks)
            def _(blocks=blocks):
                piece(blocks * MOE_R)

    @pl.when(c == MOE_NC - 1)
    def _end():
        for_blocks(n_out(t), lambda r: out_copy(t, slot, r).start())

        @pl.when(t == n_items - 1)
        def _():
            for_blocks(n_out(t), lambda r: out_copy(t, slot, r).wait())

            @pl.when(t >= 1)
            def _():
                for_blocks(n_out(t - 1), lambda r: out_copy(t - 1, 1 - slot, r).wait())


def _moe(item_e, item_row0, item_nb, src, x1p, w_gate, w_up, w_down, n_items, n_rows):
    def chunk(t, c, n):
        return jnp.where(n[t] > 0, c, MOE_NC - 1)

    grid_spec = pltpu.PrefetchScalarGridSpec(
        num_scalar_prefetch=4,
        grid=(n_items, MOE_NC),
        in_specs=[
            pl.BlockSpec(memory_space=pl.ANY),
            pl.BlockSpec((None, D_MODEL, MOE_FC), lambda t, c, e, r, n, s: (e[t], 0, chunk(t, c, n))),
            pl.BlockSpec((None, D_MODEL, MOE_FC), lambda t, c, e, r, n, s: (e[t], 0, chunk(t, c, n))),
            pl.BlockSpec((None, MOE_FC, D_MODEL), lambda t, c, e, r, n, s: (e[t], chunk(t, c, n), 0)),
        ],
        out_specs=pl.BlockSpec(memory_space=pl.ANY),
        scratch_shapes=[
            pltpu.VMEM((2, MOE_MAXB * MOE_R * PACK_ROWS, LANE), F32),
            pltpu.VMEM((MOE_MAXB * MOE_R, D_MODEL), BF16),
            pltpu.VMEM((2, MOE_MAXB * MOE_R, D_MODEL), F32),
            pltpu.SemaphoreType.DMA((2,)),
            pltpu.SemaphoreType.DMA((2,)),
        ],
    )
    return pl.pallas_call(
        _moe_body,
        grid_spec=grid_spec,
        out_shape=jax.ShapeDtypeStruct((n_rows, D_MODEL), F32),
        compiler_params=_cparams(("arbitrary", "arbitrary")),
    )(item_e, item_row0, item_nb, src, x1p, w_gate, w_up, w_down)


COMB_TM = 128
COMB_ISSUE_UNROLL = 32


def _comb_issue(pos_ref, ys_hbm, buf, sem, tile, slot, tm):
    def body(g, _):
        for u in range(COMB_ISSUE_UNROLL):
            r = g * COMB_ISSUE_UNROLL + u
            for k in range(2):
                p = pos_ref[2 * (tile * tm + r) + k]
                pltpu.make_async_copy(ys_hbm.at[pl.ds(p, 1), :], buf.at[slot, k, pl.ds(r, 1), :], sem.at[slot]).start()
        return 0

    lax.fori_loop(0, tm // COMB_ISSUE_UNROLL, body, 0)


def _combine_body(pos_ref, ys_hbm, x1_ref, route_ref, g_ref, b_ref, o_ref, buf, sem, *, tm):
    i = pl.program_id(0)
    n = pl.num_programs(0)
    slot = i % 2

    @pl.when(i == 0)
    def _():
        _comb_issue(pos_ref, ys_hbm, buf, sem, 0, 0, tm)

    @pl.when(i + 1 < n)
    def _():
        _comb_issue(pos_ref, ys_hbm, buf, sem, i + 1, 1 - slot, tm)

    for k in range(2):
        pltpu.make_async_copy(ys_hbm.at[pl.ds(0, tm), :], buf.at[slot, k], sem.at[slot]).wait()
    f = buf[slot, 0] * route_ref[:, 2:3] + buf[slot, 1] * route_ref[:, 3:4]
    o_ref[...] = _layer_norm(ALPHA * x1_ref[...] + f, g_ref[...], b_ref[...])


def _combine(pos, ys, x1, route, g, b, *, row0, n):
    d = x1.shape[1]
    tm = COMB_TM
    assert row0 % tm == 0 and n % tm == 0
    off = row0 // tm
    grid_spec = pltpu.PrefetchScalarGridSpec(
        num_scalar_prefetch=1,
        grid=(n // tm,),
        in_specs=[
            pl.BlockSpec(memory_space=pl.ANY),
            pl.BlockSpec((tm, d), lambda i, p: (i + off, 0)),
            pl.BlockSpec((tm, ROUTE_W), lambda i, p: (i + off, 0)),
            pl.BlockSpec((1, d), lambda i, p: (0, 0)),
            pl.BlockSpec((1, d), lambda i, p: (0, 0)),
        ],
        out_specs=pl.BlockSpec((tm, d), lambda i, p: (i, 0)),
        scratch_shapes=[pltpu.VMEM((2, 2, tm, d), F32), pltpu.SemaphoreType.DMA((2,))],
    )
    return pl.pallas_call(
        functools.partial(_combine_body, tm=tm),
        grid_spec=grid_spec,
        out_shape=jax.ShapeDtypeStruct((n, d), F32),
        compiler_params=_cparams(("arbitrary",)),
    )(pos, ys, x1, route, g, b)


def _dispatch_plan(eid, n_items, n_rows):
    p_total = eid.shape[0]
    blk = 128
    assert p_total % blk == 0
    onehot = (eid[:, None] == jnp.arange(N_EXPERTS, dtype=jnp.int32)[None, :]).astype(F32)
    counts = jnp.sum(onehot, axis=0).astype(jnp.int32)
    ohb = onehot.reshape(p_total // blk, blk, N_EXPERTS)
    earlier = (jnp.arange(blk)[:, None] > jnp.arange(blk)[None, :]).astype(F32)
    within = jnp.einsum("ij,bjk->bik", earlier, ohb, precision=lax.Precision.HIGHEST)
    blk_tot = jnp.sum(ohb, axis=1)
    blk_off = jnp.cumsum(blk_tot, axis=0) - blk_tot
    rank = jnp.sum((within + blk_off[:, None, :]) * ohb, axis=2).reshape(p_total).astype(jnp.int32)
    nblk = (counts + MOE_R - 1) // MOE_R
    seg_start = (jnp.cumsum(nblk) - nblk) * MOE_R
    pos = seg_start[eid] + rank
    src = jnp.zeros((n_rows,), jnp.int32).at[pos].set(jnp.arange(p_total, dtype=jnp.int32) // 2)
    items_per_e = (nblk + MOE_MAXB - 1) // MOE_MAXB
    item_end = jnp.cumsum(items_per_e)
    item_start = item_end - items_per_e
    t = jnp.arange(n_items, dtype=jnp.int32)
    e_t = jnp.minimum(jnp.sum((item_end[None, :] <= t[:, None]).astype(jnp.int32), axis=1), N_EXPERTS - 1)
    live = t < item_end[-1]
    local = t - item_start[e_t]
    used = jnp.sum(nblk)
    idle0 = used + (t - item_end[-1]) * MOE_MAXB
    nz_t = jnp.clip(n_rows // MOE_R - idle0, 0, MOE_MAXB)
    nb_t = jnp.where(live, jnp.clip(nblk[e_t] - local * MOE_MAXB, 0, MOE_MAXB), -nz_t)
    row0_t = jnp.where(live, seg_start[e_t] + local * (MOE_MAXB * MOE_R),
                       jnp.minimum(idle0, n_rows // MOE_R - 1) * MOE_R)
    last_e = e_t[jnp.maximum(item_end[-1] - 1, 0)]
    e_t = jnp.where(live, e_t, last_e)
    return pos.astype(jnp.int32), src, e_t.astype(jnp.int32), row0_t.astype(jnp.int32), nb_t.astype(jnp.int32)


def _alibi_slopes():
    return jnp.asarray([2.0 ** (-8.0 * (h + 1) / N_AH) for h in range(N_AH)], dtype=F32)


def kernel(x_prompt, x_sample, state_conv, state_mlstm_C, state_mlstm_n, state_mlstm_m, cache_win_k, cache_win_v, w_in, b_gate, conv_w, conv_b, mh_gain, att_gain, w_out, ln1_g, ln1_b, w_group, b_group, w_router, b_router, w_gate, w_up, w_down, ln2_g, ln2_b):
    bp, tp, d = x_prompt.shape
    bs, ts, _ = x_sample.shape
    assert d == D_MODEL and w_in.shape[0] == 1 and tp % ATT_BLK == 0 and ts >= CONV_W - 1
    w_buf = cache_win_k.shape[2]
    n_p, n_s = bp * tp, bs * ts
    slopes = _alibi_slopes()

    wi = w_in[0]
    g0 = 4 * D_MLSTM
    g1 = g0 + 2 * N_MH
    w_pack = jnp.concatenate(
        [wi[:, :g0], wi[:, g1:], wi[:, g0:g1], jnp.zeros((d, LANE - 2 * N_MH), F32)], axis=1).astype(BF16)

    xp2 = x_prompt.reshape(n_p, d)
    xs2 = x_sample.reshape(n_s, d)
    proj_p = _in_proj(xp2, w_pack, 512)
    proj_s = _in_proj(xs2, w_pack, n_s)

    cw = conv_w[0]
    cb = conv_b[0][None, :]
    mh_g = mh_gain[0][None, :]
    att_g = att_gain[0][None, :]
    bg = b_gate[0]

    def gates_time_major(proj, batch, seq, pad_to):
        gt = proj[:, COL_G:COL_G + 2 * N_MH].reshape(batch, seq, 2, N_MH).transpose(0, 3, 2, 1)
        if pad_to > seq:
            gt = jnp.pad(gt, ((0, 0), (0, 0), (0, 0), (0, pad_to - seq)))
        return gt

    lc_p = 256
    hm_p, c_p, n_pp, m_p = _mlstm(
        proj_p, gates_time_major(proj_p, bp, tp, tp), jnp.zeros((bp, CONV_W - 1, 2 * D_MLSTM), F32), cw, cb,
        jnp.zeros((bp, N_MH, E_MH, E_MH), F32), jnp.zeros((bp, N_MH, 1, E_MH), F32),
        jnp.zeros((bp, N_MH, 1, 1), F32), mh_g, bg, batch=bp, seq=tp, lb=lc_p, lc=lc_p, hb=N_MH, out_dtype=BF16)
    ha_p = _attn_prompt(proj_p, att_g, slopes, batch=bp, seq=tp)

    lc_s = 16
    hm_s, c_s, n_ss, m_s = _mlstm(
        proj_s, gates_time_major(proj_s, bs, ts, lc_s), state_conv[0], cw, cb,
        state_mlstm_C[0], state_mlstm_n[0][:, :, None, :], state_mlstm_m[0][:, :, None, None],
        mh_g, bg, batch=bs, seq=ts, lb=ts, lc=lc_s, hb=N_MH, out_dtype=F32)
    new_rows = lambda col: proj_s[:, col:col + N_AH * E_AH].reshape(bs, ts, N_AH, E_AH)
    ha_s, wk_s, wv_s = _attn_decode(
        new_rows(COL_QA), new_rows(COL_KA), new_rows(COL_VA), cache_win_k, cache_win_v,
        att_gain[0].reshape(N_AH, E_AH), jnp.broadcast_to(slopes[:, None], (N_AH, E_AH)), batch=bs, s_new=ts)
    ha_s = ha_s.reshape(n_s, N_AH * E_AH)

    n_all = n_p + n_s
    wo = w_out[0].astype(BF16)
    w_r = jnp.concatenate(
        [w_group[0], w_router[0].transpose(1, 0, 2).reshape(d, N_EXPERTS),
         jnp.zeros((d, ROUTE_W - N_GROUPS - N_EXPERTS), F32)], axis=1)
    b_r = jnp.concatenate(
        [b_group[0], b_router[0].reshape(N_EXPERTS), jnp.zeros((ROUTE_W - N_GROUPS - N_EXPERTS,), F32)])[None, :]
    wr_hi = w_r.astype(BF16)
    wr_lo = (w_r - wr_hi.astype(F32)).astype(BF16)
    ln1 = (ln1_g[0][None, :], ln1_b[0][None, :])
    x1, x1p, route = _outproj(hm_p, ha_p, xp2, hm_s, ha_s, xs2, wo, *ln1, wr_hi, wr_lo, b_r)

    p_total = 2 * n_all
    n_rows = ((p_total + N_EXPERTS * (MOE_R - 1)) // MOE_R + 1) * MOE_R
    n_items = N_EXPERTS + n_rows // (MOE_R * MOE_MAXB)
    eid = route[:n_all, 0:2].astype(jnp.int32).reshape(p_total)
    pos, src, item_e, item_row0, item_nb = _dispatch_plan(eid, n_items, n_rows)
    ys = _moe(item_e, item_row0, item_nb, src, x1p, w_gate[0], w_up[0], w_down[0], n_items, n_rows)
    ln2 = (ln2_g[0][None, :], ln2_b[0][None, :])
    y_p = _combine(pos[:2 * n_p], ys, x1, route, *ln2, row0=0, n=n_p).reshape(bp, tp, d)
    y_s = _combine(pos[2 * n_p:], ys, x1, route, *ln2, row0=n_p, n=n_s).reshape(bs, ts, d)

    def tail_rows(proj, batch, seq, col, width, rows):
        return proj.reshape(batch, seq, N_PROJ)[:, seq - rows:, col:col + width]

    win = min(w_buf, tp)
    p_conv = tail_rows(proj_p, bp, tp, COL_QM, 2 * D_MLSTM, CONV_W - 1)[None]
    p_wk = tail_rows(proj_p, bp, tp, COL_KA, N_AH * E_AH, win).reshape(1, bp, win, N_AH, E_AH)
    p_wv = tail_rows(proj_p, bp, tp, COL_VA, N_AH * E_AH, win).reshape(1, bp, win, N_AH, E_AH)
    s_conv = tail_rows(proj_s, bs, ts, COL_QM, 2 * D_MLSTM, CONV_W - 1)[None]
    return (y_p, y_s,
            p_conv, c_p[None], n_pp[:, :, 0, :][None], m_p[:, :, 0, 0][None], p_wk, p_wv,
            s_conv, c_s[None], n_ss[:, :, 0, :][None], m_s[:, :, 0, 0][None], wk_s, wv_s)
```

```python
import functools
import math

import jax
import jax.numpy as jnp
from jax import lax
from jax.experimental import pallas as pl
from jax.experimental.pallas import tpu as pltpu

F32 = jnp.float32
BF16 = jnp.bfloat16
NEG_INF = float("-inf")

D_MODEL = 2048
D_MLSTM = 1024
N_MH = 4
E_MH = 256
N_AH = 8
E_AH = 128
DILATED_CONFIGS = ((128, 1), (512, 4), (2048, 16))
N_BACK = 128
BAND = 128
CONV_W = 4
N_GROUPS = 4
E_PER_GROUP = 8
N_EXPERTS = 32
D_FF = 1024
EPS = 1e-5
ALPHA = 2.0 ** 0.25

LANE = 128
SUBLANE = 8

COL_QM, COL_KM, COL_VM, COL_OM = 0, 1024, 2048, 3072
COL_QA, COL_KA, COL_VA, COL_G = 4096, 5120, 6144, 7168
N_PROJ = 7296
PROJ_TN = 2432

VMEM_LIMIT = 56 * 1024 * 1024


def _cparams(sem):
    return pltpu.CompilerParams(dimension_semantics=sem, vmem_limit_bytes=VMEM_LIMIT)


def _proj_body(x_ref, w_ref, o_ref):
    o_ref[...] = jnp.dot(x_ref[...].astype(BF16), w_ref[...], preferred_element_type=F32)


def _in_proj(x, w, tm):
    n = x.shape[0]
    return pl.pallas_call(
        _proj_body,
        grid=(N_PROJ // PROJ_TN, n // tm),
        in_specs=[pl.BlockSpec((tm, D_MODEL), lambda j, i: (i, 0)),
                  pl.BlockSpec((D_MODEL, PROJ_TN), lambda j, i: (0, j))],
        out_specs=pl.BlockSpec((tm, PROJ_TN), lambda j, i: (i, j)),
        out_shape=jax.ShapeDtypeStruct((n, N_PROJ), F32),
        compiler_params=_cparams(("arbitrary", "arbitrary")),
    )(x, w)


def _mlstm_body(bg_ref, xq_ref, xk_ref, v_ref, om_ref, g_ref, hq_ref, hk_ref, cwq_ref, cwk_ref,
                cbq_ref, cbk_ref, c0_ref, n0_ref, m0_ref, gain_ref,
                hm_ref, cout_ref, nout_ref, mout_ref,
                caug, m_s, uq, uk, vbuf, *, lb, lc, t_valid, hb):
    hg = pl.program_id(1)
    c = pl.program_id(2)
    nc = pl.num_programs(2)
    e = E_MH
    heads = range(hb)
    cols = lambda i: slice(i * e, (i + 1) * e)

    row_e = lax.broadcasted_iota(jnp.int32, (e, e), 0)
    col_e = lax.broadcasted_iota(jnp.int32, (e, e), 1)
    eye_e = row_e == col_e

    @pl.when(c == 0)
    def _init():
        for i in heads:
            caug[i, :, 0:e] = c0_ref[i]
            ncol = jnp.sum(jnp.where(eye_e, n0_ref[i], 0.0), axis=1, keepdims=True)
            caug[i, :, e:e + LANE] = jnp.broadcast_to(ncol, (e, LANE))
            m_s[i] = m0_ref[i]
        uq[...] = jnp.zeros(uq.shape, F32)
        uk[...] = jnp.zeros(uk.shape, F32)
        uq[pl.ds(SUBLANE - (CONV_W - 1), CONV_W - 1), :] = hq_ref[...]
        uk[pl.ds(SUBLANE - (CONV_W - 1), CONV_W - 1), :] = hk_ref[...]
        if lb != lc:
            vbuf[...] = jnp.zeros(vbuf.shape, F32)

    uq[pl.ds(SUBLANE, lb), :] = xq_ref[...]
    uk[pl.ds(SUBLANE, lb), :] = xk_ref[...]

    def conv_silu(u, cw_ref, cb_ref):
        acc = cb_ref[...]
        for j in range(CONV_W):
            acc = acc + u[pl.ds(SUBLANE - (CONV_W - 1) + j, lc), :] * cw_ref[j:j + 1, :]
        return acc * jax.nn.sigmoid(acc)

    q_all = conv_silu(uq, cwq_ref, cbq_ref)
    k_all = conv_silu(uk, cwk_ref, cbk_ref) * (e ** -0.5)
    if lb != lc:
        vbuf[pl.ds(0, lb), :] = v_ref[...]
        v_all = vbuf[...]
    else:
        v_all = v_ref[...]

    tq = uq[pl.ds(lc + SUBLANE - (CONV_W - 1), CONV_W - 1), :]
    tk = uk[pl.ds(lc + SUBLANE - (CONV_W - 1), CONV_W - 1), :]
    uq[pl.ds(SUBLANE - (CONV_W - 1), CONV_W - 1), :] = tq
    uk[pl.ds(SUBLANE - (CONV_W - 1), CONV_W - 1), :] = tk

    lane_t = lax.broadcasted_iota(jnp.int32, (1, lc), 1)
    valid = (lane_t + c * lc) < t_valid
    row_l = lax.broadcasted_iota(jnp.int32, (lc, lc), 0)
    col_l = lax.broadcasted_iota(jnp.int32, (lc, lc), 1)
    causal = col_l <= row_l
    eye_l = col_l == row_l
    nt = (((1,), (1,)), ((), ()))
    tn = (((0,), (0,)), ((), ()))

    st = []
    for i in heads:
        h = hg * hb + i
        gi = g_ref[i, 0:1, :] + bg_ref[h]
        gf = g_ref[i, 1:2, :] + bg_ref[N_MH + h]
        lf = -(jnp.maximum(-gf, 0.0) + jnp.log1p(jnp.exp(-jnp.abs(gf))))
        ig_row = jnp.where(valid, gi, NEG_INF)
        lf_row = jnp.where(valid, lf, 0.0)
        b_col = jnp.sum(jnp.where(causal, lf_row, 0.0), axis=1, keepdims=True)
        b_row = jnp.sum(jnp.where(eye_l, b_col, 0.0), axis=0, keepdims=True)
        m_prev = m_s[i]
        dlog = jnp.where(causal, b_col - b_row + ig_row, NEG_INF)
        a_col = b_col + m_prev
        m_t = jnp.maximum(a_col, jnp.max(dlog, axis=1, keepdims=True))
        st.append(dict(ig_row=ig_row, b_row=b_row, m_prev=m_prev, m_t=m_t,
                       w_intra=jnp.exp(dlog - m_t), w_inter=jnp.exp(a_col - m_t),
                       qb=q_all[:, cols(i)].astype(BF16), kb=k_all[:, cols(i)].astype(BF16),
                       vb=v_all[:, cols(i)].astype(BF16)))

    for i in heads:
        d = st[i]
        d["s"] = lax.dot_general(d["qb"], d["kb"], nt, preferred_element_type=F32) * d["w_intra"]
        d["qc"] = jnp.dot(d["qb"], caug[i].astype(BF16), preferred_element_type=F32)

    for i in heads:
        d = st[i]
        num = jnp.dot(d["s"].astype(BF16), d["vb"], preferred_element_type=F32) + d["w_inter"] * d["qc"][:, 0:e]
        den = jnp.sum(d["s"], axis=1, keepdims=True) + d["w_inter"] * d["qc"][:, e:e + 1]
        hh = num / jnp.maximum(jnp.abs(den), jnp.exp(-d["m_t"]))
        hh = hh - jnp.mean(hh, axis=1, keepdims=True)
        hh = hh * lax.rsqrt(jnp.mean(hh * hh, axis=1, keepdims=True) + EPS)
        if lb != lc:
            hh = hh[0:lb, :]
        hm = hh * gain_ref[:, cols(i)] * jax.nn.sigmoid(om_ref[:, cols(i)])
        hm_ref[:, cols(i)] = hm.astype(hm_ref.dtype)

    for i in heads:
        d = st[i]
        b_last = d["b_row"][:, lc - 1:lc]
        logw_row = b_last - d["b_row"] + d["ig_row"]
        m_new = jnp.maximum(b_last + d["m_prev"], jnp.max(logw_row, axis=1, keepdims=True))
        w_row = jnp.exp(logw_row - m_new)
        w_col = jnp.sum(jnp.where(eye_l, w_row, 0.0), axis=1, keepdims=True)
        decay = jnp.exp(b_last + d["m_prev"] - m_new)
        kw = (k_all[:, cols(i)] * w_col).astype(BF16)
        vaug = jnp.concatenate([d["vb"], jnp.ones((lc, LANE), BF16)], axis=1)
        upd = lax.dot_general(kw, vaug, tn, preferred_element_type=F32)
        caug[i] = decay * caug[i] + upd
        m_s[i] = m_new

    @pl.when(c == nc - 1)
    def _fin():
        for i in heads:
            cout_ref[i] = caug[i, :, 0:e]
            nout_ref[i] = jnp.sum(jnp.where(eye_e, caug[i, :, e:e + 1], 0.0), axis=0, keepdims=True)
            mout_ref[i] = jnp.broadcast_to(m_s[i], (1, LANE))


def _mlstm(xp, gates_t, hist, conv_w, conv_b, c0, n0, m0, gain, b_gate, *, batch, seq, lb, lc, hb, out_dtype):
    nc = seq // lb
    e = E_MH
    w = hb * e
    ng = N_MH // hb
    assert N_MH % hb == 0
    rb = lambda b, h, c, bg: b * nc + c
    qcol, kcol, vcol, ocol = COL_QM // w, COL_KM // w, COL_VM // w, COL_OM // w
    body = functools.partial(_mlstm_body, lb=lb, lc=lc, t_valid=seq if lb == lc else lb, hb=hb)
    grid_spec = pltpu.PrefetchScalarGridSpec(
        num_scalar_prefetch=1,
        grid=(batch, ng, nc),
        in_specs=[
            pl.BlockSpec((lb, w), lambda b, h, c, bg: (rb(b, h, c, bg), qcol + h)),
            pl.BlockSpec((lb, w), lambda b, h, c, bg: (rb(b, h, c, bg), kcol + h)),
            pl.BlockSpec((lb, w), lambda b, h, c, bg: (rb(b, h, c, bg), vcol + h)),
            pl.BlockSpec((lb, w), lambda b, h, c, bg: (rb(b, h, c, bg), ocol + h)),
            pl.BlockSpec((None, hb, 2, lc), lambda b, h, c, bg: (b, h, 0, c)),
            pl.BlockSpec((None, CONV_W - 1, w), lambda b, h, c, bg: (b, 0, h)),
            pl.BlockSpec((None, CONV_W - 1, w), lambda b, h, c, bg: (b, 0, ng + h)),
            pl.BlockSpec((CONV_W, w), lambda b, h, c, bg: (0, h)),
            pl.BlockSpec((CONV_W, w), lambda b, h, c, bg: (0, ng + h)),
            pl.BlockSpec((1, w), lambda b, h, c, bg: (0, h)),
            pl.BlockSpec((1, w), lambda b, h, c, bg: (0, ng + h)),
            pl.BlockSpec((None, hb, e, e), lambda b, h, c, bg: (b, h, 0, 0)),
            pl.BlockSpec((None, hb, 1, e), lambda b, h, c, bg: (b, h, 0, 0)),
            pl.BlockSpec((None, hb, 1, 1), lambda b, h, c, bg: (b, h, 0, 0)),
            pl.BlockSpec((1, w), lambda b, h, c, bg: (0, h)),
        ],
        out_specs=[
            pl.BlockSpec((lb, w), lambda b, h, c, bg: (rb(b, h, c, bg), h)),
            pl.BlockSpec((None, hb, e, e), lambda b, h, c, bg: (b, h, 0, 0)),
            pl.BlockSpec((None, hb, 1, e), lambda b, h, c, bg: (b, h, 0, 0)),
            pl.BlockSpec((None, hb, 1, LANE), lambda b, h, c, bg: (b, h, 0, 0)),
        ],
        scratch_shapes=[
            pltpu.VMEM((hb, e, e + LANE), F32),
            pltpu.VMEM((hb, 1, 1), F32),
            pltpu.VMEM((lc + 2 * SUBLANE, w), F32),
            pltpu.VMEM((lc + 2 * SUBLANE, w), F32),
            pltpu.VMEM((lc, w), F32),
        ],
    )
    return pl.pallas_call(
        body,
        grid_spec=grid_spec,
        out_shape=[
            jax.ShapeDtypeStruct((batch * seq, D_MLSTM), out_dtype),
            jax.ShapeDtypeStruct((batch, N_MH, e, e), F32),
            jax.ShapeDtypeStruct((batch, N_MH, 1, e), F32),
            jax.ShapeDtypeStruct((batch, N_MH, 1, LANE), F32),
        ],
        compiler_params=_cparams(("arbitrary", "arbitrary", "arbitrary")),
    )(b_gate, xp, xp, xp, xp, gates_t, hist, hist, conv_w, conv_w, conv_b, conv_b, c0, n0, m0, gain)


ATT_BLK = 2048
ATT_GROUP = 16


def _attn_body(sl_ref, q_ref, kc_ref, vc_ref, kp_ref, vp_ref, gain_ref, o_ref, o_s, m_s, l_s):
    blk = pl.program_id(1)
    h = pl.program_id(2)
    slope = sl_ref[h]
    scale = E_AH ** -0.5
    qi = lax.broadcasted_iota(jnp.int32, (BAND, BAND), 0)
    ci = lax.broadcasted_iota(jnp.int32, (BAND, BAND), 1)
    dist_prev = (BAND + qi - ci).astype(F32)
    dist_cur = (qi - ci).astype(F32)
    ok_prev = ci >= qi
    ok_cur = ci <= qi

    def run_units(cfg, d, specs):
        sd = slope * float(d)
        bm_cur = jnp.where(ok_cur, -sd * dist_cur, NEG_INF)
        nt = (((1,), (1,)), ((), ()))
        scores = []
        for prev_pen, q_sl, kp_src, kc_src, _, _ in specs:
            bm_prev = jnp.where(ok_prev, prev_pen - sd * dist_prev, NEG_INF)
            qv = q_ref[q_sl, :].astype(BF16)
            kp = kp_src[0][kp_src[1], :].astype(BF16)
            kc = kc_src[0][kc_src[1], :].astype(BF16)
            s_p = lax.dot_general(qv, kp, nt, preferred_element_type=F32) * scale + bm_prev
            s_c = lax.dot_general(qv, kc, nt, preferred_element_type=F32) * scale + bm_cur
            scores.append((s_p, s_c))
        probs = []
        for s_p, s_c in scores:
            m = jnp.max(jnp.maximum(s_p, s_c), axis=1, keepdims=True)
            p_p = jnp.exp(s_p - m)
            p_c = jnp.exp(s_c - m)
            l = jnp.sum(p_p + p_c, axis=1, keepdims=True)
            probs.append((m, l, p_p.astype(BF16), p_c.astype(BF16)))
        for (_, q_sl, _, _, vp_src, vc_src), (m, l, p_p, p_c) in zip(specs, probs):
            vp = vp_src[0][vp_src[1], :].astype(BF16)
            vc = vc_src[0][vc_src[1], :].astype(BF16)
            o = jnp.dot(p_p, vp, preferred_element_type=F32) + jnp.dot(p_c, vc, preferred_element_type=F32)
            o_s[cfg, q_sl, :] = o
            m_s[cfg, q_sl, :] = jnp.broadcast_to(m, (BAND, E_AH))
            l_s[cfg, q_sl, :] = jnp.broadcast_to(l, (BAND, E_AH))

    first_pen = jnp.where(blk == 0, NEG_INF, 0.0)

    for cfg, (win, d) in enumerate(DILATED_CONFIGS):
        span = BAND * d
        n_u = ATT_BLK // span

        def sl(r, u, _d=d, _span=span):
            if _d == 1:
                if isinstance(u, int):
                    return pl.ds(u * _span, BAND)
                return pl.ds(pl.multiple_of(u * _span, BAND), BAND)
            return pl.ds(r + u * _span, BAND, stride=_d)

        def head_unit(r, d=d, span=span, sl=sl):
            cur = sl(r, 0)
            prv = sl(r + ATT_BLK - span, 0) if d > 1 else pl.ds(ATT_BLK - span, BAND)
            return (first_pen, cur, (kp_ref, prv), (kc_ref, cur), (vp_ref, prv), (vc_ref, cur))

        def tail_unit(r, u, sl=sl):
            cur = sl(r, u)
            prv = sl(r, u - 1)
            return (0.0, cur, (kc_ref, prv), (kc_ref, cur), (vc_ref, prv), (vc_ref, cur))

        specs = [head_unit(r) if u == 0 else tail_unit(r, u) for u in range(n_u) for r in range(d)]
        for g in range(0, len(specs), ATT_GROUP):
            run_units(cfg, d, specs[g:g + ATT_GROUP])

    rows = 256

    def merge(i, _):
        rs = pl.ds(pl.multiple_of(i * rows, rows), rows)
        m0, m1, m2 = m_s[0, rs, :], m_s[1, rs, :], m_s[2, rs, :]
        m_all = jnp.maximum(jnp.maximum(m0, m1), m2)
        e0, e1, e2 = jnp.exp(m0 - m_all), jnp.exp(m1 - m_all), jnp.exp(m2 - m_all)
        num = e0 * o_s[0, rs, :] + e1 * o_s[1, rs, :] + e2 * o_s[2, rs, :]
        den = e0 * l_s[0, rs, :] + e1 * l_s[1, rs, :] + e2 * l_s[2, rs, :]
        ha = num / den
        ha = ha * lax.rsqrt(jnp.mean(ha * ha, axis=1, keepdims=True) + EPS)
        o_ref[rs, :] = (ha * gain_ref[...]).astype(o_ref.dtype)
        return 0

    lax.fori_loop(0, ATT_BLK // rows, merge, 0)


def _attn_prompt(xp, gain, slopes, *, batch, seq):
    nb = seq // ATT_BLK
    qc, kc, vc = COL_QA // E_AH, COL_KA // E_AH, COL_VA // E_AH
    cur = lambda b, i, h, s: b * nb + i
    prev = lambda b, i, h, s: b * nb + jnp.maximum(i - 1, 0)
    grid_spec = pltpu.PrefetchScalarGridSpec(
        num_scalar_prefetch=1,
        grid=(batch, nb, N_AH),
        in_specs=[
            pl.BlockSpec((ATT_BLK, E_AH), lambda b, i, h, s: (cur(b, i, h, s), qc + h)),
            pl.BlockSpec((ATT_BLK, E_AH), lambda b, i, h, s: (cur(b, i, h, s), kc + h)),
            pl.BlockSpec((ATT_BLK, E_AH), lambda b, i, h, s: (cur(b, i, h, s), vc + h)),
            pl.BlockSpec((ATT_BLK, E_AH), lambda b, i, h, s: (prev(b, i, h, s), kc + h)),
            pl.BlockSpec((ATT_BLK, E_AH), lambda b, i, h, s: (prev(b, i, h, s), vc + h)),
            pl.BlockSpec((1, E_AH), lambda b, i, h, s: (0, h)),
        ],
        out_specs=pl.BlockSpec((ATT_BLK, E_AH), lambda b, i, h, s: (cur(b, i, h, s), h)),
        scratch_shapes=[pltpu.VMEM((3, ATT_BLK, E_AH), F32)] * 3,
    )
    return pl.pallas_call(
        _attn_body,
        grid_spec=grid_spec,
        out_shape=jax.ShapeDtypeStruct((batch * seq, N_AH * E_AH), BF16),
        compiler_params=_cparams(("arbitrary", "arbitrary", "arbitrary")),
    )(slopes, xp, xp, xp, xp, xp, gain)


DEC_NEAR = 8


def _near_multiplicity():
    return [sum(1 for win, d in DILATED_CONFIGS if dist % d == 0 and dist <= win) for dist in range(DEC_NEAR)]


def _decode_body(q_ref, kn_ref, vn_ref, kc_ref, vc_ref, slope_ref, gain_ref,
                 ha_ref, ko_hbm, vo_hbm, ktail, vtail, sem, *, w_buf, s_new):
    b = pl.program_id(0)
    keep = w_buf - s_new
    copies = [
        pltpu.make_async_copy(kc_ref.at[0, 0, pl.ds(s_new, keep)], ko_hbm.at[0, b, pl.ds(0, keep)], sem.at[0]),
        pltpu.make_async_copy(vc_ref.at[0, 0, pl.ds(s_new, keep)], vo_hbm.at[0, b, pl.ds(0, keep)], sem.at[1]),
        pltpu.make_async_copy(kn_ref.at[0], ko_hbm.at[0, b, pl.ds(keep, s_new)], sem.at[2]),
        pltpu.make_async_copy(vn_ref.at[0], vo_hbm.at[0, b, pl.ds(keep, s_new)], sem.at[3]),
    ]
    for cp in copies:
        cp.start()

    ktail[pl.ds(0, DEC_NEAR)] = kc_ref[0, 0, pl.ds(w_buf - DEC_NEAR, DEC_NEAR)]
    vtail[pl.ds(0, DEC_NEAR)] = vc_ref[0, 0, pl.ds(w_buf - DEC_NEAR, DEC_NEAR)]
    ktail[pl.ds(DEC_NEAR, s_new)] = kn_ref[0]
    vtail[pl.ds(DEC_NEAR, s_new)] = vn_ref[0]

    scale = E_AH ** -0.5
    slope = slope_ref[:, 0:1]
    near_i = lax.broadcasted_iota(jnp.int32, (DEC_NEAR, 1, 1), 0)
    near_dist = (DEC_NEAR - 1 - near_i).astype(F32)
    near_mult = jnp.zeros((DEC_NEAR, 1, 1), F32)
    for dist, c in enumerate(_near_multiplicity()):
        near_mult = jnp.where(near_i == DEC_NEAR - 1 - dist, float(c), near_mult)

    def far_part(s, win, d):
        n = N_BACK - (DEC_NEAR - 1) // d
        sl = pl.ds(w_buf + s - win, n, stride=d) if d > 1 else pl.ds(w_buf + s - win, n)
        i = lax.broadcasted_iota(jnp.int32, (n, 1, 1), 0)
        return sl, ((N_BACK - i) * d).astype(F32)

    ones = jnp.ones((E_AH, E_AH), BF16)

    def row_dots(kk, q):
        n = kk.shape[0]
        prod = (kk * q).astype(BF16).reshape(n * N_AH, E_AH)
        return jnp.dot(prod, ones, preferred_element_type=F32).reshape(n, N_AH, E_AH)

    def one_query(s, _):
        q = q_ref[0, s]
        parts = []
        for win, d in DILATED_CONFIGS:
            sl, dist = far_part(s, win, d)
            sc = row_dots(kc_ref[0, 0, sl], q) * scale - slope * dist
            parts.append((sc, None, lambda sl=sl: vc_ref[0, 0, sl]))
        nsl = pl.ds(s + 1, DEC_NEAR)
        sc = row_dots(ktail[nsl], q) * scale - slope * near_dist
        parts.append((sc, near_mult, lambda: vtail[nsl]))
        m = functools.reduce(jnp.maximum, [jnp.max(p[0], axis=0, keepdims=True) for p in parts])
        den = jnp.zeros((1, N_AH, E_AH), F32)
        o = jnp.zeros((1, N_AH, E_AH), F32)
        for sc, mu, load_v in parts:
            p = jnp.exp(sc - m)
            if mu is not None:
                p = p * mu
            den = den + jnp.sum(p, axis=0, keepdims=True)
            o = o + jnp.sum(p * load_v(), axis=0, keepdims=True)
        o = (o / den)[0]
        o = o * lax.rsqrt(jnp.mean(o * o, axis=-1, keepdims=True) + EPS)
        ha_ref[0, s] = o * gain_ref[...]
        return 0

    lax.fori_loop(0, s_new, one_query, 0)
    for cp in copies:
        cp.wait()


def _attn_decode(q3, kn3, vn3, cache_k, cache_v, gain, slopes, *, batch, s_new):
    w_buf = cache_k.shape[2]
    assert w_buf >= max(w for w, _ in DILATED_CONFIGS) and s_new <= DEC_NEAR
    assert all(w // d == N_BACK and d & (d - 1) == 0 for w, d in DILATED_CONFIGS)
    body = functools.partial(_decode_body, w_buf=w_buf, s_new=s_new)
    new_spec = pl.BlockSpec((1, s_new, N_AH, E_AH), lambda b: (b, 0, 0, 0))
    cache_spec = pl.BlockSpec((1, 1, w_buf, N_AH, E_AH), lambda b: (0, b, 0, 0, 0))
    tile_spec = pl.BlockSpec((N_AH, E_AH), lambda b: (0, 0))
    return pl.pallas_call(
        body,
        grid=(batch,),
        in_specs=[new_spec, new_spec, new_spec, cache_spec, cache_spec, tile_spec, tile_spec],
        out_specs=[new_spec, pl.BlockSpec(memory_space=pl.ANY), pl.BlockSpec(memory_space=pl.ANY)],
        out_shape=[
            jax.ShapeDtypeStruct((batch, s_new, N_AH, E_AH), F32),
            jax.ShapeDtypeStruct(cache_k.shape, F32),
            jax.ShapeDtypeStruct(cache_v.shape, F32),
        ],
        scratch_shapes=[pltpu.VMEM((DEC_NEAR + s_new, N_AH, E_AH), F32)] * 2 + [pltpu.SemaphoreType.DMA((4,))],
        compiler_params=_cparams(("arbitrary",)),
    )(q3, kn3, vn3, cache_k, cache_v, slopes, gain)


OP_TM = 512
ROUTE_W = LANE
PACK_ROWS = D_MODEL // LANE


def _layer_norm(z, g, b):
    mu = jnp.mean(z, axis=1, keepdims=True)
    zc = z - mu
    var = jnp.mean(zc * zc, axis=1, keepdims=True)
    return zc * lax.rsqrt(var + EPS) * g + b


def _outproj_body(hm_ref, ha_ref, x_ref, wm_ref, wa_ref, g_ref, b_ref, wrh_ref, wrl_ref, br_ref,
                  x1_ref, x1p_ref, route_ref):
    y = (jnp.dot(hm_ref[...].astype(BF16), wm_ref[...], preferred_element_type=F32)
         + jnp.dot(ha_ref[...].astype(BF16), wa_ref[...], preferred_element_type=F32))
    x1 = _layer_norm(ALPHA * x_ref[...] + y, g_ref[...], b_ref[...])
    x1_ref[...] = x1
    tm = x1.shape[0]
    for j in range(PACK_ROWS):
        x1p_ref[pl.ds(j, tm, stride=PACK_ROWS), :] = x1[:, j * LANE:(j + 1) * LANE]
    x1_hi = x1.astype(BF16)
    x1_lo = (x1 - x1_hi.astype(F32)).astype(BF16)
    logits = (jnp.dot(x1_hi, wrh_ref[...], preferred_element_type=F32)
              + jnp.dot(x1_lo, wrh_ref[...], preferred_element_type=F32)
              + jnp.dot(x1_hi, wrl_ref[...], preferred_element_type=F32)) + br_ref[...]
    lane = lax.broadcasted_iota(jnp.int32, (tm, ROUTE_W), 1)
    lane_f = lane.astype(F32)
    big = float(ROUTE_W)
    gl = jnp.where(lane < N_GROUPS, logits, NEG_INF)
    gmax = jnp.max(gl, axis=1, keepdims=True)
    g_w = 1.0 / jnp.sum(jnp.exp(gl - gmax), axis=1, keepdims=True)
    g_idx = jnp.min(jnp.where(gl == gmax, lane_f, big), axis=1, keepdims=True)
    lo = N_GROUPS + E_PER_GROUP * g_idx
    el = jnp.where(jnp.logical_and(lane_f >= lo, lane_f < lo + E_PER_GROUP), logits, NEG_INF)
    v1 = jnp.max(el, axis=1, keepdims=True)
    i1 = jnp.min(jnp.where(el == v1, lane_f, big), axis=1, keepdims=True)
    el2 = jnp.where(lane_f == i1, NEG_INF, el)
    v2 = jnp.max(el2, axis=1, keepdims=True)
    i2 = jnp.min(jnp.where(el2 == v2, lane_f, big), axis=1, keepdims=True)
    e2 = jnp.exp(v2 - v1)
    w1 = g_w / (1.0 + e2)
    w2 = g_w * e2 / (1.0 + e2)
    route = jnp.where(lane == 0, i1 - N_GROUPS,
                      jnp.where(lane == 1, i2 - N_GROUPS,
                                jnp.where(lane == 2, w1, jnp.where(lane == 3, w2, 0.0))))
    route_ref[...] = route


def _outproj_two_groups(hmp_ref, hap_ref, xp_ref, hms_ref, has_ref, xs_ref, *rest, steps_p):
    i = pl.program_id(0)

    @pl.when(i < steps_p)
    def _():
        _outproj_body(hmp_ref, hap_ref, xp_ref, *rest)

    @pl.when(i >= steps_p)
    def _():
        _outproj_body(hms_ref, has_ref, xs_ref, *rest)


def _outproj(hm_p, ha_p, x_p, hm_s, ha_s, x_s, wo, g, b, wr_hi, wr_lo, br):
    tm = OP_TM
    n_p, n_s = x_p.shape[0], x_s.shape[0]
    assert n_p % tm == 0 and n_s <= tm
    steps_p = n_p // tm
    n = n_p + tm
    pad = lambda a: jnp.pad(a, ((0, tm - n_s), (0, 0)))
    hm_s, ha_s, x_s = pad(hm_s), pad(ha_s), pad(x_s)
    row = lambda i: (i, 0)
    prow = lambda i: (jnp.minimum(i, steps_p - 1), 0)
    fixed = lambda i: (0, 0)
    once = pl.Buffered(1)
    return pl.pallas_call(
        functools.partial(_outproj_two_groups, steps_p=steps_p),
        grid=(steps_p + 1,),
        in_specs=[
            pl.BlockSpec((tm, D_MLSTM), prow),
            pl.BlockSpec((tm, N_AH * E_AH), prow),
            pl.BlockSpec((tm, D_MODEL), prow),
            pl.BlockSpec((tm, D_MLSTM), fixed),
            pl.BlockSpec((tm, N_AH * E_AH), fixed),
            pl.BlockSpec((tm, D_MODEL), fixed),
            pl.BlockSpec((D_MLSTM, D_MODEL), fixed, pipeline_mode=once),
            pl.BlockSpec((N_AH * E_AH, D_MODEL), lambda i: (1, 0), pipeline_mode=once),
            pl.BlockSpec((1, D_MODEL), fixed),
            pl.BlockSpec((1, D_MODEL), fixed),
            pl.BlockSpec((D_MODEL, ROUTE_W), fixed),
            pl.BlockSpec((D_MODEL, ROUTE_W), fixed),
            pl.BlockSpec((1, ROUTE_W), fixed),
        ],
        out_specs=[pl.BlockSpec((tm, D_MODEL), row), pl.BlockSpec((tm * PACK_ROWS, LANE), row),
                   pl.BlockSpec((tm, ROUTE_W), row)],
        out_shape=[jax.ShapeDtypeStruct((n, D_MODEL), F32), jax.ShapeDtypeStruct((n * PACK_ROWS, LANE), F32),
                   jax.ShapeDtypeStruct((n, ROUTE_W), F32)],
        compiler_params=_cparams(("arbitrary",)),
    )(hm_p, ha_p, x_p, hm_s, ha_s, x_s, wo, wo, g, b, wr_hi, wr_lo, br)


MOE_R = 128
MOE_MAXB = 6
MOE_FC = 256
MOE_NC = D_FF // MOE_FC
MOE_ISSUE_UNROLL = 32


def _moe_body(ex_ref, row0_ref, nb_ref, src_ref, x1p_hbm, wg_ref, wu_ref, wd_ref, ys_hbm,
              ubuf, xbuf, acc, sem_in, sem_out):
    t = pl.program_id(0)
    c = pl.program_id(1)
    n_items = pl.num_programs(0)
    slot = t % 2
    nb = nb_ref[t]

    def rows(base, r):
        return pl.ds(pl.multiple_of(base + r * MOE_R, MOE_R), MOE_R)

    def tile_rows(i):
        return pl.ds(pl.multiple_of(i * PACK_ROWS, PACK_ROWS), PACK_ROWS)

    def out_copy(tt, sl, r):
        return pltpu.make_async_copy(acc.at[sl, rows(0, r), :], ys_hbm.at[rows(row0_ref[tt], r), :], sem_out.at[sl])

    def for_blocks(n, fn):
        def body(r, _):
            fn(r)
            return 0

        lax.fori_loop(0, n, body, 0)

    def n_in(tt):
        return jnp.maximum(nb_ref[tt], 0)

    def n_out(tt):
        return jnp.abs(nb_ref[tt])

    def start_gather(tt, sl):
        base = row0_ref[tt]

        def group(gi):
            for k in range(MOE_ISSUE_UNROLL):
                i = gi * MOE_ISSUE_UNROLL + k
                tok = src_ref[base + i]
                pltpu.make_async_copy(x1p_hbm.at[tile_rows(tok), :], ubuf.at[sl, tile_rows(i), :],
                                      sem_in.at[sl]).start(priority=k % 2)

        for_blocks(n_in(tt) * (MOE_R // MOE_ISSUE_UNROLL), group)

    def wait_gather(tt, sl):
        def block(r):
            span = pl.ds(pl.multiple_of(r * (MOE_R * PACK_ROWS), MOE_R * PACK_ROWS), MOE_R * PACK_ROWS)
            pltpu.make_async_copy(x1p_hbm.at[pl.ds(0, MOE_R * PACK_ROWS), :], ubuf.at[sl, span, :], sem_in.at[sl]).wait()

        for_blocks(n_in(tt), block)

    def unpack_block(r):
        for j in range(PACK_ROWS):
            start = r * (MOE_R * PACK_ROWS) + j
            u = ubuf[slot, pl.ds(start, MOE_R, stride=PACK_ROWS), :]
            xbuf[rows(0, r), j * LANE:(j + 1) * LANE] = u.astype(BF16)

    @pl.when(c == 0)
    def _begin():
        @pl.when(t == 0)
        def _():
            start_gather(0, 0)

        @pl.when(t + 1 < n_items)
        def _():
            start_gather(t + 1, 1 - slot)

        wait_gather(t, slot)
        for_blocks(n_in(t), unpack_block)

        @pl.when(t >= 2)
        def _():
            for_blocks(n_out(t - 2), lambda r: out_copy(t - 2, slot, r).wait())

        @pl.when(nb < 0)
        def _():
            acc[slot] = jnp.zeros(acc.shape[1:], F32)

    @pl.when(nb > 0)
    def _compute():
        wg = wg_ref[...].astype(BF16)
        wu = wu_ref[...].astype(BF16)
        wd = wd_ref[...].astype(BF16)

        def piece(size):
            rs = pl.ds(0, size)
            x = xbuf[rs, :]
            gt = jnp.dot(x, wg, preferred_element_type=F32)
            up = jnp.dot(x, wu, preferred_element_type=F32)
            hid = (gt * jax.nn.sigmoid(gt) * up).astype(BF16)
            y = jnp.dot(hid, wd, preferred_element_type=F32)

            @pl.when(c == 0)
            def _():
                acc[slot, rs, :] = y

            @pl.when(c > 0)
            def _():
                acc[slot, rs, :] = acc[slot, rs, :] + y

        for blocks in range(1, MOE_MAXB + 1):
            @pl.when(nb == blocks)
            def _(blocks=blocks):
                piece(blocks * MOE_R)

    @pl.when(c == MOE_NC - 1)
    def _end():
        for_blocks(n_out(t), lambda r: out_copy(t, slot, r).start())

        @pl.when(t == n_items - 1)
        def _():
            for_blocks(n_out(t), lambda r: out_copy(t, slot, r).wait())

            @pl.when(t >= 1)
            def _():
                for_blocks(n_out(t - 1), lambda r: out_copy(t - 1, 1 - slot, r).wait())


def _moe(item_e, item_row0, item_nb, src, x1p, w_gate, w_up, w_down, n_items, n_rows):
    def chunk(t, c, n):
        return jnp.where(n[t] > 0, c, MOE_NC - 1)

    grid_spec = pltpu.PrefetchScalarGridSpec(
        num_scalar_prefetch=4,
        grid=(n_items, MOE_NC),
        in_specs=[
            pl.BlockSpec(memory_space=pl.ANY),
            pl.BlockSpec((None, D_MODEL, MOE_FC), lambda t, c, e, r, n, s: (e[t], 0, chunk(t, c, n))),
            pl.BlockSpec((None, D_MODEL, MOE_FC), lambda t, c, e, r, n, s: (e[t], 0, chunk(t, c, n))),
            pl.BlockSpec((None, MOE_FC, D_MODEL), lambda t, c, e, r, n, s: (e[t], chunk(t, c, n), 0)),
        ],
        out_specs=pl.BlockSpec(memory_space=pl.ANY),
        scratch_shapes=[
            pltpu.VMEM((2, MOE_MAXB * MOE_R * PACK_ROWS, LANE), F32),
            pltpu.VMEM((MOE_MAXB * MOE_R, D_MODEL), BF16),
            pltpu.VMEM((2, MOE_MAXB * MOE_R, D_MODEL), F32),
            pltpu.SemaphoreType.DMA((2,)),
            pltpu.SemaphoreType.DMA((2,)),
        ],
    )
    return pl.pallas_call(
        _moe_body,
        grid_spec=grid_spec,
        out_shape=jax.ShapeDtypeStruct((n_rows, D_MODEL), F32),
        compiler_params=_cparams(("arbitrary", "arbitrary")),
    )(item_e, item_row0, item_nb, src, x1p, w_gate, w_up, w_down)


COMB_TM = 128
COMB_ISSUE_UNROLL = 32


def _comb_issue(pos_ref, ys_hbm, buf, sem, tile, slot, tm):
    def body(g, _):
        for u in range(COMB_ISSUE_UNROLL):
            r = g * COMB_ISSUE_UNROLL + u
            for k in range(2):
                p = pos_ref[2 * (tile * tm + r) + k]
                pltpu.make_async_copy(ys_hbm.at[pl.ds(p, 1), :], buf.at[slot, k, pl.ds(r, 1), :],
                                      sem.at[slot]).start(priority=k)
        return 0

    lax.fori_loop(0, tm // COMB_ISSUE_UNROLL, body, 0)


def _combine_body(pos_ref, ys_hbm, x1_ref, route_ref, g_ref, b_ref, o_ref, buf, sem, *, tm):
    i = pl.program_id(0)
    n = pl.num_programs(0)
    slot = i % 2

    @pl.when(i == 0)
    def _():
        _comb_issue(pos_ref, ys_hbm, buf, sem, 0, 0, tm)

    @pl.when(i + 1 < n)
    def _():
        _comb_issue(pos_ref, ys_hbm, buf, sem, i + 1, 1 - slot, tm)

    for k in range(2):
        pltpu.make_async_copy(ys_hbm.at[pl.ds(0, tm), :], buf.at[slot, k], sem.at[slot]).wait()
    f = buf[slot, 0] * route_ref[:, 2:3] + buf[slot, 1] * route_ref[:, 3:4]
    o_ref[...] = _layer_norm(ALPHA * x1_ref[...] + f, g_ref[...], b_ref[...])


def _combine(pos, ys, x1, route, g, b, *, row0, n):
    d = x1.shape[1]
    tm = COMB_TM
    assert row0 % tm == 0 and n % tm == 0
    off = row0 // tm
    grid_spec = pltpu.PrefetchScalarGridSpec(
        num_scalar_prefetch=1,
        grid=(n // tm,),
        in_specs=[
            pl.BlockSpec(memory_space=pl.ANY),
            pl.BlockSpec((tm, d), lambda i, p: (i + off, 0)),
            pl.BlockSpec((tm, ROUTE_W), lambda i, p: (i + off, 0)),
            pl.BlockSpec((1, d), lambda i, p: (0, 0)),
            pl.BlockSpec((1, d), lambda i, p: (0, 0)),
        ],
        out_specs=pl.BlockSpec((tm, d), lambda i, p: (i, 0)),
        scratch_shapes=[pltpu.VMEM((2, 2, tm, d), F32), pltpu.SemaphoreType.DMA((2,))],
    )
    return pl.pallas_call(
        functools.partial(_combine_body, tm=tm),
        grid_spec=grid_spec,
        out_shape=jax.ShapeDtypeStruct((n, d), F32),
        compiler_params=_cparams(("arbitrary",)),
    )(pos, ys, x1, route, g, b)


def _dispatch_plan(eid, n_items, n_rows):
    p_total = eid.shape[0]
    blk = 128
    assert p_total % blk == 0
    onehot = (eid[:, None] == jnp.arange(N_EXPERTS, dtype=jnp.int32)[None, :]).astype(F32)
    counts = jnp.sum(onehot, axis=0).astype(jnp.int32)
    ohb = onehot.reshape(p_total // blk, blk, N_EXPERTS)
    earlier = (jnp.arange(blk)[:, None] > jnp.arange(blk)[None, :]).astype(F32)
    within = jnp.einsum("ij,bjk->bik", earlier, ohb, precision=lax.Precision.HIGHEST)
    blk_tot = jnp.sum(ohb, axis=1)
    blk_off = jnp.cumsum(blk_tot, axis=0) - blk_tot
    rank = jnp.sum((within + blk_off[:, None, :]) * ohb, axis=2).reshape(p_total).astype(jnp.int32)
    nblk = (counts + MOE_R - 1) // MOE_R
    seg_start = (jnp.cumsum(nblk) - nblk) * MOE_R
    pos = seg_start[eid] + rank
    src = jnp.zeros((n_rows,), jnp.int32).at[pos].set(jnp.arange(p_total, dtype=jnp.int32) // 2)
    items_per_e = (nblk + MOE_MAXB - 1) // MOE_MAXB
    item_end = jnp.cumsum(items_per_e)
    item_start = item_end - items_per_e
    t = jnp.arange(n_items, dtype=jnp.int32)
    e_t = jnp.minimum(jnp.sum((item_end[None, :] <= t[:, None]).astype(jnp.int32), axis=1), N_EXPERTS - 1)
    live = t < item_end[-1]
    local = t - item_start[e_t]
    used = jnp.sum(nblk)
    idle0 = used + (t - item_end[-1]) * MOE_MAXB
    nz_t = jnp.clip(n_rows // MOE_R - idle0, 0, MOE_MAXB)
    nb_t = jnp.where(live, jnp.clip(nblk[e_t] - local * MOE_MAXB, 0, MOE_MAXB), -nz_t)
    row0_t = jnp.where(live, seg_start[e_t] + local * (MOE_MAXB * MOE_R),
                       jnp.minimum(idle0, n_rows // MOE_R - 1) * MOE_R)
    last_e = e_t[jnp.maximum(item_end[-1] - 1, 0)]
    e_t = jnp.where(live, e_t, last_e)
    return pos.astype(jnp.int32), src, e_t.astype(jnp.int32), row0_t.astype(jnp.int32), nb_t.astype(jnp.int32)


def _alibi_slopes():
    return jnp.asarray([2.0 ** (-8.0 * (h + 1) / N_AH) for h in range(N_AH)], dtype=F32)


def kernel(x_prompt, x_sample, state_conv, state_mlstm_C, state_mlstm_n, state_mlstm_m, cache_win_k, cache_win_v, w_in, b_gate, conv_w, conv_b, mh_gain, att_gain, w_out, ln1_g, ln1_b, w_group, b_group, w_router, b_router, w_gate, w_up, w_down, ln2_g, ln2_b):
    bp, tp, d = x_prompt.shape
    bs, ts, _ = x_sample.shape
    assert d == D_MODEL and w_in.shape[0] == 1 and tp % ATT_BLK == 0 and ts >= CONV_W - 1
    w_buf = cache_win_k.shape[2]
    n_p, n_s = bp * tp, bs * ts
    slopes = _alibi_slopes()

    wi = w_in[0]
    g0 = 4 * D_MLSTM
    g1 = g0 + 2 * N_MH
    w_pack = jnp.concatenate(
        [wi[:, :g0], wi[:, g1:], wi[:, g0:g1], jnp.zeros((d, LANE - 2 * N_MH), F32)], axis=1).astype(BF16)

    xp2 = x_prompt.reshape(n_p, d)
    xs2 = x_sample.reshape(n_s, d)
    proj_p = _in_proj(xp2, w_pack, 512)
    proj_s = _in_proj(xs2, w_pack, n_s)

    cw = conv_w[0]
    cb = conv_b[0][None, :]
    mh_g = mh_gain[0][None, :]
    att_g = att_gain[0][None, :]
    bg = b_gate[0]

    def gates_time_major(proj, batch, seq, pad_to):
        gt = proj[:, COL_G:COL_G + 2 * N_MH].reshape(batch, seq, 2, N_MH).transpose(0, 3, 2, 1)
        if pad_to > seq:
            gt = jnp.pad(gt, ((0, 0), (0, 0), (0, 0), (0, pad_to - seq)))
        return gt

    lc_p = 256
    hm_p, c_p, n_pp, m_p = _mlstm(
        proj_p, gates_time_major(proj_p, bp, tp, tp), jnp.zeros((bp, CONV_W - 1, 2 * D_MLSTM), F32), cw, cb,
        jnp.zeros((bp, N_MH, E_MH, E_MH), F32), jnp.zeros((bp, N_MH, 1, E_MH), F32),
        jnp.zeros((bp, N_MH, 1, 1), F32), mh_g, bg, batch=bp, seq=tp, lb=lc_p, lc=lc_p, hb=N_MH, out_dtype=BF16)
    ha_p = _attn_prompt(proj_p, att_g, slopes, batch=bp, seq=tp)

    lc_s = 16
    hm_s, c_s, n_ss, m_s = _mlstm(
        proj_s, gates_time_major(proj_s, bs, ts, lc_s), state_conv[0], cw, cb,
        state_mlstm_C[0], state_mlstm_n[0][:, :, None, :], state_mlstm_m[0][:, :, None, None],
        mh_g, bg, batch=bs, seq=ts, lb=ts, lc=lc_s, hb=N_MH, out_dtype=F32)
    new_rows = lambda col: proj_s[:, col:col + N_AH * E_AH].reshape(bs, ts, N_AH, E_AH)
    ha_s, wk_s, wv_s = _attn_decode(
        new_rows(COL_QA), new_rows(COL_KA), new_rows(COL_VA), cache_win_k, cache_win_v,
        att_gain[0].reshape(N_AH, E_AH), jnp.broadcast_to(slopes[:, None], (N_AH, E_AH)), batch=bs, s_new=ts)
    ha_s = ha_s.reshape(n_s, N_AH * E_AH)

    n_all = n_p + n_s
    wo = w_out[0].astype(BF16)
    w_r = jnp.concatenate(
        [w_group[0], w_router[0].transpose(1, 0, 2).reshape(d, N_EXPERTS),
         jnp.zeros((d, ROUTE_W - N_GROUPS - N_EXPERTS), F32)], axis=1)
    b_r = jnp.concatenate(
        [b_group[0], b_router[0].reshape(N_EXPERTS), jnp.zeros((ROUTE_W - N_GROUPS - N_EXPERTS,), F32)])[None, :]
    wr_hi = w_r.astype(BF16)
    wr_lo = (w_r - wr_hi.astype(F32)).astype(BF16)
    ln1 = (ln1_g[0][None, :], ln1_b[0][None, :])
    x1, x1p, route = _outproj(hm_p, ha_p, xp2, hm_s, ha_s, xs2, wo, *ln1, wr_hi, wr_lo, b_r)

    p_total = 2 * n_all
    n_rows = ((p_total + N_EXPERTS * (MOE_R - 1)) // MOE_R + 1) * MOE_R
    n_items = N_EXPERTS + n_rows // (MOE_R * MOE_MAXB)
    eid = route[:n_all, 0:2].astype(jnp.int32).reshape(p_total)
    pos, src, item_e, item_row0, item_nb = _dispatch_plan(eid, n_items, n_rows)
    ys = _moe(item_e, item_row0, item_nb, src, x1p, w_gate[0], w_up[0], w_down[0], n_items, n_rows)
    ln2 = (ln2_g[0][None, :], ln2_b[0][None, :])
    y_p = _combine(pos[:2 * n_p], ys, x1, route, *ln2, row0=0, n=n_p).reshape(bp, tp, d)
    y_s = _combine(pos[2 * n_p:], ys, x1, route, *ln2, row0=n_p, n=n_s).reshape(bs, ts, d)

    def tail_rows(proj, batch, seq, col, width, rows):
        return proj.reshape(batch, seq, N_PROJ)[:, seq - rows:, col:col + width]

    win = min(w_buf, tp)
    p_conv = tail_rows(proj_p, bp, tp, COL_QM, 2 * D_MLSTM, CONV_W - 1)[None]
    p_wk = tail_rows(proj_p, bp, tp, COL_KA, N_AH * E_AH, win).reshape(1, bp, win, N_AH, E_AH)
    p_wv = tail_rows(proj_p, bp, tp, COL_VA, N_AH * E_AH, win).reshape(1, bp, win, N_AH, E_AH)
    s_conv = tail_rows(proj_s, bs, ts, COL_QM, 2 * D_MLSTM, CONV_W - 1)[None]
    return (y_p, y_s,
            p_conv, c_p[None], n_pp[:, :, 0, :][None], m_p[:, :, 0, 0][None], p_wk, p_wv,
            s_conv, c_s[None], n_ss[:, :, 0, :][None], m_s[:, :, 0, 0][None], wk_s, wv_s)
```

```python
import functools
import math

import jax
import jax.numpy as jnp
from jax import lax
from jax.experimental import pallas as pl
from jax.experimental.pallas import tpu as pltpu

F32 = jnp.float32
BF16 = jnp.bfloat16
NEG_INF = float("-inf")

D_MODEL = 2048
D_MLSTM = 1024
N_MH = 4
E_MH = 256
N_AH = 8
E_AH = 128
DILATED_CONFIGS = ((128, 1), (512, 4), (2048, 16))
N_BACK = 128
BAND = 128
CONV_W = 4
N_GROUPS = 4
E_PER_GROUP = 8
N_EXPERTS = 32
D_FF = 1024
EPS = 1e-5
ALPHA = 2.0 ** 0.25

LANE = 128
SUBLANE = 8

COL_QM, COL_KM, COL_VM, COL_OM = 0, 1024, 2048, 3072
COL_QA, COL_KA, COL_VA, COL_G = 4096, 5120, 6144, 7168
N_PROJ = 7296
PROJ_TN = 2432

VMEM_LIMIT = 56 * 1024 * 1024


def _cparams(sem):
    return pltpu.CompilerParams(dimension_semantics=sem, vmem_limit_bytes=VMEM_LIMIT)


def _proj_body(x_ref, w_ref, o_ref):
    o_ref[...] = jnp.dot(x_ref[...].astype(BF16), w_ref[...], preferred_element_type=F32)


def _in_proj(x, w, tm):
    n = x.shape[0]
    return pl.pallas_call(
        _proj_body,
        grid=(N_PROJ // PROJ_TN, n // tm),
        in_specs=[pl.BlockSpec((tm, D_MODEL), lambda j, i: (i, 0)),
                  pl.BlockSpec((D_MODEL, PROJ_TN), lambda j, i: (0, j))],
        out_specs=pl.BlockSpec((tm, PROJ_TN), lambda j, i: (i, j)),
        out_shape=jax.ShapeDtypeStruct((n, N_PROJ), F32),
        compiler_params=_cparams(("arbitrary", "arbitrary")),
    )(x, w)


def _mlstm_body(bg_ref, xq_ref, xk_ref, v_ref, om_ref, g_ref, hq_ref, hk_ref, cwq_ref, cwk_ref,
                cbq_ref, cbk_ref, c0_ref, n0_ref, m0_ref, gain_ref,
                hm_ref, cout_ref, nout_ref, mout_ref,
                caug, m_s, uq, uk, vbuf, *, lb, lc, t_valid, hb):
    hg = pl.program_id(1)
    c = pl.program_id(2)
    nc = pl.num_programs(2)
    e = E_MH
    heads = range(hb)
    cols = lambda i: slice(i * e, (i + 1) * e)

    row_e = lax.broadcasted_iota(jnp.int32, (e, e), 0)
    col_e = lax.broadcasted_iota(jnp.int32, (e, e), 1)
    eye_e = row_e == col_e

    @pl.when(c == 0)
    def _init():
        for i in heads:
            caug[i, :, 0:e] = c0_ref[i]
            ncol = jnp.sum(jnp.where(eye_e, n0_ref[i], 0.0), axis=1, keepdims=True)
            caug[i, :, e:e + LANE] = jnp.broadcast_to(ncol, (e, LANE))
            m_s[i] = m0_ref[i]
        uq[...] = jnp.zeros(uq.shape, F32)
        uk[...] = jnp.zeros(uk.shape, F32)
        uq[pl.ds(SUBLANE - (CONV_W - 1), CONV_W - 1), :] = hq_ref[...]
        uk[pl.ds(SUBLANE - (CONV_W - 1), CONV_W - 1), :] = hk_ref[...]
        if lb != lc:
            vbuf[...] = jnp.zeros(vbuf.shape, F32)

    uq[pl.ds(SUBLANE, lb), :] = xq_ref[...]
    uk[pl.ds(SUBLANE, lb), :] = xk_ref[...]

    def conv_silu(u, cw_ref, cb_ref):
        acc = cb_ref[...]
        for j in range(CONV_W):
            acc = acc + u[pl.ds(SUBLANE - (CONV_W - 1) + j, lc), :] * cw_ref[j:j + 1, :]
        return acc * jax.nn.sigmoid(acc)

    q_all = conv_silu(uq, cwq_ref, cbq_ref)
    k_all = conv_silu(uk, cwk_ref, cbk_ref) * (e ** -0.5)
    if lb != lc:
        vbuf[pl.ds(0, lb), :] = v_ref[...]
        v_all = vbuf[...]
    else:
        v_all = v_ref[...]

    tq = uq[pl.ds(lc + SUBLANE - (CONV_W - 1), CONV_W - 1), :]
    tk = uk[pl.ds(lc + SUBLANE - (CONV_W - 1), CONV_W - 1), :]
    uq[pl.ds(SUBLANE - (CONV_W - 1), CONV_W - 1), :] = tq
    uk[pl.ds(SUBLANE - (CONV_W - 1), CONV_W - 1), :] = tk

    lane_t = lax.broadcasted_iota(jnp.int32, (1, lc), 1)
    valid = (lane_t + c * lc) < t_valid
    row_l = lax.broadcasted_iota(jnp.int32, (lc, lc), 0)
    col_l = lax.broadcasted_iota(jnp.int32, (lc, lc), 1)
    causal = col_l <= row_l
    eye_l = col_l == row_l
    nt = (((1,), (1,)), ((), ()))
    tn = (((0,), (0,)), ((), ()))

    st = []
    for i in heads:
        h = hg * hb + i
        gi = g_ref[i, 0:1, :] + bg_ref[h]
        gf = g_ref[i, 1:2, :] + bg_ref[N_MH + h]
        lf = -(jnp.maximum(-gf, 0.0) + jnp.log1p(jnp.exp(-jnp.abs(gf))))
        ig_row = jnp.where(valid, gi, NEG_INF)
        lf_row = jnp.where(valid, lf, 0.0)
        b_col = jnp.sum(jnp.where(causal, lf_row, 0.0), axis=1, keepdims=True)
        b_row = jnp.sum(jnp.where(eye_l, b_col, 0.0), axis=0, keepdims=True)
        m_prev = m_s[i]
        dlog = jnp.where(causal, b_col - b_row + ig_row, NEG_INF)
        a_col = b_col + m_prev
        m_t = jnp.maximum(a_col, jnp.max(dlog, axis=1, keepdims=True))
        st.append(dict(ig_row=ig_row, b_row=b_row, m_prev=m_prev, m_t=m_t,
                       w_intra=jnp.exp(dlog - m_t), w_inter=jnp.exp(a_col - m_t),
                       qb=q_all[:, cols(i)].astype(BF16), kb=k_all[:, cols(i)].astype(BF16),
                       vb=v_all[:, cols(i)].astype(BF16)))

    for i in heads:
        d = st[i]
        d["s"] = lax.dot_general(d["qb"], d["kb"], nt, preferred_element_type=F32) * d["w_intra"]
        d["qc"] = jnp.dot(d["qb"], caug[i].astype(BF16), preferred_element_type=F32)

    for i in heads:
        d = st[i]
        num = jnp.dot(d["s"].astype(BF16), d["vb"], preferred_element_type=F32) + d["w_inter"] * d["qc"][:, 0:e]
        den = jnp.sum(d["s"], axis=1, keepdims=True) + d["w_inter"] * d["qc"][:, e:e + 1]
        hh = num / jnp.maximum(jnp.abs(den), jnp.exp(-d["m_t"]))
        hh = hh - jnp.mean(hh, axis=1, keepdims=True)
        hh = hh * lax.rsqrt(jnp.mean(hh * hh, axis=1, keepdims=True) + EPS)
        if lb != lc:
            hh = hh[0:lb, :]
        hm = hh * gain_ref[:, cols(i)] * jax.nn.sigmoid(om_ref[:, cols(i)])
        hm_ref[:, cols(i)] = hm.astype(hm_ref.dtype)

    for i in heads:
        d = st[i]
        b_last = d["b_row"][:, lc - 1:lc]
        logw_row = b_last - d["b_row"] + d["ig_row"]
        m_new = jnp.maximum(b_last + d["m_prev"], jnp.max(logw_row, axis=1, keepdims=True))
        w_row = jnp.exp(logw_row - m_new)
        w_col = jnp.sum(jnp.where(eye_l, w_row, 0.0), axis=1, keepdims=True)
        decay = jnp.exp(b_last + d["m_prev"] - m_new)
        kw = (k_all[:, cols(i)] * w_col).astype(BF16)
        vaug = jnp.concatenate([d["vb"], jnp.ones((lc, LANE), BF16)], axis=1)
        upd = lax.dot_general(kw, vaug, tn, preferred_element_type=F32)
        caug[i] = decay * caug[i] + upd
        m_s[i] = m_new

    @pl.when(c == nc - 1)
    def _fin():
        for i in heads:
            cout_ref[i] = caug[i, :, 0:e]
            nout_ref[i] = jnp.sum(jnp.where(eye_e, caug[i, :, e:e + 1], 0.0), axis=0, keepdims=True)
            mout_ref[i] = jnp.broadcast_to(m_s[i], (1, LANE))


def _mlstm(xp, gates_t, hist, conv_w, conv_b, c0, n0, m0, gain, b_gate, *, batch, seq, lb, lc, hb, out_dtype):
    nc = seq // lb
    e = E_MH
    w = hb * e
    ng = N_MH // hb
    assert N_MH % hb == 0
    rb = lambda b, h, c, bg: b * nc + c
    qcol, kcol, vcol, ocol = COL_QM // w, COL_KM // w, COL_VM // w, COL_OM // w
    body = functools.partial(_mlstm_body, lb=lb, lc=lc, t_valid=seq if lb == lc else lb, hb=hb)
    grid_spec = pltpu.PrefetchScalarGridSpec(
        num_scalar_prefetch=1,
        grid=(batch, ng, nc),
        in_specs=[
            pl.BlockSpec((lb, w), lambda b, h, c, bg: (rb(b, h, c, bg), qcol + h)),
            pl.BlockSpec((lb, w), lambda b, h, c, bg: (rb(b, h, c, bg), kcol + h)),
            pl.BlockSpec((lb, w), lambda b, h, c, bg: (rb(b, h, c, bg), vcol + h)),
            pl.BlockSpec((lb, w), lambda b, h, c, bg: (rb(b, h, c, bg), ocol + h)),
            pl.BlockSpec((None, hb, 2, lc), lambda b, h, c, bg: (b, h, 0, c)),
            pl.BlockSpec((None, CONV_W - 1, w), lambda b, h, c, bg: (b, 0, h)),
            pl.BlockSpec((None, CONV_W - 1, w), lambda b, h, c, bg: (b, 0, ng + h)),
            pl.BlockSpec((CONV_W, w), lambda b, h, c, bg: (0, h)),
            pl.BlockSpec((CONV_W, w), lambda b, h, c, bg: (0, ng + h)),
            pl.BlockSpec((1, w), lambda b, h, c, bg: (0, h)),
            pl.BlockSpec((1, w), lambda b, h, c, bg: (0, ng + h)),
            pl.BlockSpec((None, hb, e, e), lambda b, h, c, bg: (b, h, 0, 0)),
            pl.BlockSpec((None, hb, 1, e), lambda b, h, c, bg: (b, h, 0, 0)),
            pl.BlockSpec((None, hb, 1, 1), lambda b, h, c, bg: (b, h, 0, 0)),
            pl.BlockSpec((1, w), lambda b, h, c, bg: (0, h)),
        ],
        out_specs=[
            pl.BlockSpec((lb, w), lambda b, h, c, bg: (rb(b, h, c, bg), h)),
            pl.BlockSpec((None, hb, e, e), lambda b, h, c, bg: (b, h, 0, 0)),
            pl.BlockSpec((None, hb, 1, e), lambda b, h, c, bg: (b, h, 0, 0)),
            pl.BlockSpec((None, hb, 1, LANE), lambda b, h, c, bg: (b, h, 0, 0)),
        ],
        scratch_shapes=[
            pltpu.VMEM((hb, e, e + LANE), F32),
            pltpu.VMEM((hb, 1, 1), F32),
            pltpu.VMEM((lc + 2 * SUBLANE, w), F32),
            pltpu.VMEM((lc + 2 * SUBLANE, w), F32),
            pltpu.VMEM((lc, w), F32),
        ],
    )
    return pl.pallas_call(
        body,
        grid_spec=grid_spec,
        out_shape=[
            jax.ShapeDtypeStruct((batch * seq, D_MLSTM), out_dtype),
            jax.ShapeDtypeStruct((batch, N_MH, e, e), F32),
            jax.ShapeDtypeStruct((batch, N_MH, 1, e), F32),
            jax.ShapeDtypeStruct((batch, N_MH, 1, LANE), F32),
        ],
        compiler_params=_cparams(("arbitrary", "arbitrary", "arbitrary")),
    )(b_gate, xp, xp, xp, xp, gates_t, hist, hist, conv_w, conv_w, conv_b, conv_b, c0, n0, m0, gain)


ATT_BLK = 2048
ATT_GROUP = 16


def _attn_body(sl_ref, q_ref, kc_ref, vc_ref, kp_ref, vp_ref, gain_ref, o_ref, o_s, m_s, l_s):
    blk = pl.program_id(1)
    h = pl.program_id(2)
    slope = sl_ref[h]
    scale = E_AH ** -0.5
    qi = lax.broadcasted_iota(jnp.int32, (BAND, BAND), 0)
    ci = lax.broadcasted_iota(jnp.int32, (BAND, BAND), 1)
    dist_prev = (BAND + qi - ci).astype(F32)
    dist_cur = (qi - ci).astype(F32)
    ok_prev = ci >= qi
    ok_cur = ci <= qi

    def run_units(cfg, d, specs):
        sd = slope * float(d)
        bm_cur = jnp.where(ok_cur, -sd * dist_cur, NEG_INF)
        nt = (((1,), (1,)), ((), ()))
        scores = []
        for prev_pen, q_sl, kp_src, kc_src, _, _ in specs:
            bm_prev = jnp.where(ok_prev, prev_pen - sd * dist_prev, NEG_INF)
            qv = q_ref[q_sl, :].astype(BF16)
            kp = kp_src[0][kp_src[1], :].astype(BF16)
            kc = kc_src[0][kc_src[1], :].astype(BF16)
            s_p = lax.dot_general(qv, kp, nt, preferred_element_type=F32) * scale + bm_prev
            s_c = lax.dot_general(qv, kc, nt, preferred_element_type=F32) * scale + bm_cur
            scores.append((s_p, s_c))
        probs = []
        for s_p, s_c in scores:
            m = jnp.max(jnp.maximum(s_p, s_c), axis=1, keepdims=True)
            p_p = jnp.exp(s_p - m)
            p_c = jnp.exp(s_c - m)
            l = jnp.sum(p_p + p_c, axis=1, keepdims=True)
            probs.append((m, l, p_p.astype(BF16), p_c.astype(BF16)))
        for (_, q_sl, _, _, vp_src, vc_src), (m, l, p_p, p_c) in zip(specs, probs):
            vp = vp_src[0][vp_src[1], :].astype(BF16)
            vc = vc_src[0][vc_src[1], :].astype(BF16)
            o = jnp.dot(p_p, vp, preferred_element_type=F32) + jnp.dot(p_c, vc, preferred_element_type=F32)
            o_s[cfg, q_sl, :] = o
            m_s[cfg, q_sl, :] = jnp.broadcast_to(m, (BAND, E_AH))
            l_s[cfg, q_sl, :] = jnp.broadcast_to(l, (BAND, E_AH))

    first_pen = jnp.where(blk == 0, NEG_INF, 0.0)

    for cfg, (win, d) in enumerate(DILATED_CONFIGS):
        span = BAND * d
        n_u = ATT_BLK // span

        def sl(r, u, _d=d, _span=span):
            if _d == 1:
                if isinstance(u, int):
                    return pl.ds(u * _span, BAND)
                return pl.ds(pl.multiple_of(u * _span, BAND), BAND)
            return pl.ds(r + u * _span, BAND, stride=_d)

        def head_unit(r, d=d, span=span, sl=sl):
            cur = sl(r, 0)
            prv = sl(r + ATT_BLK - span, 0) if d > 1 else pl.ds(ATT_BLK - span, BAND)
            return (first_pen, cur, (kp_ref, prv), (kc_ref, cur), (vp_ref, prv), (vc_ref, cur))

        def tail_unit(r, u, sl=sl):
            cur = sl(r, u)
            prv = sl(r, u - 1)
            return (0.0, cur, (kc_ref, prv), (kc_ref, cur), (vc_ref, prv), (vc_ref, cur))

        specs = [head_unit(r) if u == 0 else tail_unit(r, u) for u in range(n_u) for r in range(d)]
        for g in range(0, len(specs), ATT_GROUP):
            run_units(cfg, d, specs[g:g + ATT_GROUP])

    rows = 256

    def merge(i, _):
        rs = pl.ds(pl.multiple_of(i * rows, rows), rows)
        m0, m1, m2 = m_s[0, rs, :], m_s[1, rs, :], m_s[2, rs, :]
        m_all = jnp.maximum(jnp.maximum(m0, m1), m2)
        e0, e1, e2 = jnp.exp(m0 - m_all), jnp.exp(m1 - m_all), jnp.exp(m2 - m_all)
        num = e0 * o_s[0, rs, :] + e1 * o_s[1, rs, :] + e2 * o_s[2, rs, :]
        den = e0 * l_s[0, rs, :] + e1 * l_s[1, rs, :] + e2 * l_s[2, rs, :]
        ha = num / den
        ha = ha * lax.rsqrt(jnp.mean(ha * ha, axis=1, keepdims=True) + EPS)
        o_ref[rs, :] = (ha * gain_ref[...]).astype(o_ref.dtype)
        return 0

    lax.fori_loop(0, ATT_BLK // rows, merge, 0)


def _attn_prompt(xp, gain, slopes, *, batch, seq):
    nb = seq // ATT_BLK
    qc, kc, vc = COL_QA // E_AH, COL_KA // E_AH, COL_VA // E_AH
    cur = lambda b, i, h, s: b * nb + i
    prev = lambda b, i, h, s: b * nb + jnp.maximum(i - 1, 0)
    grid_spec = pltpu.PrefetchScalarGridSpec(
        num_scalar_prefetch=1,
        grid=(batch, nb, N_AH),
        in_specs=[
            pl.BlockSpec((ATT_BLK, E_AH), lambda b, i, h, s: (cur(b, i, h, s), qc + h)),
            pl.BlockSpec((ATT_BLK, E_AH), lambda b, i, h, s: (cur(b, i, h, s), kc + h)),
            pl.BlockSpec((ATT_BLK, E_AH), lambda b, i, h, s: (cur(b, i, h, s), vc + h)),
            pl.BlockSpec((ATT_BLK, E_AH), lambda b, i, h, s: (prev(b, i, h, s), kc + h)),
            pl.BlockSpec((ATT_BLK, E_AH), lambda b, i, h, s: (prev(b, i, h, s), vc + h)),
            pl.BlockSpec((1, E_AH), lambda b, i, h, s: (0, h)),
        ],
        out_specs=pl.BlockSpec((ATT_BLK, E_AH), lambda b, i, h, s: (cur(b, i, h, s), h)),
        scratch_shapes=[pltpu.VMEM((3, ATT_BLK, E_AH), F32)] * 3,
    )
    return pl.pallas_call(
        _attn_body,
        grid_spec=grid_spec,
        out_shape=jax.ShapeDtypeStruct((batch * seq, N_AH * E_AH), BF16),
        compiler_params=_cparams(("arbitrary", "arbitrary", "arbitrary")),
    )(slopes, xp, xp, xp, xp, xp, gain)


DEC_NEAR = 8


def _near_multiplicity():
    return [sum(1 for win, d in DILATED_CONFIGS if dist % d == 0 and dist <= win) for dist in range(DEC_NEAR)]


def _decode_body(q_ref, kn_ref, vn_ref, kc_ref, vc_ref, slope_ref, gain_ref,
                 ha_ref, ko_hbm, vo_hbm, ktail, vtail, sem, *, w_buf, s_new):
    b = pl.program_id(0)
    keep = w_buf - s_new
    copies = [
        pltpu.make_async_copy(kc_ref.at[0, 0, pl.ds(s_new, keep)], ko_hbm.at[0, b, pl.ds(0, keep)], sem.at[0]),
        pltpu.make_async_copy(vc_ref.at[0, 0, pl.ds(s_new, keep)], vo_hbm.at[0, b, pl.ds(0, keep)], sem.at[1]),
        pltpu.make_async_copy(kn_ref.at[0], ko_hbm.at[0, b, pl.ds(keep, s_new)], sem.at[2]),
        pltpu.make_async_copy(vn_ref.at[0], vo_hbm.at[0, b, pl.ds(keep, s_new)], sem.at[3]),
    ]
    for cp in copies:
        cp.start(priority=1)

    ktail[pl.ds(0, DEC_NEAR)] = kc_ref[0, 0, pl.ds(w_buf - DEC_NEAR, DEC_NEAR)]
    vtail[pl.ds(0, DEC_NEAR)] = vc_ref[0, 0, pl.ds(w_buf - DEC_NEAR, DEC_NEAR)]
    ktail[pl.ds(DEC_NEAR, s_new)] = kn_ref[0]
    vtail[pl.ds(DEC_NEAR, s_new)] = vn_ref[0]

    scale = E_AH ** -0.5
    slope = slope_ref[:, 0:1]
    near_i = lax.broadcasted_iota(jnp.int32, (DEC_NEAR, 1, 1), 0)
    near_dist = (DEC_NEAR - 1 - near_i).astype(F32)
    near_mult = jnp.zeros((DEC_NEAR, 1, 1), F32)
    for dist, c in enumerate(_near_multiplicity()):
        near_mult = jnp.where(near_i == DEC_NEAR - 1 - dist, float(c), near_mult)

    def far_part(s, win, d):
        n = N_BACK - (DEC_NEAR - 1) // d
        sl = pl.ds(w_buf + s - win, n, stride=d) if d > 1 else pl.ds(w_buf + s - win, n)
        i = lax.broadcasted_iota(jnp.int32, (n, 1, 1), 0)
        return sl, ((N_BACK - i) * d).astype(F32)

    ones = jnp.ones((E_AH, E_AH), BF16)

    def row_dots(kk, q):
        n = kk.shape[0]
        prod = (kk * q).astype(BF16).reshape(n * N_AH, E_AH)
        return jnp.dot(prod, ones, preferred_element_type=F32).reshape(n, N_AH, E_AH)

    def one_query(s, _):
        q = q_ref[0, s]
        parts = []
        for win, d in DILATED_CONFIGS:
            sl, dist = far_part(s, win, d)
            sc = row_dots(kc_ref[0, 0, sl], q) * scale - slope * dist
            parts.append((sc, None, lambda sl=sl: vc_ref[0, 0, sl]))
        nsl = pl.ds(s + 1, DEC_NEAR)
        sc = row_dots(ktail[nsl], q) * scale - slope * near_dist
        parts.append((sc, near_mult, lambda: vtail[nsl]))
        m = functools.reduce(jnp.maximum, [jnp.max(p[0], axis=0, keepdims=True) for p in parts])
        den = jnp.zeros((1, N_AH, E_AH), F32)
        o = jnp.zeros((1, N_AH, E_AH), F32)
        for sc, mu, load_v in parts:
            p = jnp.exp(sc - m)
            if mu is not None:
                p = p * mu
            den = den + jnp.sum(p, axis=0, keepdims=True)
            o = o + jnp.sum(p * load_v(), axis=0, keepdims=True)
        o = (o / den)[0]
        o = o * lax.rsqrt(jnp.mean(o * o, axis=-1, keepdims=True) + EPS)
        ha_ref[0, s] = o * gain_ref[...]
        return 0

    lax.fori_loop(0, s_new, one_query, 0)
    for cp in copies:
        cp.wait()


def _attn_decode(q3, kn3, vn3, cache_k, cache_v, gain, slopes, *, batch, s_new):
    w_buf = cache_k.shape[2]
    assert w_buf >= max(w for w, _ in DILATED_CONFIGS) and s_new <= DEC_NEAR
    assert all(w // d == N_BACK and d & (d - 1) == 0 for w, d in DILATED_CONFIGS)
    body = functools.partial(_decode_body, w_buf=w_buf, s_new=s_new)
    new_spec = pl.BlockSpec((1, s_new, N_AH, E_AH), lambda b: (b, 0, 0, 0))
    cache_spec = pl.BlockSpec((1, 1, w_buf, N_AH, E_AH), lambda b: (0, b, 0, 0, 0))
    tile_spec = pl.BlockSpec((N_AH, E_AH), lambda b: (0, 0))
    return pl.pallas_call(
        body,
        grid=(batch,),
        in_specs=[new_spec, new_spec, new_spec, cache_spec, cache_spec, tile_spec, tile_spec],
        out_specs=[new_spec, pl.BlockSpec(memory_space=pl.ANY), pl.BlockSpec(memory_space=pl.ANY)],
        out_shape=[
            jax.ShapeDtypeStruct((batch, s_new, N_AH, E_AH), F32),
            jax.ShapeDtypeStruct(cache_k.shape, F32),
            jax.ShapeDtypeStruct(cache_v.shape, F32),
        ],
        scratch_shapes=[pltpu.VMEM((DEC_NEAR + s_new, N_AH, E_AH), F32)] * 2 + [pltpu.SemaphoreType.DMA((4,))],
        compiler_params=_cparams(("arbitrary",)),
    )(q3, kn3, vn3, cache_k, cache_v, slopes, gain)


OP_TM = 512
ROUTE_W = LANE
PACK_ROWS = D_MODEL // LANE


def _layer_norm(z, g, b):
    mu = jnp.mean(z, axis=1, keepdims=True)
    zc = z - mu
    var = jnp.mean(zc * zc, axis=1, keepdims=True)
    return zc * lax.rsqrt(var + EPS) * g + b


def _outproj_body(hm_ref, ha_ref, x_ref, wm_ref, wa_ref, g_ref, b_ref, wrh_ref, wrl_ref, br_ref,
                  x1_ref, x1p_ref, route_ref):
    y = (jnp.dot(hm_ref[...].astype(BF16), wm_ref[...], preferred_element_type=F32)
         + jnp.dot(ha_ref[...].astype(BF16), wa_ref[...], preferred_element_type=F32))
    x1 = _layer_norm(ALPHA * x_ref[...] + y, g_ref[...], b_ref[...])
    x1_ref[...] = x1
    tm = x1.shape[0]
    for j in range(PACK_ROWS):
        x1p_ref[pl.ds(j, tm, stride=PACK_ROWS), :] = x1[:, j * LANE:(j + 1) * LANE]
    x1_hi = x1.astype(BF16)
    x1_lo = (x1 - x1_hi.astype(F32)).astype(BF16)
    logits = (jnp.dot(x1_hi, wrh_ref[...], preferred_element_type=F32)
              + jnp.dot(x1_lo, wrh_ref[...], preferred_element_type=F32)
              + jnp.dot(x1_hi, wrl_ref[...], preferred_element_type=F32)) + br_ref[...]
    lane = lax.broadcasted_iota(jnp.int32, (tm, ROUTE_W), 1)
    lane_f = lane.astype(F32)
    big = float(ROUTE_W)
    gl = jnp.where(lane < N_GROUPS, logits, NEG_INF)
    gmax = jnp.max(gl, axis=1, keepdims=True)
    g_w = 1.0 / jnp.sum(jnp.exp(gl - gmax), axis=1, keepdims=True)
    g_idx = jnp.min(jnp.where(gl == gmax, lane_f, big), axis=1, keepdims=True)
    lo = N_GROUPS + E_PER_GROUP * g_idx
    el = jnp.where(jnp.logical_and(lane_f >= lo, lane_f < lo + E_PER_GROUP), logits, NEG_INF)
    v1 = jnp.max(el, axis=1, keepdims=True)
    i1 = jnp.min(jnp.where(el == v1, lane_f, big), axis=1, keepdims=True)
    el2 = jnp.where(lane_f == i1, NEG_INF, el)
    v2 = jnp.max(el2, axis=1, keepdims=True)
    i2 = jnp.min(jnp.where(el2 == v2, lane_f, big), axis=1, keepdims=True)
    e2 = jnp.exp(v2 - v1)
    w1 = g_w / (1.0 + e2)
    w2 = g_w * e2 / (1.0 + e2)
    route = jnp.where(lane == 0, i1 - N_GROUPS,
                      jnp.where(lane == 1, i2 - N_GROUPS,
                                jnp.where(lane == 2, w1, jnp.where(lane == 3, w2, 0.0))))
    route_ref[...] = route


def _outproj_two_groups(hmp_ref, hap_ref, xp_ref, hms_ref, has_ref, xs_ref, *rest, steps_p):
    i = pl.program_id(0)

    @pl.when(i < steps_p)
    def _():
        _outproj_body(hmp_ref, hap_ref, xp_ref, *rest)

    @pl.when(i >= steps_p)
    def _():
        _outproj_body(hms_ref, has_ref, xs_ref, *rest)


def _outproj(hm_p, ha_p, x_p, hm_s, ha_s, x_s, wo, g, b, wr_hi, wr_lo, br):
    tm = OP_TM
    n_p, n_s = x_p.shape[0], x_s.shape[0]
    assert n_p % tm == 0 and n_s <= tm
    steps_p = n_p // tm
    n = n_p + tm
    pad = lambda a: jnp.pad(a, ((0, tm - n_s), (0, 0)))
    hm_s, ha_s, x_s = pad(hm_s), pad(ha_s), pad(x_s)
    row = lambda i: (i, 0)
    prow = lambda i: (jnp.minimum(i, steps_p - 1), 0)
    fixed = lambda i: (0, 0)
    once = pl.Buffered(1)
    return pl.pallas_call(
        functools.partial(_outproj_two_groups, steps_p=steps_p),
        grid=(steps_p + 1,),
        in_specs=[
            pl.BlockSpec((tm, D_MLSTM), prow),
            pl.BlockSpec((tm, N_AH * E_AH), prow),
            pl.BlockSpec((tm, D_MODEL), prow),
            pl.BlockSpec((tm, D_MLSTM), fixed),
            pl.BlockSpec((tm, N_AH * E_AH), fixed),
            pl.BlockSpec((tm, D_MODEL), fixed),
            pl.BlockSpec((D_MLSTM, D_MODEL), fixed, pipeline_mode=once),
            pl.BlockSpec((N_AH * E_AH, D_MODEL), lambda i: (1, 0), pipeline_mode=once),
            pl.BlockSpec((1, D_MODEL), fixed),
            pl.BlockSpec((1, D_MODEL), fixed),
            pl.BlockSpec((D_MODEL, ROUTE_W), fixed),
            pl.BlockSpec((D_MODEL, ROUTE_W), fixed),
            pl.BlockSpec((1, ROUTE_W), fixed),
        ],
        out_specs=[pl.BlockSpec((tm, D_MODEL), row), pl.BlockSpec((tm * PACK_ROWS, LANE), row),
                   pl.BlockSpec((tm, ROUTE_W), row)],
        out_shape=[jax.ShapeDtypeStruct((n, D_MODEL), F32), jax.ShapeDtypeStruct((n * PACK_ROWS, LANE), F32),
                   jax.ShapeDtypeStruct((n, ROUTE_W), F32)],
        compiler_params=_cparams(("arbitrary",)),
    )(hm_p, ha_p, x_p, hm_s, ha_s, x_s, wo, wo, g, b, wr_hi, wr_lo, br)


MOE_R = 128
MOE_MAXB = 6
MOE_FC = 256
MOE_NC = D_FF // MOE_FC
MOE_ISSUE_UNROLL = 32


def _moe_body(ex_ref, row0_ref, nb_ref, src_ref, x1p_hbm, wg_ref, wu_ref, wd_ref, ys_hbm,
              ubuf, xbuf, acc, sem_in, sem_out):
    t = pl.program_id(0)
    c = pl.program_id(1)
    n_items = pl.num_programs(0)
    slot = t % 2
    nb = nb_ref[t]

    def rows(base, r):
        return pl.ds(pl.multiple_of(base + r * MOE_R, MOE_R), MOE_R)

    def tile_rows(i):
        return pl.ds(pl.multiple_of(i * PACK_ROWS, PACK_ROWS), PACK_ROWS)

    def out_copy(tt, sl, r):
        return pltpu.make_async_copy(acc.at[sl, rows(0, r), :], ys_hbm.at[rows(row0_ref[tt], r), :], sem_out.at[sl])

    def for_blocks(n, fn):
        def body(r, _):
            fn(r)
            return 0

        lax.fori_loop(0, n, body, 0)

    def n_in(tt):
        return jnp.maximum(nb_ref[tt], 0)

    def n_out(tt):
        return jnp.abs(nb_ref[tt])

    def start_gather(tt, sl):
        base = row0_ref[tt]

        def group(gi):
            for k in range(MOE_ISSUE_UNROLL):
                i = gi * MOE_ISSUE_UNROLL + k
                tok = src_ref[base + i]
                pltpu.make_async_copy(x1p_hbm.at[tile_rows(tok), :], ubuf.at[sl, tile_rows(i), :], sem_in.at[sl]).start()

        for_blocks(n_in(tt) * (MOE_R // MOE_ISSUE_UNROLL), group)

    def wait_gather(tt, sl):
        def block(r):
            span = pl.ds(pl.multiple_of(r * (MOE_R * PACK_ROWS), MOE_R * PACK_ROWS), MOE_R * PACK_ROWS)
            pltpu.make_async_copy(x1p_hbm.at[pl.ds(0, MOE_R * PACK_ROWS), :], ubuf.at[sl, span, :], sem_in.at[sl]).wait()

        for_blocks(n_in(tt), block)

    def unpack_block(r):
        for j in range(PACK_ROWS):
            start = r * (MOE_R * PACK_ROWS) + j
            u = ubuf[slot, pl.ds(start, MOE_R, stride=PACK_ROWS), :]
            xbuf[rows(0, r), j * LANE:(j + 1) * LANE] = u.astype(BF16)

    @pl.when(c == 0)
    def _begin():
        @pl.when(t == 0)
        def _():
            start_gather(0, 0)

        @pl.when(t + 1 < n_items)
        def _():
            start_gather(t + 1, 1 - slot)

        wait_gather(t, slot)
        for_blocks(n_in(t), unpack_block)

        @pl.when(t >= 2)
        def _():
            for_blocks(n_out(t - 2), lambda r: out_copy(t - 2, slot, r).wait())

        @pl.when(nb < 0)
        def _():
            acc[slot] = jnp.zeros(acc.shape[1:], F32)

    @pl.when(nb > 0)
    def _compute():
        wg = wg_ref[...].astype(BF16)
        wu = wu_ref[...].astype(BF16)
        wd = wd_ref[...].astype(BF16)

        def piece(size):
            rs = pl.ds(0, size)
            x = xbuf[rs, :]
            gt = jnp.dot(x, wg, preferred_element_type=F32)
            up = jnp.dot(x, wu, preferred_element_type=F32)
            hid = (gt * jax.nn.sigmoid(gt) * up).astype(BF16)
            y = jnp.dot(hid, wd, preferred_element_type=F32)

            @pl.when(c == 0)
            def _():
                acc[slot, rs, :] = y

            @pl.when(c > 0)
            def _():
                acc[slot, rs, :] = acc[slot, rs, :] + y

        for blocks in range(1, MOE_MAXB + 1):
            @pl.when(nb == blocks)
            def _(blocks=blocks):
                piece(blocks * MOE_R)

    @pl.when(c == MOE_NC - 1)
    def _end():
        for_blocks(n_out(t), lambda r: out_copy(t, slot, r).start())

        @pl.when(t == n_items - 1)
        def _():
            for_blocks(n_out(t), lambda r: out_copy(t, slot, r).wait())

            @pl.when(t >= 1)
            def _():
                for_blocks(n_out(t - 1), lambda r: out_copy(t - 1, 1 - slot, r).wait())


def _moe(item_e, item_row0, item_nb, src, x1p, w_gate, w_up, w_down, n_items, n_rows):
    def chunk(t, c, n):
        return jnp.where(n[t] > 0, c, MOE_NC - 1)

    grid_spec = pltpu.PrefetchScalarGridSpec(
        num_scalar_prefetch=4,
        grid=(n_items, MOE_NC),
        in_specs=[
            pl.BlockSpec(memory_space=pl.ANY),
            pl.BlockSpec((None, D_MODEL, MOE_FC), lambda t, c, e, r, n, s: (e[t], 0, chunk(t, c, n))),
            pl.BlockSpec((None, D_MODEL, MOE_FC), lambda t, c, e, r, n, s: (e[t], 0, chunk(t, c, n))),
            pl.BlockSpec((None, MOE_FC, D_MODEL), lambda t, c, e, r, n, s: (e[t], chunk(t, c, n), 0)),
        ],
        out_specs=pl.BlockSpec(memory_space=pl.ANY),
        scratch_shapes=[
            pltpu.VMEM((2, MOE_MAXB * MOE_R * PACK_ROWS, LANE), F32),
            pltpu.VMEM((MOE_MAXB * MOE_R, D_MODEL), BF16),
            pltpu.VMEM((2, MOE_MAXB * MOE_R, D_MODEL), F32),
            pltpu.SemaphoreType.DMA((2,)),
            pltpu.SemaphoreType.DMA((2,)),
        ],
    )
    return pl.pallas_call(
        _moe_body,
        grid_spec=grid_spec,
        out_shape=jax.ShapeDtypeStruct((n_rows, D_MODEL), F32),
        compiler_params=_cparams(("arbitrary", "arbitrary")),
    )(item_e, item_row0, item_nb, src, x1p, w_gate, w_up, w_down)


COMB_TM = 128
COMB_ISSUE_UNROLL = 32


def _comb_issue(pos_ref, ys_hbm, buf, sem, tile, slot, tm):
    def body(g, _):
        for u in range(COMB_ISSUE_UNROLL):
            r = g * COMB_ISSUE_UNROLL + u
            for k in range(2):
                p = pos_ref[2 * (tile * tm + r) + k]
                pltpu.make_async_copy(ys_hbm.at[pl.ds(p, 1), :], buf.at[slot, k, pl.ds(r, 1), :], sem.at[slot]).start()
        return 0

    lax.fori_loop(0, tm // COMB_ISSUE_UNROLL, body, 0)


def _combine_body(pos_ref, ys_hbm, x1_ref, route_ref, g_ref, b_ref, o_ref, buf, sem, *, tm):
    i = pl.program_id(0)
    n = pl.num_programs(0)
    slot = i % 2

    @pl.when(i == 0)
    def _():
        _comb_issue(pos_ref, ys_hbm, buf, sem, 0, 0, tm)

    @pl.when(i + 1 < n)
    def _():
        _comb_issue(pos_ref, ys_hbm, buf, sem, i + 1, 1 - slot, tm)

    for k in range(2):
        pltpu.make_async_copy(ys_hbm.at[pl.ds(0, tm), :], buf.at[slot, k], sem.at[slot]).wait()
    f = buf[slot, 0] * route_ref[:, 2:3] + buf[slot, 1] * route_ref[:, 3:4]
    o_ref[...] = _layer_norm(ALPHA * x1_ref[...] + f, g_ref[...], b_ref[...])


def _combine(pos, ys, x1, route, g, b, *, row0, n):
    d = x1.shape[1]
    tm = COMB_TM
    assert row0 % tm == 0 and n % tm == 0
    off = row0 // tm
    grid_spec = pltpu.PrefetchScalarGridSpec(
        num_scalar_prefetch=1,
        grid=(n // tm,),
        in_specs=[
            pl.BlockSpec(memory_space=pl.ANY),
            pl.BlockSpec((tm, d), lambda i, p: (i + off, 0)),
            pl.BlockSpec((tm, ROUTE_W), lambda i, p: (i + off, 0)),
            pl.BlockSpec((1, d), lambda i, p: (0, 0)),
            pl.BlockSpec((1, d), lambda i, p: (0, 0)),
        ],
        out_specs=pl.BlockSpec((tm, d), lambda i, p: (i, 0)),
        scratch_shapes=[pltpu.VMEM((2, 2, tm, d), F32), pltpu.SemaphoreType.DMA((2,))],
    )
    return pl.pallas_call(
        functools.partial(_combine_body, tm=tm),
        grid_spec=grid_spec,
        out_shape=jax.ShapeDtypeStruct((n, d), F32),
        compiler_params=_cparams(("arbitrary",)),
    )(pos, ys, x1, route, g, b)


def _dispatch_plan(eid, n_items, n_rows):
    p_total = eid.shape[0]
    blk = 128
    assert p_total % blk == 0
    onehot = (eid[:, None] == jnp.arange(N_EXPERTS, dtype=jnp.int32)[None, :]).astype(F32)
    counts = jnp.sum(onehot, axis=0).astype(jnp.int32)
    ohb = onehot.reshape(p_total // blk, blk, N_EXPERTS)
    earlier = (jnp.arange(blk)[:, None] > jnp.arange(blk)[None, :]).astype(F32)
    within = jnp.einsum("ij,bjk->bik", earlier, ohb, precision=lax.Precision.HIGHEST)
    blk_tot = jnp.sum(ohb, axis=1)
    blk_off = jnp.cumsum(blk_tot, axis=0) - blk_tot
    rank = jnp.sum((within + blk_off[:, None, :]) * ohb, axis=2).reshape(p_total).astype(jnp.int32)
    nblk = (counts + MOE_R - 1) // MOE_R
    seg_start = (jnp.cumsum(nblk) - nblk) * MOE_R
    pos = seg_start[eid] + rank
    src = jnp.zeros((n_rows,), jnp.int32).at[pos].set(jnp.arange(p_total, dtype=jnp.int32) // 2)
    items_per_e = (nblk + MOE_MAXB - 1) // MOE_MAXB
    item_end = jnp.cumsum(items_per_e)
    item_start = item_end - items_per_e
    t = jnp.arange(n_items, dtype=jnp.int32)
    e_t = jnp.minimum(jnp.sum((item_end[None, :] <= t[:, None]).astype(jnp.int32), axis=1), N_EXPERTS - 1)
    live = t < item_end[-1]
    local = t - item_start[e_t]
    used = jnp.sum(nblk)
    idle0 = used + (t - item_end[-1]) * MOE_MAXB
    nz_t = jnp.clip(n_rows // MOE_R - idle0, 0, MOE_MAXB)
    nb_t = jnp.where(live, jnp.clip(nblk[e_t] - local * MOE_MAXB, 0, MOE_MAXB), -nz_t)
    row0_t = jnp.where(live, seg_start[e_t] + local * (MOE_MAXB * MOE_R),
                       jnp.minimum(idle0, n_rows // MOE_R - 1) * MOE_R)
    last_e = e_t[jnp.maximum(item_end[-1] - 1, 0)]
    e_t = jnp.where(live, e_t, last_e)
    return pos.astype(jnp.int32), src, e_t.astype(jnp.int32), row0_t.astype(jnp.int32), nb_t.astype(jnp.int32)


def _alibi_slopes():
    return jnp.asarray([2.0 ** (-8.0 * (h + 1) / N_AH) for h in range(N_AH)], dtype=F32)


def kernel(x_prompt, x_sample, state_conv, state_mlstm_C, state_mlstm_n, state_mlstm_m, cache_win_k, cache_win_v, w_in, b_gate, conv_w, conv_b, mh_gain, att_gain, w_out, ln1_g, ln1_b, w_group, b_group, w_router, b_router, w_gate, w_up, w_down, ln2_g, ln2_b):
    bp, tp, d = x_prompt.shape
    bs, ts, _ = x_sample.shape
    assert d == D_MODEL and w_in.shape[0] == 1 and tp % ATT_BLK == 0 and ts >= CONV_W - 1
    w_buf = cache_win_k.shape[2]
    n_p, n_s = bp * tp, bs * ts
    slopes = _alibi_slopes()

    wi = w_in[0]
    g0 = 4 * D_MLSTM
    g1 = g0 + 2 * N_MH
    w_pack = jnp.concatenate(
        [wi[:, :g0], wi[:, g1:], wi[:, g0:g1], jnp.zeros((d, LANE - 2 * N_MH), F32)], axis=1).astype(BF16)

    xp2 = x_prompt.reshape(n_p, d)
    xs2 = x_sample.reshape(n_s, d)
    proj_p = _in_proj(xp2, w_pack, 512)
    proj_s = _in_proj(xs2, w_pack, n_s)

    cw = conv_w[0]
    cb = conv_b[0][None, :]
    mh_g = mh_gain[0][None, :]
    att_g = att_gain[0][None, :]
    bg = b_gate[0]

    def gates_time_major(proj, batch, seq, pad_to):
        gt = proj[:, COL_G:COL_G + 2 * N_MH].reshape(batch, seq, 2, N_MH).transpose(0, 3, 2, 1)
        if pad_to > seq:
            gt = jnp.pad(gt, ((0, 0), (0, 0), (0, 0), (0, pad_to - seq)))
        return gt

    lc_p = 256
    hm_p, c_p, n_pp, m_p = _mlstm(
        proj_p, gates_time_major(proj_p, bp, tp, tp), jnp.zeros((bp, CONV_W - 1, 2 * D_MLSTM), F32), cw, cb,
        jnp.zeros((bp, N_MH, E_MH, E_MH), F32), jnp.zeros((bp, N_MH, 1, E_MH), F32),
        jnp.zeros((bp, N_MH, 1, 1), F32), mh_g, bg, batch=bp, seq=tp, lb=lc_p, lc=lc_p, hb=N_MH, out_dtype=BF16)
    ha_p = _attn_prompt(proj_p, att_g, slopes, batch=bp, seq=tp)

    lc_s = 16
    hm_s, c_s, n_ss, m_s = _mlstm(
        proj_s, gates_time_major(proj_s, bs, ts, lc_s), state_conv[0], cw, cb,
        state_mlstm_C[0], state_mlstm_n[0][:, :, None, :], state_mlstm_m[0][:, :, None, None],
        mh_g, bg, batch=bs, seq=ts, lb=ts, lc=lc_s, hb=N_MH, out_dtype=F32)
    new_rows = lambda col: proj_s[:, col:col + N_AH * E_AH].reshape(bs, ts, N_AH, E_AH)
    ha_s, wk_s, wv_s = _attn_decode(
        new_rows(COL_QA), new_rows(COL_KA), new_rows(COL_VA), cache_win_k, cache_win_v,
        att_gain[0].reshape(N_AH, E_AH), jnp.broadcast_to(slopes[:, None], (N_AH, E_AH)), batch=bs, s_new=ts)
    ha_s = ha_s.reshape(n_s, N_AH * E_AH)

    n_all = n_p + n_s
    wo = w_out[0].astype(BF16)
    w_r = jnp.concatenate(
        [w_group[0], w_router[0].transpose(1, 0, 2).reshape(d, N_EXPERTS),
         jnp.zeros((d, ROUTE_W - N_GROUPS - N_EXPERTS), F32)], axis=1)
    b_r = jnp.concatenate(
        [b_group[0], b_router[0].reshape(N_EXPERTS), jnp.zeros((ROUTE_W - N_GROUPS - N_EXPERTS,), F32)])[None, :]
    wr_hi = w_r.astype(BF16)
    wr_lo = (w_r - wr_hi.astype(F32)).astype(BF16)
    ln1 = (ln1_g[0][None, :], ln1_b[0][None, :])
    x1, x1p, route = _outproj(hm_p, ha_p, xp2, hm_s, ha_s, xs2, wo, *ln1, wr_hi, wr_lo, b_r)

    p_total = 2 * n_all
    n_rows = ((p_total + N_EXPERTS * (MOE_R - 1)) // MOE_R + 1) * MOE_R
    n_items = N_EXPERTS + n_rows // (MOE_R * MOE_MAXB)
    eid = route[:n_all, 0:2].astype(jnp.int32).reshape(p_total)
    pos, src, item_e, item_row0, item_nb = _dispatch_plan(eid, n_items, n_rows)
    ys = _moe(item_e, item_row0, item_nb, src, x1p, w_gate[0], w_up[0], w_down[0], n_items, n_rows)
    ln2 = (ln2_g[0][None, :], ln2_b[0][None, :])
    y_p = _combine(pos[:2 * n_p], ys, x1, route, *ln2, row0=0, n=n_p).reshape(bp, tp, d)
    y_s = _combine(pos[2 * n_p:], ys, x1, route, *ln2, row0=n_p, n=n_s).reshape(bs, ts, d)

    def tail_rows(proj, batch, seq, col, width, rows):
        return proj.reshape(batch, seq, N_PROJ)[:, seq - rows:, col:col + width]

    win = min(w_buf, tp)
    p_conv = tail_rows(proj_p, bp, tp, COL_QM, 2 * D_MLSTM, CONV_W - 1)[None]
    p_wk = tail_rows(proj_p, bp, tp, COL_KA, N_AH * E_AH, win).reshape(1, bp, win, N_AH, E_AH)
    p_wv = tail_rows(proj_p, bp, tp, COL_VA, N_AH * E_AH, win).reshape(1, bp, win, N_AH, E_AH)
    s_conv = tail_rows(proj_s, bs, ts, COL_QM, 2 * D_MLSTM, CONV_W - 1)[None]
    return (y_p, y_s,
            p_conv, c_p[None], n_pp[:, :, 0, :][None], m_p[:, :, 0, 0][None], p_wk, p_wv,
            s_conv, c_s[None], n_ss[:, :, 0, :][None], m_s[:, :, 0, 0][None], wk_s, wv_s)
```
